```python
import jax, jax.numpy as jnp
from jax import lax
import numpy as np

D_MODEL = 1024
BATCH = 2
SEQ = 8192
DEPTH = 1

EPS = 1e-6
A_HEAD_DIM = 64
A_HEADS_PER_GROUP = 4
A_GROUPS = ((128, 1), (512, 4), (2048, 16))
A_HEADS = A_HEADS_PER_GROUP * len(A_GROUPS)
A_WIDTH = A_HEADS * A_HEAD_DIM
A_OUT = A_HEADS_PER_GROUP * A_HEAD_DIM
A_ROT_DIM = A_HEAD_DIM // 4
A_ROPE_THETA = 500000.0
B_HEADS = 8
B_NOPE = 64
B_ROPE = 32
B_QK = B_NOPE + B_ROPE
B_V = 64
B_Q_RANK = 512
B_KV_RANK = 256
B_ROPE_THETA = 10000.0
B_BLOCK = 128
B_OUT = B_HEADS * B_V
D_FF = 4 * D_MODEL
IN_A = 3 * A_WIDTH
IN_B = B_Q_RANK + B_KV_RANK + B_ROPE
IN_G = 2 * D_MODEL
IN_TOTAL = IN_A + IN_B + IN_G

kernel_name = "hybrid_dilated_swa_mla_block"


def rms_norm(x, g):
    xf = x.astype(jnp.float32)
    y = xf * lax.rsqrt(jnp.mean(xf * xf, axis=-1, keepdims=True) + EPS)
    return (y * g.astype(jnp.float32)).astype(x.dtype)


def rotary(x, pos, theta, rot_dim):
    half = rot_dim // 2
    inv = jnp.float32(theta) ** (-(jnp.arange(half, dtype=jnp.float32) * 2.0 / rot_dim))
    ang = pos.astype(jnp.float32)[:, None] * inv[None, :]
    cos = jnp.cos(ang)[:, None, :]
    sin = jnp.sin(ang)[:, None, :]
    xr = x[..., :rot_dim].astype(jnp.float32)
    x1, x2 = xr[..., :half], xr[..., half:]
    rot = jnp.concatenate([x1 * cos - x2 * sin, x2 * cos + x1 * sin], axis=-1).astype(x.dtype)
    return jnp.concatenate([rot, x[..., rot_dim:]], axis=-1)


def banded_attention(q, k, v, radius):
    N, L, H, D = q.shape
    C = radius
    nb = -(-L // C)
    Lp = nb * C
    qb = jnp.pad(q, ((0, 0), (0, Lp - L), (0, 0), (0, 0))).reshape(N, nb, C, H, D)
    pad_kv = ((0, 0), (C, Lp - L + C), (0, 0), (0, 0))
    kp = jnp.pad(k, pad_kv).reshape(N, nb + 2, C, H, D)
    vp = jnp.pad(v, pad_kv).reshape(N, nb + 2, C, H, D)
    kb = jnp.concatenate([kp[:, :-2], kp[:, 1:-1], kp[:, 2:]], axis=2)
    vb = jnp.concatenate([vp[:, :-2], vp[:, 1:-1], vp[:, 2:]], axis=2)
    q_idx = jnp.arange(nb)[:, None] * C + jnp.arange(C)[None, :]
    k_idx = jnp.arange(nb)[:, None] * C - C + jnp.arange(3 * C)[None, :]
    dist = q_idx[:, :, None] - k_idx[:, None, :]
    valid = (jnp.abs(dist) <= radius) & (k_idx[:, None, :] >= 0) & (k_idx[:, None, :] < L)
    s = jnp.einsum('nbqhd,nbkhd->nbhqk', qb, kb).astype(jnp.float32) * (D ** -0.5)
    s = jnp.where(valid[None, :, None], s, -jnp.inf)
    m = jnp.max(s, axis=-1, keepdims=True)
    p = jnp.exp(s - m)
    l = jnp.sum(p, axis=-1, keepdims=True)
    o = jnp.einsum('nbhqk,nbkhd->nbqhd', (p / l).astype(v.dtype), vb)
    lse = (m + jnp.log(l))[..., 0]
    o = o.reshape(N, Lp, H, D)[:, :L]
    lse = lse.transpose(0, 1, 3, 2).reshape(N, Lp, H)[:, :L]
    return o, lse


def dilated_group(q, k, v, window, dilation):
    B, S, H, D = q.shape
    L = S // dilation
    radius = window // (2 * dilation)

    def to_sub(t):
        return t.reshape(B, L, dilation, H, D).transpose(0, 2, 1, 3, 4).reshape(B * dilation, L, H, D)

    o, lse = banded_attention(to_sub(q), to_sub(k), to_sub(v), radius)
    o = o.reshape(B, dilation, L, H, D).transpose(0, 2, 1, 3, 4).reshape(B, S, H, D)
    lse = lse.reshape(B, dilation, L, H).transpose(0, 2, 1, 3).reshape(B, S, H)
    return o, lse


def mixer_dilated(qkv, pos):
    B, S, _ = qkv.shape
    q, k, v = jnp.split(qkv, 3, axis=-1)
    q = rotary(q.reshape(B, S, A_HEADS, A_HEAD_DIM), pos, A_ROPE_THETA, A_ROT_DIM)
    k = rotary(k.reshape(B, S, A_HEADS, A_HEAD_DIM), pos, A_ROPE_THETA, A_ROT_DIM)
    v = v.reshape(B, S, A_HEADS, A_HEAD_DIM)
    outs, lses = [], []
    for g, (window, dilation) in enumerate(A_GROUPS):
        hs = slice(g * A_HEADS_PER_GROUP, (g + 1) * A_HEADS_PER_GROUP)
        o, lse = dilated_group(q[:, :, hs], k[:, :, hs], v[:, :, hs], window, dilation)
        outs.append(o)
        lses.append(lse)
    o = jnp.stack(outs, axis=0)
    w = jax.nn.softmax(jnp.stack(lses, axis=0), axis=0)
    o = jnp.sum(w[..., None].astype(o.dtype) * o, axis=0)
    return o.reshape(B, S, A_OUT)


def mixer_mla(q_c, kv_c, k_pe, pos, q_norm, w_uq, kv_norm, w_ukv):
    B, S, _ = q_c.shape
    q = (rms_norm(q_c, q_norm) @ w_uq).reshape(B, S, B_HEADS, B_QK)
    q = jnp.concatenate([q[..., :B_NOPE], rotary(q[..., B_NOPE:], pos, B_ROPE_THETA, B_ROPE)], axis=-1)
    kv = (rms_norm(kv_c, kv_norm) @ w_ukv).reshape(B, S, B_HEADS, B_NOPE + B_V)
    k_nope, v = kv[..., :B_NOPE], kv[..., B_NOPE:]
    k_pe = rotary(k_pe[:, :, None, :], pos, B_ROPE_THETA, B_ROPE)
    k = jnp.concatenate([k_nope, jnp.broadcast_to(k_pe, (B, S, B_HEADS, B_ROPE))], axis=-1)
    k = k.transpose(0, 2, 1, 3)
    v = v.transpose(0, 2, 1, 3)
    nb = S // B_BLOCK
    q_blocks = q.transpose(0, 2, 1, 3).reshape(B, B_HEADS, nb, B_BLOCK, B_QK).transpose(2, 0, 1, 3, 4)
    scale = B_QK ** -0.5

    def block(qb):
        s = jnp.einsum('bhqd,bhkd->bhqk', qb, k).astype(jnp.float32) * scale
        p = jax.nn.softmax(s, axis=-1)
        return jnp.einsum('bhqk,bhkd->bhqd', p.astype(v.dtype), v)

    o = lax.map(block, q_blocks)
    return o.transpose(1, 0, 3, 2, 4).reshape(B, S, B_OUT)


def setup_inputs(seed: int = 0) -> dict:
    key = jax.random.key(seed)
    ks = jax.random.split(key, 18)
    f = jnp.float32

    def w(k, shape):
        return jax.random.normal(k, shape, f) * (shape[0] ** -0.5)

    def gain(k, n):
        return 1.0 + 0.05 * jax.random.normal(k, (n,), f)

    return {
        "x": jax.random.normal(ks[0], (BATCH, SEQ, D_MODEL), f),
        "norm_mix_pre": gain(ks[1], D_MODEL),
        "w_in": w(ks[2], (D_MODEL, IN_TOTAL)),
        "b_gate": 0.02 * jax.random.normal(ks[3], (IN_G,), f),
        "mla_q_norm": gain(ks[4], B_Q_RANK),
        "mla_w_uq": w(ks[5], (B_Q_RANK, B_HEADS * B_QK)),
        "mla_kv_norm": gain(ks[6], B_KV_RANK),
        "mla_w_ukv": w(ks[7], (B_KV_RANK, B_HEADS * (B_NOPE + B_V))),
        "w_o_a": w(ks[8], (A_OUT, D_MODEL)),
        "w_o_b": w(ks[9], (B_OUT, D_MODEL)),
        "w_out": w(ks[10], (D_MODEL, D_MODEL)),
        "norm_mix_post": gain(ks[11], D_MODEL),
        "norm_mlp_pre": gain(ks[12], D_MODEL),
        "w_ff1": w(ks[13], (D_MODEL, D_FF)),
        "w_ff2": w(ks[14], (D_FF, D_MODEL)),
        "norm_mlp_post": gain(ks[15], D_MODEL),
    }


def reference(x, norm_mix_pre, w_in, b_gate, mla_q_norm, mla_w_uq, mla_kv_norm, mla_w_ukv,
              w_o_a, w_o_b, w_out, norm_mix_post, norm_mlp_pre, w_ff1, w_ff2, norm_mlp_post):
    B, S, D = x.shape
    pos = jnp.arange(S, dtype=jnp.int32)
    h = x
    for _ in range(DEPTH):
        xn = rms_norm(h, norm_mix_pre)
        proj = xn @ w_in
        qkv_a = proj[..., :IN_A]
        q_c = proj[..., IN_A:IN_A + B_Q_RANK]
        kv_c = proj[..., IN_A + B_Q_RANK:IN_A + B_Q_RANK + B_KV_RANK]
        k_pe = proj[..., IN_A + B_Q_RANK + B_KV_RANK:IN_A + IN_B]
        gates = jax.nn.sigmoid(proj[..., IN_A + IN_B:] + b_gate)
        g_a, g_b = gates[..., :D_MODEL], gates[..., D_MODEL:]
        o_a = mixer_dilated(qkv_a, pos) @ w_o_a
        o_b = mixer_mla(q_c, kv_c, k_pe, pos, mla_q_norm, mla_w_uq, mla_kv_norm, mla_w_ukv) @ w_o_b
        mix = (g_a * o_a + g_b * o_b) @ w_out
        h = h + rms_norm(mix, norm_mix_post)
        hn = rms_norm(h, norm_mlp_pre)
        ff = jnp.square(jax.nn.relu(hn @ w_ff1)) @ w_ff2
        h = h + rms_norm(ff, norm_mlp_post)
    return h
```

```python
import functools

import jax
import jax.numpy as jnp
import numpy as np
from jax import lax
from jax.experimental import pallas as pl
from jax.experimental.pallas import tpu as pltpu

EPS = 1e-6
LANES = 128

A_HEAD_DIM = 64
A_HEADS_PER_GROUP = 4
A_GROUPS = ((128, 1), (512, 4), (2048, 16))
A_HEADS = A_HEADS_PER_GROUP * len(A_GROUPS)
A_WIDTH = A_HEADS * A_HEAD_DIM
A_GROUP_WIDTH = A_HEADS_PER_GROUP * A_HEAD_DIM
A_ROT_DIM = A_HEAD_DIM // 4
A_ROPE_THETA = 500000.0
A_TILE = 128
B_HEADS = 8
B_NOPE = 64
B_ROPE = 32
B_QK = B_NOPE + B_ROPE
B_V = 64
B_Q_RANK = 512
B_KV_RANK = 256
B_ROPE_THETA = 10000.0
B_WIDTH = B_HEADS * LANES

NEG_BIG = -1e30
VMEM_LIMIT = 56 * 1024 * 1024


def _rms(x, g):
    return x * lax.rsqrt(jnp.mean(x * x, axis=-1, keepdims=True) + EPS) * g


def _rotate_block(xb, cos, sin_signed, first_half, half):
    up = pltpu.roll(xb, LANES - half, 1)
    down = pltpu.roll(xb, half, 1)
    return xb * cos + jnp.where(first_half, up, down) * sin_signed


_C_QA, _C_KA, _C_VA = 0, A_WIDTH, 2 * A_WIDTH
_C_QC = 3 * A_WIDTH
_C_KVC = _C_QC + B_Q_RANK
_C_KPE = _C_KVC + B_KV_RANK
_C_GATE = _C_KPE + LANES


def _in_proj_kernel(x_ref, g_ref, w_ref, bg_ref, qn_ref, wuq_ref, kvn_ref, wuk_ref, wuv_ref,
                    ca_ref, sa_ref, cb_ref, sb_ref,
                    qa_o, ka_o, va_o, qb_o, kb_o, vb_o, gate_o, *, d_model):
    xn = _rms(x_ref[...], g_ref[...]).astype(jnp.bfloat16)

    def proj(lo, hi):
        return jnp.dot(xn, w_ref[:, lo:hi], preferred_element_type=jnp.float32)

    lane = lax.broadcasted_iota(jnp.int32, (1, LANES), 1)
    first_a = (lane % A_HEAD_DIM) < (A_ROT_DIM // 2)
    first_b = lane < (B_NOPE + B_ROPE // 2)
    ca, sa, cb, sb = ca_ref[...], sa_ref[...], cb_ref[...], sb_ref[...]

    qa = proj(_C_QA, _C_QA + A_WIDTH)
    ka = proj(_C_KA, _C_KA + A_WIDTH)
    a_scale = A_HEAD_DIM ** -0.5
    for j in range(A_WIDTH // LANES):
        blk = slice(j * LANES, (j + 1) * LANES)
        qa_o[:, blk] = (_rotate_block(qa[:, blk], ca, sa, first_a, A_ROT_DIM // 2) * a_scale).astype(qa_o.dtype)
        ka_o[:, blk] = _rotate_block(ka[:, blk], ca, sa, first_a, A_ROT_DIM // 2).astype(ka_o.dtype)
    va_o[...] = proj(_C_VA, _C_VA + A_WIDTH).astype(va_o.dtype)

    qcn = _rms(proj(_C_QC, _C_QC + B_Q_RANK), qn_ref[...]).astype(jnp.bfloat16)
    qb = jnp.dot(qcn, wuq_ref[...], preferred_element_type=jnp.float32)
    kvn = _rms(proj(_C_KVC, _C_KVC + B_KV_RANK), kvn_ref[...]).astype(jnp.bfloat16)
    kb = jnp.dot(kvn, wuk_ref[...], preferred_element_type=jnp.float32)
    vb = jnp.dot(kvn, wuv_ref[...], preferred_element_type=jnp.float32)
    kpe = _rotate_block(proj(_C_KPE, _C_KPE + LANES), cb, sb, first_b, B_ROPE // 2)
    b_scale = B_QK ** -0.5
    for h in range(B_HEADS):
        blk = slice(h * LANES, (h + 1) * LANES)
        qb_o[:, blk] = (_rotate_block(qb[:, blk], cb, sb, first_b, B_ROPE // 2) * b_scale).astype(qb_o.dtype)
        kb_o[:, blk] = (kb[:, blk] + kpe).astype(kb_o.dtype)
        vb_o[:, blk] = jnp.where(lane < B_V, vb[:, blk], 1.0).astype(vb_o.dtype)

    gate_o[...] = jax.nn.sigmoid(proj(_C_GATE, _C_GATE + 2 * d_model) + bg_ref[...]).astype(gate_o.dtype)


def _in_proj(x2, g, w1, bg, qn, wuq, kvn, wuk, wuv, ca, sa, cb, sb, *, seq, tm):
    t, d_model = x2.shape
    n1 = w1.shape[1]
    steps_per_seq = seq // tm
    row = lambda i: (i, 0)
    const = lambda i: (0, 0)
    pos = lambda i: (i % steps_per_seq, 0)
    bf = jnp.bfloat16
    out_shape = (
        jax.ShapeDtypeStruct((t, A_WIDTH), bf), jax.ShapeDtypeStruct((t, A_WIDTH), bf),
        jax.ShapeDtypeStruct((t, A_WIDTH), bf),
        jax.ShapeDtypeStruct((t, B_WIDTH), bf), jax.ShapeDtypeStruct((t, B_WIDTH), bf),
        jax.ShapeDtypeStruct((t, B_WIDTH), bf),
        jax.ShapeDtypeStruct((t, 2 * d_model), bf),
    )
    return pl.pallas_call(
        functools.partial(_in_proj_kernel, d_model=d_model),
        grid=(t // tm,),
        in_specs=[
            pl.BlockSpec((tm, d_model), row),
            pl.BlockSpec((1, d_model), const),
            pl.BlockSpec((d_model, n1), const),
            pl.BlockSpec((1, 2 * d_model), const),
            pl.BlockSpec((1, B_Q_RANK), const),
            pl.BlockSpec((B_Q_RANK, B_WIDTH), const),
            pl.BlockSpec((1, B_KV_RANK), const),
            pl.BlockSpec((B_KV_RANK, B_WIDTH), const),
            pl.BlockSpec((B_KV_RANK, B_WIDTH), const),
            pl.BlockSpec((tm, LANES), pos), pl.BlockSpec((tm, LANES), pos),
            pl.BlockSpec((tm, LANES), pos), pl.BlockSpec((tm, LANES), pos),
        ],
        out_specs=(
            pl.BlockSpec((tm, A_WIDTH), row), pl.BlockSpec((tm, A_WIDTH), row),
            pl.BlockSpec((tm, A_WIDTH), row),
            pl.BlockSpec((tm, B_WIDTH), row), pl.BlockSpec((tm, B_WIDTH), row),
            pl.BlockSpec((tm, B_WIDTH), row),
            pl.BlockSpec((tm, 2 * d_model), row),
        ),
        out_shape=out_shape,
        compiler_params=pltpu.CompilerParams(dimension_semantics=("parallel",), vmem_limit_bytes=VMEM_LIMIT),
        name="in_proj",
    )(x2, g, w1, bg, qn, wuq, kvn, wuk, wuv, ca, sa, cb, sb)


def _dilated_kernel(q_ref, k_ref, v_ref, o_ref, lse_ref, *, length, radius):
    win = A_TILE + 2 * radius
    lane = lax.broadcasted_iota(jnp.int32, (1, LANES), 1)
    head0 = lane < A_HEAD_DIM
    row = lax.broadcasted_iota(jnp.int32, (A_TILE, win), 0)
    col = lax.broadcasted_iota(jnp.int32, (A_TILE, win), 1)

    def tile(i, carry):
        l0 = pl.multiple_of(i * A_TILE, A_TILE)
        ws = pl.multiple_of(jnp.clip(l0 - radius, 0, length - win), radius)
        q = q_ref[0, pl.ds(l0, A_TILE), :]
        k = k_ref[0, pl.ds(ws, win), :]
        v = v_ref[0, pl.ds(ws, win), :]
        valid = jnp.abs((l0 + row) - (ws + col)) <= radius
        outs, lses = [], []
        for h in range(2):
            qh = jnp.where(head0 if h == 0 else ~head0, q, jnp.zeros_like(q))
            s = lax.dot_general(qh, k, (((1,), (1,)), ((), ())), preferred_element_type=jnp.float32)
            s = jnp.where(valid, s, NEG_BIG)
            m = jnp.max(s, axis=-1, keepdims=True)
            p = jnp.exp(s - m)
            l = jnp.sum(p, axis=-1, keepdims=True)
            o = jnp.dot(p.astype(v.dtype), v, preferred_element_type=jnp.float32)
            outs.append(o / l)
            lses.append(m + jnp.log(l))
        o_ref[0, pl.ds(l0, A_TILE), :] = jnp.where(head0, outs[0], outs[1]).astype(o_ref.dtype)
        lse_ref[0, pl.ds(l0, A_TILE), :] = jnp.where(head0, lses[0], lses[1])
        return carry

    lax.fori_loop(0, length // A_TILE, tile, 0)


def _dilated_group(qa, ka, va, *, batch, seq, group, window, dilation):
    length = seq // dilation
    radius = window // (2 * dilation)
    blocks_in = A_WIDTH // LANES
    blocks_out = A_GROUP_WIDTH // LANES
    view_in = lambda a: a.reshape(batch, length, dilation * A_WIDTH)
    in_map = lambda b, c: (b, 0, (c // blocks_out) * blocks_in + group * blocks_out + c % blocks_out)
    out_map = lambda b, c: (b, 0, c)
    in_spec = pl.BlockSpec((1, length, LANES), in_map)
    out_spec = pl.BlockSpec((1, length, LANES), out_map)
    o, lse = pl.pallas_call(
        functools.partial(_dilated_kernel, length=length, radius=radius),
        grid=(batch, dilation * blocks_out),
        in_specs=[in_spec, in_spec, in_spec],
        out_specs=(out_spec, out_spec),
        out_shape=(jax.ShapeDtypeStruct((batch, length, dilation * A_GROUP_WIDTH), jnp.bfloat16),
                   jax.ShapeDtypeStruct((batch, length, dilation * A_GROUP_WIDTH), jnp.float32)),
        compiler_params=pltpu.CompilerParams(dimension_semantics=("parallel", "parallel"),
                                             vmem_limit_bytes=VMEM_LIMIT),
        name=f"dilated_g{group}",
    )(view_in(qa), view_in(ka), view_in(va))
    return o.reshape(batch * seq, A_GROUP_WIDTH), lse.reshape(batch * seq, A_GROUP_WIDTH)


def _mla_kernel(q_ref, k_ref, v_ref, o_ref, *, seq, tk):
    tq = q_ref.shape[1]
    lane = lax.broadcasted_iota(jnp.int32, (1, LANES), 1)
    outs = []
    for h in range(2):
        blk = slice(h * LANES, (h + 1) * LANES)
        q = q_ref[0, :, blk]

        def body(j, carry, blk=blk, q=q):
            m, acc = carry
            k0 = pl.multiple_of(j * tk, tk)
            k = k_ref[0, pl.ds(k0, tk), blk]
            v = v_ref[0, pl.ds(k0, tk), blk]
            s = lax.dot_general(q, k, (((1,), (1,)), ((), ())), preferred_element_type=jnp.float32)
            m_new = jnp.maximum(m, jnp.max(s, axis=-1, keepdims=True))
            p = jnp.exp(s - m_new).astype(v.dtype)
            acc = jnp.exp(m - m_new) * acc + jnp.dot(p, v, preferred_element_type=jnp.float32)
            return m_new, acc

        m0 = jnp.full((tq, 1), NEG_BIG, jnp.float32)
        acc0 = jnp.zeros((tq, LANES), jnp.float32)
        _, acc = lax.fori_loop(0, seq // tk, body, (m0, acc0))
        outs.append(acc / pltpu.roll(acc, LANES - B_V, 1))
    o_ref[0] = jnp.where(lane < B_V, outs[0], pltpu.roll(outs[1], B_V, 1)).astype(o_ref.dtype)


def _mla_attention(qb, kb, vb, *, batch, seq, tq, tk):
    pairs = B_HEADS // 2
    return pl.pallas_call(
        functools.partial(_mla_kernel, seq=seq, tk=tk),
        grid=(batch, pairs, seq // tq),
        in_specs=[
            pl.BlockSpec((1, tq, 2 * LANES), lambda b, hp, i: (b, i, hp)),
            pl.BlockSpec((1, seq, 2 * LANES), lambda b, hp, i: (b, 0, hp)),
            pl.BlockSpec((1, seq, 2 * LANES), lambda b, hp, i: (b, 0, hp)),
        ],
        out_specs=pl.BlockSpec((1, tq, LANES), lambda b, hp, i: (b, i, hp)),
        out_shape=jax.ShapeDtypeStruct((batch, seq, B_HEADS * B_V), jnp.bfloat16),
        compiler_params=pltpu.CompilerParams(dimension_semantics=("parallel", "parallel", "parallel"),
                                             vmem_limit_bytes=VMEM_LIMIT),
        name="mla_attn",
    )(qb, kb, vb)


def _mix_out_kernel(x_ref, o0_ref, o1_ref, o2_ref, l0_ref, l1_ref, l2_ref, ob_ref, gate_ref,
                    woa_ref, wob_ref, wout_ref, gpost_ref, gpre_ref, h_o, hn_o, *, d_model):
    lses = [l0_ref[...], l1_ref[...], l2_ref[...]]
    outs = [o0_ref[...], o1_ref[...], o2_ref[...]]
    top = jnp.maximum(jnp.maximum(lses[0], lses[1]), lses[2])
    es = [jnp.exp(l - top) for l in lses]
    num = es[0] * outs[0] + es[1] * outs[1] + es[2] * outs[2]
    a = (num / (es[0] + es[1] + es[2])).astype(jnp.bfloat16)
    o_a = jnp.dot(a, woa_ref[...], preferred_element_type=jnp.float32)
    o_b = jnp.dot(ob_ref[...], wob_ref[...], preferred_element_type=jnp.float32)
    gates = gate_ref[...].astype(jnp.float32)
    mixin = (gates[:, :d_model] * o_a + gates[:, d_model:] * o_b).astype(jnp.bfloat16)
    mix = jnp.dot(mixin, wout_ref[...], preferred_element_type=jnp.float32)
    h = x_ref[...] + _rms(mix, gpost_ref[...])
    h_o[...] = h
    hn_o[...] = _rms(h, gpre_ref[...]).astype(hn_o.dtype)


def _mix_out(x2, oa, lse, ob, gates, woa, wob, wout, gpost, gpre, *, tm):
    t, d_model = x2.shape
    row = lambda i: (i, 0)
    const = lambda i: (0, 0)
    a_spec = pl.BlockSpec((tm, A_GROUP_WIDTH), row)
    return pl.pallas_call(
        functools.partial(_mix_out_kernel, d_model=d_model),
        grid=(t // tm,),
        in_specs=[
            pl.BlockSpec((tm, d_model), row),
            a_spec, a_spec, a_spec, a_spec, a_spec, a_spec,
            pl.BlockSpec((tm, B_HEADS * B_V), row),
            pl.BlockSpec((tm, 2 * d_model), row),
            pl.BlockSpec((A_GROUP_WIDTH, d_model), const),
            pl.BlockSpec((B_HEADS * B_V, d_model), const),
            pl.BlockSpec((d_model, d_model), const),
            pl.BlockSpec((1, d_model), const),
            pl.BlockSpec((1, d_model), const),
        ],
        out_specs=(pl.BlockSpec((tm, d_model), row), pl.BlockSpec((tm, d_model), row)),
        out_shape=(jax.ShapeDtypeStruct((t, d_model), jnp.float32),
                   jax.ShapeDtypeStruct((t, d_model), jnp.bfloat16)),
        compiler_params=pltpu.CompilerParams(dimension_semantics=("parallel",), vmem_limit_bytes=VMEM_LIMIT),
        name="mix_out",
    )(x2, *oa, *lse, ob, gates, woa, wob, wout, gpost, gpre)


def _ffn_kernel(h_ref, hn_ref, w1_ref, w2_ref, g_ref, out_o):
    u = jnp.dot(hn_ref[...], w1_ref[...], preferred_element_type=jnp.float32)
    u = jnp.square(jnp.maximum(u, 0.0)).astype(jnp.bfloat16)
    ff = jnp.dot(u, w2_ref[...], preferred_element_type=jnp.float32)
    out_o[...] = h_ref[...] + _rms(ff, g_ref[...])


def _ffn(h, hn, w1, w2, g, *, tm):
    t, d_model = h.shape
    d_ff = w1.shape[1]
    row = lambda i: (i, 0)
    const = lambda i: (0, 0)
    return pl.pallas_call(
        _ffn_kernel,
        grid=(t // tm,),
        in_specs=[
            pl.BlockSpec((tm, d_model), row), pl.BlockSpec((tm, d_model), row),
            pl.BlockSpec((d_model, d_ff), const), pl.BlockSpec((d_ff, d_model), const),
            pl.BlockSpec((1, d_model), const),
        ],
        out_specs=pl.BlockSpec((tm, d_model), row),
        out_shape=jax.ShapeDtypeStruct((t, d_model), jnp.float32),
        compiler_params=pltpu.CompilerParams(dimension_semantics=("parallel",), vmem_limit_bytes=VMEM_LIMIT),
        name="ffn",
    )(h, hn, w1, w2, g)


def _rotary_tables(seq, theta, rot_dim, period, offset):
    half = rot_dim // 2
    inv = jnp.float32(theta) ** (-(jnp.arange(half, dtype=jnp.float32) * 2.0 / rot_dim))
    ang = jnp.arange(seq, dtype=jnp.int32).astype(jnp.float32)[:, None] * inv[None, :]
    cos, sin = jnp.cos(ang), jnp.sin(ang)
    cos_p = jnp.ones((seq, period), jnp.float32).at[:, offset:offset + rot_dim].set(jnp.concatenate([cos, cos], 1))
    sin_p = jnp.zeros((seq, period), jnp.float32).at[:, offset:offset + rot_dim].set(jnp.concatenate([-sin, sin], 1))
    reps = LANES // period
    return jnp.tile(cos_p, (1, reps)), jnp.tile(sin_p, (1, reps))


def _pad_heads(w, heads, width):
    rows = w.shape[0]
    w = w.reshape(rows, heads, width)
    return jnp.pad(w, ((0, 0), (0, 0), (0, LANES - width))).reshape(rows, heads * LANES)


def kernel(x, norm_mix_pre, w_in, b_gate, mla_q_norm, mla_w_uq, mla_kv_norm, mla_w_ukv, w_o_a, w_o_b, w_out,
           norm_mix_post, norm_mlp_pre, w_ff1, w_ff2, norm_mlp_post):
    batch, seq, d_model = x.shape
    bf = jnp.bfloat16
    x2 = x.reshape(batch * seq, d_model)

    in_a = 3 * A_WIDTH
    c_kpe = in_a + B_Q_RANK + B_KV_RANK
    kpe_cols = jnp.pad(w_in[:, c_kpe:c_kpe + B_ROPE], ((0, 0), (B_NOPE, LANES - B_QK)))
    w1 = jnp.concatenate([w_in[:, :c_kpe], kpe_cols, w_in[:, c_kpe + B_ROPE:]], axis=1).astype(bf)
    wuq = _pad_heads(mla_w_uq, B_HEADS, B_QK).astype(bf)
    wukv = mla_w_ukv.reshape(B_KV_RANK, B_HEADS, B_NOPE + B_V)
    wuk = _pad_heads(wukv[:, :, :B_NOPE].reshape(B_KV_RANK, B_HEADS * B_NOPE), B_HEADS, B_NOPE).astype(bf)
    wuv = _pad_heads(wukv[:, :, B_NOPE:].reshape(B_KV_RANK, B_HEADS * B_V), B_HEADS, B_V).astype(bf)

    ca, sa = _rotary_tables(seq, A_ROPE_THETA, A_ROT_DIM, A_HEAD_DIM, 0)
    cb, sb = _rotary_tables(seq, B_ROPE_THETA, B_ROPE, LANES, B_NOPE)

    vec = lambda v: v.reshape(1, -1)
    qa, ka, va, qb, kb, vb, gates = _in_proj(
        x2, vec(norm_mix_pre), w1, vec(b_gate), vec(mla_q_norm), wuq, vec(mla_kv_norm), wuk, wuv,
        ca, sa, cb, sb, seq=seq, tm=512)

    oa, lse = [], []
    for g, (window, dilation) in enumerate(A_GROUPS):
        o_g, lse_g = _dilated_group(qa, ka, va, batch=batch, seq=seq, group=g, window=window, dilation=dilation)
        oa.append(o_g)
        lse.append(lse_g)

    shape_b = (batch, seq, B_WIDTH)
    ob = _mla_attention(qb.reshape(shape_b), kb.reshape(shape_b), vb.reshape(shape_b),
                        batch=batch, seq=seq, tq=256, tk=512)
    ob = ob.reshape(batch * seq, B_HEADS * B_V)

    h, hn = _mix_out(x2, oa, lse, ob, gates, w_o_a.astype(bf), w_o_b.astype(bf), w_out.astype(bf),
                     vec(norm_mix_post), vec(norm_mlp_pre), tm=512)
    out = _ffn(h, hn, w_ff1.astype(bf), w_ff2.astype(bf), vec(norm_mlp_post), tm=512)
    return out.reshape(batch, seq, d_model)
```

```python
import functools

import jax
import jax.numpy as jnp
import numpy as np
from jax import lax
from jax.experimental import pallas as pl
from jax.experimental.pallas import tpu as pltpu

EPS = 1e-6
LANES = 128

A_HEAD_DIM = 64
A_HEADS_PER_GROUP = 4
A_GROUPS = ((128, 1), (512, 4), (2048, 16))
A_HEADS = A_HEADS_PER_GROUP * len(A_GROUPS)
A_WIDTH = A_HEADS * A_HEAD_DIM
A_GROUP_WIDTH = A_HEADS_PER_GROUP * A_HEAD_DIM
A_ROT_DIM = A_HEAD_DIM // 4
A_ROPE_THETA = 500000.0
A_TILE = 128
B_HEADS = 8
B_NOPE = 64
B_ROPE = 32
B_QK = B_NOPE + B_ROPE
B_V = 64
B_Q_RANK = 512
B_KV_RANK = 256
B_ROPE_THETA = 10000.0
B_WIDTH = B_HEADS * LANES

NEG_BIG = -1e30
LOG2_E = 1.4426950408889634
VMEM_LIMIT = 56 * 1024 * 1024


def _rms(x, g):
    return x * lax.rsqrt(jnp.mean(x * x, axis=-1, keepdims=True) + EPS) * g


def _rotate_block(xb, cos, sin_signed, first_half, half):
    up = pltpu.roll(xb, LANES - half, 1)
    down = pltpu.roll(xb, half, 1)
    return xb * cos + jnp.where(first_half, up, down) * sin_signed


_C_QA, _C_KA, _C_VA = 0, A_WIDTH, 2 * A_WIDTH
_C_QC = 3 * A_WIDTH
_C_KVC = _C_QC + B_Q_RANK
_C_KPE = _C_KVC + B_KV_RANK
_C_GATE = _C_KPE + LANES


def _in_proj_kernel(x_ref, g_ref, w_ref, bg_ref, qn_ref, wuq_ref, kvn_ref, wuk_ref, wuv_ref,
                    ca_ref, sa_ref, cb_ref, sb_ref,
                    qa_o, ka_o, va_o, qb_o, kb_o, vb_o, gate_o, *, d_model):
    xn = _rms(x_ref[...], g_ref[...]).astype(jnp.bfloat16)

    def proj(lo, hi):
        return jnp.dot(xn, w_ref[:, lo:hi], preferred_element_type=jnp.float32)

    lane = lax.broadcasted_iota(jnp.int32, (1, LANES), 1)
    first_a = (lane % A_HEAD_DIM) < (A_ROT_DIM // 2)
    first_b = lane < (B_NOPE + B_ROPE // 2)
    ca, sa, cb, sb = ca_ref[...], sa_ref[...], cb_ref[...], sb_ref[...]

    qa = proj(_C_QA, _C_QA + A_WIDTH)
    ka = proj(_C_KA, _C_KA + A_WIDTH)
    a_scale = A_HEAD_DIM ** -0.5
    for j in range(A_WIDTH // LANES):
        blk = slice(j * LANES, (j + 1) * LANES)
        qa_o[:, blk] = (_rotate_block(qa[:, blk], ca, sa, first_a, A_ROT_DIM // 2) * a_scale).astype(qa_o.dtype)
        ka_o[:, blk] = _rotate_block(ka[:, blk], ca, sa, first_a, A_ROT_DIM // 2).astype(ka_o.dtype)
    va_o[...] = proj(_C_VA, _C_VA + A_WIDTH).astype(va_o.dtype)

    qcn = _rms(proj(_C_QC, _C_QC + B_Q_RANK), qn_ref[...]).astype(jnp.bfloat16)
    qb = jnp.dot(qcn, wuq_ref[...], preferred_element_type=jnp.float32)
    kvn = _rms(proj(_C_KVC, _C_KVC + B_KV_RANK), kvn_ref[...]).astype(jnp.bfloat16)
    kb = jnp.dot(kvn, wuk_ref[...], preferred_element_type=jnp.float32)
    vb = jnp.dot(kvn, wuv_ref[...], preferred_element_type=jnp.float32)
    kpe = _rotate_block(proj(_C_KPE, _C_KPE + LANES), cb, sb, first_b, B_ROPE // 2)
    b_scale = B_QK ** -0.5 * LOG2_E
    for h in range(B_HEADS):
        blk = slice(h * LANES, (h + 1) * LANES)
        qb_o[:, blk] = (_rotate_block(qb[:, blk], cb, sb, first_b, B_ROPE // 2) * b_scale).astype(qb_o.dtype)
        kb_o[:, blk] = (kb[:, blk] + kpe).astype(kb_o.dtype)
        vb_o[:, blk] = jnp.where(lane < B_V, vb[:, blk], 1.0).astype(vb_o.dtype)

    gate_o[...] = jax.nn.sigmoid(proj(_C_GATE, _C_GATE + 2 * d_model) + bg_ref[...]).astype(gate_o.dtype)


def _in_proj(x2, g, w1, bg, qn, wuq, kvn, wuk, wuv, ca, sa, cb, sb, *, seq, tm):
    t, d_model = x2.shape
    n1 = w1.shape[1]
    steps_per_seq = seq // tm
    row = lambda i: (i, 0)
    const = lambda i: (0, 0)
    pos = lambda i: (i % steps_per_seq, 0)
    bf = jnp.bfloat16
    out_shape = (
        jax.ShapeDtypeStruct((t, A_WIDTH), bf), jax.ShapeDtypeStruct((t, A_WIDTH), bf),
        jax.ShapeDtypeStruct((t, A_WIDTH), bf),
        jax.ShapeDtypeStruct((t, B_WIDTH), bf), jax.ShapeDtypeStruct((t, B_WIDTH), bf),
        jax.ShapeDtypeStruct((t, B_WIDTH), bf),
        jax.ShapeDtypeStruct((t, 2 * d_model), bf),
    )
    return pl.pallas_call(
        functools.partial(_in_proj_kernel, d_model=d_model),
        grid=(t // tm,),
        in_specs=[
            pl.BlockSpec((tm, d_model), row),
            pl.BlockSpec((1, d_model), const),
            pl.BlockSpec((d_model, n1), const),
            pl.BlockSpec((1, 2 * d_model), const),
            pl.BlockSpec((1, B_Q_RANK), const),
            pl.BlockSpec((B_Q_RANK, B_WIDTH), const),
            pl.BlockSpec((1, B_KV_RANK), const),
            pl.BlockSpec((B_KV_RANK, B_WIDTH), const),
            pl.BlockSpec((B_KV_RANK, B_WIDTH), const),
            pl.BlockSpec((tm, LANES), pos), pl.BlockSpec((tm, LANES), pos),
            pl.BlockSpec((tm, LANES), pos), pl.BlockSpec((tm, LANES), pos),
        ],
        out_specs=(
            pl.BlockSpec((tm, A_WIDTH), row), pl.BlockSpec((tm, A_WIDTH), row),
            pl.BlockSpec((tm, A_WIDTH), row),
            pl.BlockSpec((tm, B_WIDTH), row), pl.BlockSpec((tm, B_WIDTH), row),
            pl.BlockSpec((tm, B_WIDTH), row),
            pl.BlockSpec((tm, 2 * d_model), row),
        ),
        out_shape=out_shape,
        compiler_params=pltpu.CompilerParams(dimension_semantics=("parallel",), vmem_limit_bytes=VMEM_LIMIT),
        name="in_proj",
    )(x2, g, w1, bg, qn, wuq, kvn, wuk, wuv, ca, sa, cb, sb)


def _dilated_kernel(q_ref, k_ref, v_ref, o_ref, lse_ref, *, length, radius):
    win = A_TILE + 2 * radius
    lane = lax.broadcasted_iota(jnp.int32, (1, LANES), 1)
    head0 = lane < A_HEAD_DIM
    row = lax.broadcasted_iota(jnp.int32, (A_TILE, win), 0)
    col = lax.broadcasted_iota(jnp.int32, (A_TILE, win), 1)

    def tile(i, carry):
        l0 = pl.multiple_of(i * A_TILE, A_TILE)
        ws = pl.multiple_of(jnp.clip(l0 - radius, 0, length - win), radius)
        q = q_ref[0, pl.ds(l0, A_TILE), :]
        k = k_ref[0, pl.ds(ws, win), :]
        v = v_ref[0, pl.ds(ws, win), :]
        valid = jnp.abs((l0 + row) - (ws + col)) <= radius
        outs, lses = [], []
        for h in range(2):
            qh = jnp.where(head0 if h == 0 else ~head0, q, jnp.zeros_like(q))
            s = lax.dot_general(qh, k, (((1,), (1,)), ((), ())), preferred_element_type=jnp.float32)
            s = jnp.where(valid, s, NEG_BIG)
            m = jnp.max(s, axis=-1, keepdims=True)
            p = jnp.exp(s - m)
            l = jnp.sum(p, axis=-1, keepdims=True)
            o = jnp.dot(p.astype(v.dtype), v, preferred_element_type=jnp.float32)
            outs.append(o / l)
            lses.append(m + jnp.log(l))
        o_ref[0, pl.ds(l0, A_TILE), :] = jnp.where(head0, outs[0], outs[1]).astype(o_ref.dtype)
        lse_ref[0, pl.ds(l0, A_TILE), :] = jnp.where(head0, lses[0], lses[1])
        return carry

    lax.fori_loop(0, length // A_TILE, tile, 0)


def _dilated_group(qa, ka, va, *, batch, seq, group, window, dilation):
    length = seq // dilation
    radius = window // (2 * dilation)
    blocks_in = A_WIDTH // LANES
    blocks_out = A_GROUP_WIDTH // LANES
    view_in = lambda a: a.reshape(batch, length, dilation * A_WIDTH)
    in_map = lambda b, c: (b, 0, (c // blocks_out) * blocks_in + group * blocks_out + c % blocks_out)
    out_map = lambda b, c: (b, 0, c)
    in_spec = pl.BlockSpec((1, length, LANES), in_map)
    out_spec = pl.BlockSpec((1, length, LANES), out_map)
    o, lse = pl.pallas_call(
        functools.partial(_dilated_kernel, length=length, radius=radius),
        grid=(batch, dilation * blocks_out),
        in_specs=[in_spec, in_spec, in_spec],
        out_specs=(out_spec, out_spec),
        out_shape=(jax.ShapeDtypeStruct((batch, length, dilation * A_GROUP_WIDTH), jnp.bfloat16),
                   jax.ShapeDtypeStruct((batch, length, dilation * A_GROUP_WIDTH), jnp.float32)),
        compiler_params=pltpu.CompilerParams(dimension_semantics=("parallel", "parallel"),
                                             vmem_limit_bytes=VMEM_LIMIT),
        name=f"dilated_g{group}",
    )(view_in(qa), view_in(ka), view_in(va))
    return o.reshape(batch * seq, A_GROUP_WIDTH), lse.reshape(batch * seq, A_GROUP_WIDTH)


def _mla_kernel(q_ref, k_ref, v_ref, o_ref, *, seq, tk):
    tq = q_ref.shape[1]
    lane = lax.broadcasted_iota(jnp.int32, (1, LANES), 1)
    blks = [slice(h * LANES, (h + 1) * LANES) for h in range(2)]
    qs = [q_ref[0, :, blk] for blk in blks]

    def body(j, carry):
        k0 = pl.multiple_of(j * tk, tk)
        new = []
        for h in range(2):
            m, acc = carry[h]
            k = k_ref[0, pl.ds(k0, tk), blks[h]]
            v = v_ref[0, pl.ds(k0, tk), blks[h]]
            s = lax.dot_general(qs[h], k, (((1,), (1,)), ((), ())), preferred_element_type=jnp.float32)
            m_new = jnp.maximum(m, jnp.max(s, axis=-1, keepdims=True))
            p = jnp.exp2(s - m_new).astype(v.dtype)
            acc = jnp.exp2(m - m_new) * acc + jnp.dot(p, v, preferred_element_type=jnp.float32)
            new.append((m_new, acc))
        return tuple(new)

    m0 = jnp.full((tq, 1), NEG_BIG, jnp.float32)
    acc0 = jnp.zeros((tq, LANES), jnp.float32)
    res = lax.fori_loop(0, seq // tk, body, ((m0, acc0), (m0, acc0)))
    outs = [acc / pltpu.roll(acc, LANES - B_V, 1) for _, acc in res]
    o_ref[0] = jnp.where(lane < B_V, outs[0], pltpu.roll(outs[1], B_V, 1)).astype(o_ref.dtype)


def _mla_attention(qb, kb, vb, *, batch, seq, tq, tk):
    pairs = B_HEADS // 2
    return pl.pallas_call(
        functools.partial(_mla_kernel, seq=seq, tk=tk),
        grid=(batch, pairs, seq // tq),
        in_specs=[
            pl.BlockSpec((1, tq, 2 * LANES), lambda b, hp, i: (b, i, hp)),
            pl.BlockSpec((1, seq, 2 * LANES), lambda b, hp, i: (b, 0, hp)),
            pl.BlockSpec((1, seq, 2 * LANES), lambda b, hp, i: (b, 0, hp)),
        ],
        out_specs=pl.BlockSpec((1, tq, LANES), lambda b, hp, i: (b, i, hp)),
        out_shape=jax.ShapeDtypeStruct((batch, seq, B_HEADS * B_V), jnp.bfloat16),
        compiler_params=pltpu.CompilerParams(dimension_semantics=("parallel", "parallel", "parallel"),
                                             vmem_limit_bytes=VMEM_LIMIT),
        name="mla_attn",
    )(qb, kb, vb)


def _mix_out_kernel(x_ref, o0_ref, o1_ref, o2_ref, l0_ref, l1_ref, l2_ref, ob_ref, gate_ref,
                    woa_ref, wob_ref, wout_ref, gpost_ref, gpre_ref, h_o, hn_o, *, d_model):
    lses = [l0_ref[...], l1_ref[...], l2_ref[...]]
    outs = [o0_ref[...], o1_ref[...], o2_ref[...]]
    top = jnp.maximum(jnp.maximum(lses[0], lses[1]), lses[2])
    es = [jnp.exp(l - top) for l in lses]
    num = es[0] * outs[0] + es[1] * outs[1] + es[2] * outs[2]
    a = (num / (es[0] + es[1] + es[2])).astype(jnp.bfloat16)
    o_a = jnp.dot(a, woa_ref[...], preferred_element_type=jnp.float32)
    o_b = jnp.dot(ob_ref[...], wob_ref[...], preferred_element_type=jnp.float32)
    gates = gate_ref[...].astype(jnp.float32)
    mixin = (gates[:, :d_model] * o_a + gates[:, d_model:] * o_b).astype(jnp.bfloat16)
    mix = jnp.dot(mixin, wout_ref[...], preferred_element_type=jnp.float32)
    h = x_ref[...] + _rms(mix, gpost_ref[...])
    h_o[...] = h
    hn_o[...] = _rms(h, gpre_ref[...]).astype(hn_o.dtype)


def _mix_out(x2, oa, lse, ob, gates, woa, wob, wout, gpost, gpre, *, tm):
    t, d_model = x2.shape
    row = lambda i: (i, 0)
    const = lambda i: (0, 0)
    a_spec = pl.BlockSpec((tm, A_GROUP_WIDTH), row)
    return pl.pallas_call(
        functools.partial(_mix_out_kernel, d_model=d_model),
        grid=(t // tm,),
        in_specs=[
            pl.BlockSpec((tm, d_model), row),
            a_spec, a_spec, a_spec, a_spec, a_spec, a_spec,
            pl.BlockSpec((tm, B_HEADS * B_V), row),
            pl.BlockSpec((tm, 2 * d_model), row),
            pl.BlockSpec((A_GROUP_WIDTH, d_model), const),
            pl.BlockSpec((B_HEADS * B_V, d_model), const),
            pl.BlockSpec((d_model, d_model), const),
            pl.BlockSpec((1, d_model), const),
            pl.BlockSpec((1, d_model), const),
        ],
        out_specs=(pl.BlockSpec((tm, d_model), row), pl.BlockSpec((tm, d_model), row)),
        out_shape=(jax.ShapeDtypeStruct((t, d_model), jnp.float32),
                   jax.ShapeDtypeStruct((t, d_model), jnp.bfloat16)),
        compiler_params=pltpu.CompilerParams(dimension_semantics=("parallel",), vmem_limit_bytes=VMEM_LIMIT),
        name="mix_out",
    )(x2, *oa, *lse, ob, gates, woa, wob, wout, gpost, gpre)


def _ffn_kernel(h_ref, hn_ref, w1_ref, w2_ref, g_ref, out_o):
    u = jnp.dot(hn_ref[...], w1_ref[...], preferred_element_type=jnp.float32)
    u = jnp.square(jnp.maximum(u, 0.0)).astype(jnp.bfloat16)
    ff = jnp.dot(u, w2_ref[...], preferred_element_type=jnp.float32)
    out_o[...] = h_ref[...] + _rms(ff, g_ref[...])


def _ffn(h, hn, w1, w2, g, *, tm):
    t, d_model = h.shape
    d_ff = w1.shape[1]
    row = lambda i: (i, 0)
    const = lambda i: (0, 0)
    return pl.pallas_call(
        _ffn_kernel,
        grid=(t // tm,),
        in_specs=[
            pl.BlockSpec((tm, d_model), row), pl.BlockSpec((tm, d_model), row),
            pl.BlockSpec((d_model, d_ff), const), pl.BlockSpec((d_ff, d_model), const),
            pl.BlockSpec((1, d_model), const),
        ],
        out_specs=pl.BlockSpec((tm, d_model), row),
        out_shape=jax.ShapeDtypeStruct((t, d_model), jnp.float32),
        compiler_params=pltpu.CompilerParams(dimension_semantics=("parallel",), vmem_limit_bytes=VMEM_LIMIT),
        name="ffn",
    )(h, hn, w1, w2, g)


def _rotary_tables(seq, theta, rot_dim, period, offset):
    half = rot_dim // 2
    inv = jnp.float32(theta) ** (-(jnp.arange(half, dtype=jnp.float32) * 2.0 / rot_dim))
    ang = jnp.arange(seq, dtype=jnp.int32).astype(jnp.float32)[:, None] * inv[None, :]
    cos, sin = jnp.cos(ang), jnp.sin(ang)
    cos_p = jnp.ones((seq, period), jnp.float32).at[:, offset:offset + rot_dim].set(jnp.concatenate([cos, cos], 1))
    sin_p = jnp.zeros((seq, period), jnp.float32).at[:, offset:offset + rot_dim].set(jnp.concatenate([-sin, sin], 1))
    reps = LANES // period
    return jnp.tile(cos_p, (1, reps)), jnp.tile(sin_p, (1, reps))


def _pad_heads(w, heads, width):
    rows = w.shape[0]
    w = w.reshape(rows, heads, width)
    return jnp.pad(w, ((0, 0), (0, 0), (0, LANES - width))).reshape(rows, heads * LANES)


def kernel(x, norm_mix_pre, w_in, b_gate, mla_q_norm, mla_w_uq, mla_kv_norm, mla_w_ukv, w_o_a, w_o_b, w_out,
           norm_mix_post, norm_mlp_pre, w_ff1, w_ff2, norm_mlp_post):
    batch, seq, d_model = x.shape
    bf = jnp.bfloat16
    x2 = x.reshape(batch * seq, d_model)

    in_a = 3 * A_WIDTH
    c_kpe = in_a + B_Q_RANK + B_KV_RANK
    kpe_cols = jnp.pad(w_in[:, c_kpe:c_kpe + B_ROPE], ((0, 0), (B_NOPE, LANES - B_QK)))
    w1 = jnp.concatenate([w_in[:, :c_kpe], kpe_cols, w_in[:, c_kpe + B_ROPE:]], axis=1).astype(bf)
    wuq = _pad_heads(mla_w_uq, B_HEADS, B_QK).astype(bf)
    wukv = mla_w_ukv.reshape(B_KV_RANK, B_HEADS, B_NOPE + B_V)
    wuk = _pad_heads(wukv[:, :, :B_NOPE].reshape(B_KV_RANK, B_HEADS * B_NOPE), B_HEADS, B_NOPE).astype(bf)
    wuv = _pad_heads(wukv[:, :, B_NOPE:].reshape(B_KV_RANK, B_HEADS * B_V), B_HEADS, B_V).astype(bf)

    ca, sa = _rotary_tables(seq, A_ROPE_THETA, A_ROT_DIM, A_HEAD_DIM, 0)
    cb, sb = _rotary_tables(seq, B_ROPE_THETA, B_ROPE, LANES, B_NOPE)

    vec = lambda v: v.reshape(1, -1)
    qa, ka, va, qb, kb, vb, gates = _in_proj(
        x2, vec(norm_mix_pre), w1, vec(b_gate), vec(mla_q_norm), wuq, vec(mla_kv_norm), wuk, wuv,
        ca, sa, cb, sb, seq=seq, tm=512)

    oa, lse = [], []
    for g, (window, dilation) in enumerate(A_GROUPS):
        o_g, lse_g = _dilated_group(qa, ka, va, batch=batch, seq=seq, group=g, window=window, dilation=dilation)
        oa.append(o_g)
        lse.append(lse_g)

    shape_b = (batch, seq, B_WIDTH)
    ob = _mla_attention(qb.reshape(shape_b), kb.reshape(shape_b), vb.reshape(shape_b),
                        batch=batch, seq=seq, tq=256, tk=1024)
    ob = ob.reshape(batch * seq, B_HEADS * B_V)

    h, hn = _mix_out(x2, oa, lse, ob, gates, w_o_a.astype(bf), w_o_b.astype(bf), w_out.astype(bf),
                     vec(norm_mix_post), vec(norm_mlp_pre), tm=512)
    out = _ffn(h, hn, w_ff1.astype(bf), w_ff2.astype(bf), vec(norm_mlp_post), tm=512)
    return out.reshape(batch, seq, d_model)
```

```python
import functools

import jax
import jax.numpy as jnp
import numpy as np
from jax import lax
from jax.experimental import pallas as pl
from jax.experimental.pallas import tpu as pltpu

EPS = 1e-6
LANES = 128

A_HEAD_DIM = 64
A_HEADS_PER_GROUP = 4
A_GROUPS = ((128, 1), (512, 4), (2048, 16))
A_HEADS = A_HEADS_PER_GROUP * len(A_GROUPS)
A_WIDTH = A_HEADS * A_HEAD_DIM
A_GROUP_WIDTH = A_HEADS_PER_GROUP * A_HEAD_DIM
A_ROT_DIM = A_HEAD_DIM // 4
A_ROPE_THETA = 500000.0
A_TILE = 128
B_HEADS = 8
B_NOPE = 64
B_ROPE = 32
B_QK = B_NOPE + B_ROPE
B_V = 64
B_Q_RANK = 512
B_KV_RANK = 256
B_ROPE_THETA = 10000.0
B_WIDTH = B_HEADS * LANES
MLA_KEY_BLOCK = 256

NEG_BIG = -1e30
LOG2_E = 1.4426950408889634
VMEM_LIMIT = 56 * 1024 * 1024


def _rms(x, g):
    return x * lax.rsqrt(jnp.mean(x * x, axis=-1, keepdims=True) + EPS) * g


def _rotate_block(xb, cos, sin_signed, first_half, half):
    up = pltpu.roll(xb, LANES - half, 1)
    down = pltpu.roll(xb, half, 1)
    return xb * cos + jnp.where(first_half, up, down) * sin_signed


_C_QA, _C_KA, _C_VA = 0, A_WIDTH, 2 * A_WIDTH
_C_QC = 3 * A_WIDTH
_C_KVC = _C_QC + B_Q_RANK
_C_KPE = _C_KVC + B_KV_RANK
_C_GATE = _C_KPE + LANES


_NT_DIMS = (((1,), (1,)), ((), ()))


def _in_proj_kernel(x_ref, g_ref, w_ref, bg_ref, qn_ref, wuqt_ref, kvn_ref, wuk_ref, wuvt_ref,
                    ca_ref, sa_ref, cb_ref, sb_ref, cbt_ref, sbt_ref,
                    qa_o, ka_o, va_o, qbt_o, kb_o, vbt_o, gate_o, *, d_model):
    xn = _rms(x_ref[...], g_ref[...]).astype(jnp.bfloat16)

    def proj(lo, hi):
        return jnp.dot(xn, w_ref[:, lo:hi], preferred_element_type=jnp.float32)

    lane = lax.broadcasted_iota(jnp.int32, (1, LANES), 1)
    first_a = (lane % A_HEAD_DIM) < (A_ROT_DIM // 2)
    first_b = lane < (B_NOPE + B_ROPE // 2)
    ca, sa, cb, sb = ca_ref[...], sa_ref[...], cb_ref[...], sb_ref[...]

    qa = proj(_C_QA, _C_QA + A_WIDTH)
    ka = proj(_C_KA, _C_KA + A_WIDTH)
    a_scale = A_HEAD_DIM ** -0.5
    for j in range(A_WIDTH // LANES):
        blk = slice(j * LANES, (j + 1) * LANES)
        qa_o[:, blk] = (_rotate_block(qa[:, blk], ca, sa, first_a, A_ROT_DIM // 2) * a_scale).astype(qa_o.dtype)
        ka_o[:, blk] = _rotate_block(ka[:, blk], ca, sa, first_a, A_ROT_DIM // 2).astype(ka_o.dtype)
    va_o[...] = proj(_C_VA, _C_VA + A_WIDTH).astype(va_o.dtype)

    qcn = _rms(proj(_C_QC, _C_QC + B_Q_RANK), qn_ref[...]).astype(jnp.bfloat16)
    kvn = _rms(proj(_C_KVC, _C_KVC + B_KV_RANK), kvn_ref[...]).astype(jnp.bfloat16)
    qbt = lax.dot_general(wuqt_ref[...], qcn, _NT_DIMS, preferred_element_type=jnp.float32)
    vbt = lax.dot_general(wuvt_ref[...], kvn, _NT_DIMS, preferred_element_type=jnp.float32)
    kb = jnp.dot(kvn, wuk_ref[...], preferred_element_type=jnp.float32)
    kpe = _rotate_block(proj(_C_KPE, _C_KPE + LANES), cb, sb, first_b, B_ROPE // 2)
    b_scale = B_QK ** -0.5 * LOG2_E
    cbt, sbt = cbt_ref[...], sbt_ref[...]
    row = lax.broadcasted_iota(jnp.int32, (LANES, 1), 0)
    r1, r2 = B_NOPE + B_ROPE // 2, B_QK
    for h in range(B_HEADS):
        blk = slice(h * LANES, (h + 1) * LANES)
        xt = qbt[blk, :]
        partner = jnp.concatenate([xt[:B_NOPE], xt[r1:r2], xt[B_NOPE:r1], xt[r2:]], axis=0)
        qbt_o[0, 0, blk, :] = ((xt * cbt + partner * sbt) * b_scale).astype(qbt_o.dtype)
        kb_o[:, blk] = (kb[:, blk] + kpe).astype(kb_o.dtype)
        vbt_o[0, 0, blk, :] = jnp.where(row < B_V, vbt[blk, :], 1.0).astype(vbt_o.dtype)

    gate_o[...] = jax.nn.sigmoid(proj(_C_GATE, _C_GATE + 2 * d_model) + bg_ref[...]).astype(gate_o.dtype)


def _in_proj(x2, g, w1, bg, qn, wuqt, kvn, wuk, wuvt, ca, sa, cb, sb, cbt, sbt, *, seq, tm):
    t, d_model = x2.shape
    n1 = w1.shape[1]
    steps_per_seq = seq // tm
    row = lambda i: (i, 0)
    const = lambda i: (0, 0)
    pos = lambda i: (i % steps_per_seq, 0)
    pos_t = lambda i: (0, i % steps_per_seq)
    tile_t = lambda i: (i // steps_per_seq, i % steps_per_seq, 0, 0)
    bf = jnp.bfloat16
    transposed = jax.ShapeDtypeStruct((t // seq, steps_per_seq, B_WIDTH, tm), bf)
    out_shape = (
        jax.ShapeDtypeStruct((t, A_WIDTH), bf), jax.ShapeDtypeStruct((t, A_WIDTH), bf),
        jax.ShapeDtypeStruct((t, A_WIDTH), bf),
        transposed, jax.ShapeDtypeStruct((t, B_WIDTH), bf), transposed,
        jax.ShapeDtypeStruct((t, 2 * d_model), bf),
    )
    return pl.pallas_call(
        functools.partial(_in_proj_kernel, d_model=d_model),
        grid=(t // tm,),
        in_specs=[
            pl.BlockSpec((tm, d_model), row),
            pl.BlockSpec((1, d_model), const),
            pl.BlockSpec((d_model, n1), const),
            pl.BlockSpec((1, 2 * d_model), const),
            pl.BlockSpec((1, B_Q_RANK), const),
            pl.BlockSpec((B_WIDTH, B_Q_RANK), const),
            pl.BlockSpec((1, B_KV_RANK), const),
            pl.BlockSpec((B_KV_RANK, B_WIDTH), const),
            pl.BlockSpec((B_WIDTH, B_KV_RANK), const),
            pl.BlockSpec((tm, LANES), pos), pl.BlockSpec((tm, LANES), pos),
            pl.BlockSpec((tm, LANES), pos), pl.BlockSpec((tm, LANES), pos),
            pl.BlockSpec((LANES, tm), pos_t), pl.BlockSpec((LANES, tm), pos_t),
        ],
        out_specs=(
            pl.BlockSpec((tm, A_WIDTH), row), pl.BlockSpec((tm, A_WIDTH), row),
            pl.BlockSpec((tm, A_WIDTH), row),
            pl.BlockSpec((1, 1, B_WIDTH, tm), tile_t), pl.BlockSpec((tm, B_WIDTH), row),
            pl.BlockSpec((1, 1, B_WIDTH, tm), tile_t),
            pl.BlockSpec((tm, 2 * d_model), row),
        ),
        out_shape=out_shape,
        compiler_params=pltpu.CompilerParams(dimension_semantics=("parallel",), vmem_limit_bytes=VMEM_LIMIT),
        name="in_proj",
    )(x2, g, w1, bg, qn, wuqt, kvn, wuk, wuvt, ca, sa, cb, sb, cbt, sbt)


def _dilated_kernel(q_ref, k_ref, v_ref, o_ref, lse_ref, *, length, radius):
    win = A_TILE + 2 * radius
    lane = lax.broadcasted_iota(jnp.int32, (1, LANES), 1)
    head0 = lane < A_HEAD_DIM
    row = lax.broadcasted_iota(jnp.int32, (A_TILE, win), 0)
    col = lax.broadcasted_iota(jnp.int32, (A_TILE, win), 1)

    def tile(i, carry):
        l0 = pl.multiple_of(i * A_TILE, A_TILE)
        ws = pl.multiple_of(jnp.clip(l0 - radius, 0, length - win), radius)
        q = q_ref[0, pl.ds(l0, A_TILE), :]
        k = k_ref[0, pl.ds(ws, win), :]
        v = v_ref[0, pl.ds(ws, win), :]
        valid = jnp.abs((l0 + row) - (ws + col)) <= radius
        outs, lses = [], []
        for h in range(2):
            qh = jnp.where(head0 if h == 0 else ~head0, q, jnp.zeros_like(q))
            s = lax.dot_general(qh, k, (((1,), (1,)), ((), ())), preferred_element_type=jnp.float32)
            s = jnp.where(valid, s, NEG_BIG)
            m = jnp.max(s, axis=-1, keepdims=True)
            p = jnp.exp(s - m)
            l = jnp.sum(p, axis=-1, keepdims=True)
            o = jnp.dot(p.astype(v.dtype), v, preferred_element_type=jnp.float32)
            outs.append(o / l)
            lses.append(m + jnp.log(l))
        o_ref[0, pl.ds(l0, A_TILE), :] = jnp.where(head0, outs[0], outs[1]).astype(o_ref.dtype)
        lse_ref[0, pl.ds(l0, A_TILE), :] = jnp.where(head0, lses[0], lses[1])
        return carry

    lax.fori_loop(0, length // A_TILE, tile, 0)


def _dilated_group(qa, ka, va, *, batch, seq, group, window, dilation):
    length = seq // dilation
    radius = window // (2 * dilation)
    blocks_in = A_WIDTH // LANES
    blocks_out = A_GROUP_WIDTH // LANES
    view_in = lambda a: a.reshape(batch, length, dilation * A_WIDTH)
    in_map = lambda b, c: (b, 0, (c // blocks_out) * blocks_in + group * blocks_out + c % blocks_out)
    out_map = lambda b, c: (b, 0, c)
    in_spec = pl.BlockSpec((1, length, LANES), in_map)
    out_spec = pl.BlockSpec((1, length, LANES), out_map)
    o, lse = pl.pallas_call(
        functools.partial(_dilated_kernel, length=length, radius=radius),
        grid=(batch, dilation * blocks_out),
        in_specs=[in_spec, in_spec, in_spec],
        out_specs=(out_spec, out_spec),
        out_shape=(jax.ShapeDtypeStruct((batch, length, dilation * A_GROUP_WIDTH), jnp.bfloat16),
                   jax.ShapeDtypeStruct((batch, length, dilation * A_GROUP_WIDTH), jnp.float32)),
        compiler_params=pltpu.CompilerParams(dimension_semantics=("parallel", "parallel"),
                                             vmem_limit_bytes=VMEM_LIMIT),
        name=f"dilated_g{group}",
    )(view_in(qa), view_in(ka), view_in(va))
    return o.reshape(batch * seq, A_GROUP_WIDTH), lse.reshape(batch * seq, A_GROUP_WIDTH)


def _mla_kernel(qt_ref, k_ref, vt_ref, o_ref, sa_ref, sb_ref, pa_ref, pb_ref, *, n_groups):
    tq = qt_ref.shape[3]
    tk = vt_ref.shape[3]
    key_blocks = tk // MLA_KEY_BLOCK
    lane = lax.broadcasted_iota(jnp.int32, (1, LANES), 1)
    blks = [slice(h * LANES, (h + 1) * LANES) for h in range(2)]
    qts = [qt_ref[0, 0, blk, :] for blk in blks]

    def rows(c):
        return slice(c * MLA_KEY_BLOCK, (c + 1) * MLA_KEY_BLOCK)

    def qk(h, g, s_ref):
        base = g * tk if isinstance(g, int) else pl.multiple_of(g * tk, tk)
        for c in range(key_blocks):
            k = k_ref[0, pl.ds(base + c * MLA_KEY_BLOCK, MLA_KEY_BLOCK), blks[h]]
            s_ref[h, rows(c), :] = jnp.dot(k, qts[h], preferred_element_type=jnp.float32)

    def pv(h, g, p_ref):
        total = None
        for c in range(key_blocks):
            d = jnp.dot(vt_ref[0, g, blks[h], rows(c)], p_ref[h, rows(c), :], preferred_element_type=jnp.float32)
            total = d if total is None else total + d
        return total

    def softmax(h, m, s_ref, p_ref):
        m_new = m
        for c in range(key_blocks):
            m_new = jnp.maximum(m_new, jnp.max(s_ref[h, rows(c), :], axis=0, keepdims=True))
        for c in range(key_blocks):
            p_ref[h, rows(c), :] = jnp.exp2(s_ref[h, rows(c), :] - m_new).astype(p_ref.dtype)
        return m_new, jnp.exp2(m - m_new)

    def step(g, carry, s_read, s_write, p_read, p_write, first=False, last=False):
        if not last:
            for h in range(2):
                qk(h, g + 1, s_write)
        pvs = [None if first else pv(h, g - 1, p_read) for h in range(2)]
        new = []
        for h in range(2):
            m, acc, alpha = carry[h]
            m_new, alpha_new = softmax(h, m, s_read, p_write)
            new.append((m_new, acc if first else alpha * acc + pvs[h], alpha_new))
        return tuple(new)

    def even(g, carry, **kw):
        return step(g, carry, sa_ref, sb_ref, pb_ref, pa_ref, **kw)

    def odd(g, carry, **kw):
        return step(g, carry, sb_ref, sa_ref, pa_ref, pb_ref, **kw)

    for h in range(2):
        qk(h, 0, sa_ref)
    init = (jnp.full((1, tq), NEG_BIG, jnp.float32), jnp.zeros((LANES, tq), jnp.float32),
            jnp.ones((1, tq), jnp.float32))
    carry = even(0, (init, init), first=True)

    def body(t, carry):
        g = 2 * t + 1
        return even(g + 1, odd(g, carry))

    carry = lax.fori_loop(0, (n_groups - 2) // 2, body, carry)
    carry = odd(n_groups - 1, carry, last=True)
    accs = [(alpha * acc + pv(h, n_groups - 1, pb_ref)).T for h, (_, acc, alpha) in enumerate(carry)]
    outs = [a / pltpu.roll(a, LANES - B_V, 1) for a in accs]
    o_ref[0] = jnp.where(lane < B_V, outs[0], pltpu.roll(outs[1], B_V, 1)).astype(o_ref.dtype)


def _mla_attention(qbt, kb, vbt):
    batch, tiles, _, tile = qbt.shape
    seq = kb.shape[1]
    pairs = B_HEADS // 2
    return pl.pallas_call(
        functools.partial(_mla_kernel, n_groups=tiles),
        grid=(batch, pairs, tiles),
        in_specs=[
            pl.BlockSpec((1, 1, 2 * LANES, tile), lambda b, hp, i: (b, i, hp, 0)),
            pl.BlockSpec((1, seq, 2 * LANES), lambda b, hp, i: (b, 0, hp)),
            pl.BlockSpec((1, tiles, 2 * LANES, tile), lambda b, hp, i: (b, 0, hp, 0)),
        ],
        out_specs=pl.BlockSpec((1, tile, LANES), lambda b, hp, i: (b, i, hp)),
        out_shape=jax.ShapeDtypeStruct((batch, seq, B_HEADS * B_V), jnp.bfloat16),
        scratch_shapes=[pltpu.VMEM((2, tile, tile), jnp.float32), pltpu.VMEM((2, tile, tile), jnp.float32),
                        pltpu.VMEM((2, tile, tile), jnp.bfloat16), pltpu.VMEM((2, tile, tile), jnp.bfloat16)],
        compiler_params=pltpu.CompilerParams(dimension_semantics=("parallel", "parallel", "parallel"),
                                             vmem_limit_bytes=VMEM_LIMIT),
        name="mla_attn",
    )(qbt, kb, vbt)


def _mix_out_kernel(x_ref, o0_ref, o1_ref, o2_ref, l0_ref, l1_ref, l2_ref, ob_ref, gate_ref,
                    woa_ref, wob_ref, wout_ref, gpost_ref, gpre_ref, h_o, hn_o, *, d_model):
    lses = [l0_ref[...], l1_ref[...], l2_ref[...]]
    outs = [o0_ref[...], o1_ref[...], o2_ref[...]]
    top = jnp.maximum(jnp.maximum(lses[0], lses[1]), lses[2])
    es = [jnp.exp(l - top) for l in lses]
    num = es[0] * outs[0] + es[1] * outs[1] + es[2] * outs[2]
    a = (num / (es[0] + es[1] + es[2])).astype(jnp.bfloat16)
    o_a = jnp.dot(a, woa_ref[...], preferred_element_type=jnp.float32)
    o_b = jnp.dot(ob_ref[...], wob_ref[...], preferred_element_type=jnp.float32)
    gates = gate_ref[...].astype(jnp.float32)
    mixin = (gates[:, :d_model] * o_a + gates[:, d_model:] * o_b).astype(jnp.bfloat16)
    mix = jnp.dot(mixin, wout_ref[...], preferred_element_type=jnp.float32)
    h = x_ref[...] + _rms(mix, gpost_ref[...])
    h_o[...] = h
    hn_o[...] = _rms(h, gpre_ref[...]).astype(hn_o.dtype)


def _mix_out(x2, oa, lse, ob, gates, woa, wob, wout, gpost, gpre, *, tm):
    t, d_model = x2.shape
    row = lambda i: (i, 0)
    const = lambda i: (0, 0)
    a_spec = pl.BlockSpec((tm, A_GROUP_WIDTH), row)
    return pl.pallas_call(
        functools.partial(_mix_out_kernel, d_model=d_model),
        grid=(t // tm,),
        in_specs=[
            pl.BlockSpec((tm, d_model), row),
            a_spec, a_spec, a_spec, a_spec, a_spec, a_spec,
            pl.BlockSpec((tm, B_HEADS * B_V), row),
            pl.BlockSpec((tm, 2 * d_model), row),
            pl.BlockSpec((A_GROUP_WIDTH, d_model), const),
            pl.BlockSpec((B_HEADS * B_V, d_model), const),
            pl.BlockSpec((d_model, d_model), const),
            pl.BlockSpec((1, d_model), const),
            pl.BlockSpec((1, d_model), const),
        ],
        out_specs=(pl.BlockSpec((tm, d_model), row), pl.BlockSpec((tm, d_model), row)),
        out_shape=(jax.ShapeDtypeStruct((t, d_model), jnp.float32),
                   jax.ShapeDtypeStruct((t, d_model), jnp.bfloat16)),
        compiler_params=pltpu.CompilerParams(dimension_semantics=("parallel",), vmem_limit_bytes=VMEM_LIMIT),
        name="mix_out",
    )(x2, *oa, *lse, ob, gates, woa, wob, wout, gpost, gpre)


def _ffn_kernel(h_ref, hn_ref, w1_ref, w2_ref, g_ref, out_o):
    u = jnp.dot(hn_ref[...], w1_ref[...], preferred_element_type=jnp.float32)
    u = jnp.square(jnp.maximum(u, 0.0)).astype(jnp.bfloat16)
    ff = jnp.dot(u, w2_ref[...], preferred_element_type=jnp.float32)
    out_o[...] = h_ref[...] + _rms(ff, g_ref[...])


def _ffn(h, hn, w1, w2, g, *, tm):
    t, d_model = h.shape
    d_ff = w1.shape[1]
    row = lambda i: (i, 0)
    const = lambda i: (0, 0)
    return pl.pallas_call(
        _ffn_kernel,
        grid=(t // tm,),
        in_specs=[
            pl.BlockSpec((tm, d_model), row), pl.BlockSpec((tm, d_model), row),
            pl.BlockSpec((d_model, d_ff), const), pl.BlockSpec((d_ff, d_model), const),
            pl.BlockSpec((1, d_model), const),
        ],
        out_specs=pl.BlockSpec((tm, d_model), row),
        out_shape=jax.ShapeDtypeStruct((t, d_model), jnp.float32),
        compiler_params=pltpu.CompilerParams(dimension_semantics=("parallel",), vmem_limit_bytes=VMEM_LIMIT),
        name="ffn",
    )(h, hn, w1, w2, g)


def _rotary_tables(seq, theta, rot_dim, period, offset):
    half = rot_dim // 2
    inv = jnp.float32(theta) ** (-(jnp.arange(half, dtype=jnp.float32) * 2.0 / rot_dim))
    ang = jnp.arange(seq, dtype=jnp.int32).astype(jnp.float32)[:, None] * inv[None, :]
    cos, sin = jnp.cos(ang), jnp.sin(ang)
    cos_p = jnp.ones((seq, period), jnp.float32).at[:, offset:offset + rot_dim].set(jnp.concatenate([cos, cos], 1))
    sin_p = jnp.zeros((seq, period), jnp.float32).at[:, offset:offset + rot_dim].set(jnp.concatenate([-sin, sin], 1))
    reps = LANES // period
    return jnp.tile(cos_p, (1, reps)), jnp.tile(sin_p, (1, reps))


def _pad_heads(w, heads, width):
    rows = w.shape[0]
    w = w.reshape(rows, heads, width)
    return jnp.pad(w, ((0, 0), (0, 0), (0, LANES - width))).reshape(rows, heads * LANES)


def kernel(x, norm_mix_pre, w_in, b_gate, mla_q_norm, mla_w_uq, mla_kv_norm, mla_w_ukv, w_o_a, w_o_b, w_out,
           norm_mix_post, norm_mlp_pre, w_ff1, w_ff2, norm_mlp_post):
    batch, seq, d_model = x.shape
    bf = jnp.bfloat16
    x2 = x.reshape(batch * seq, d_model)

    in_a = 3 * A_WIDTH
    c_kpe = in_a + B_Q_RANK + B_KV_RANK
    kpe_cols = jnp.pad(w_in[:, c_kpe:c_kpe + B_ROPE], ((0, 0), (B_NOPE, LANES - B_QK)))
    w1 = jnp.concatenate([w_in[:, :c_kpe], kpe_cols, w_in[:, c_kpe + B_ROPE:]], axis=1).astype(bf)
    wuqt = _pad_heads(mla_w_uq, B_HEADS, B_QK).T.astype(bf)
    wukv = mla_w_ukv.reshape(B_KV_RANK, B_HEADS, B_NOPE + B_V)
    wuk = _pad_heads(wukv[:, :, :B_NOPE].reshape(B_KV_RANK, B_HEADS * B_NOPE), B_HEADS, B_NOPE).astype(bf)
    wuvt = _pad_heads(wukv[:, :, B_NOPE:].reshape(B_KV_RANK, B_HEADS * B_V), B_HEADS, B_V).T.astype(bf)

    ca, sa = _rotary_tables(seq, A_ROPE_THETA, A_ROT_DIM, A_HEAD_DIM, 0)
    cb, sb = _rotary_tables(seq, B_ROPE_THETA, B_ROPE, LANES, B_NOPE)

    vec = lambda v: v.reshape(1, -1)
    qa, ka, va, qbt, kb, vbt, gates = _in_proj(
        x2, vec(norm_mix_pre), w1, vec(b_gate), vec(mla_q_norm), wuqt, vec(mla_kv_norm), wuk, wuvt,
        ca, sa, cb, sb, cb.T, sb.T, seq=seq, tm=512)

    oa, lse = [], []
    for g, (window, dilation) in enumerate(A_GROUPS):
        o_g, lse_g = _dilated_group(qa, ka, va, batch=batch, seq=seq, group=g, window=window, dilation=dilation)
        oa.append(o_g)
        lse.append(lse_g)

    ob = _mla_attention(qbt, kb.reshape(batch, seq, B_WIDTH), vbt)
    ob = ob.reshape(batch * seq, B_HEADS * B_V)

    h, hn = _mix_out(x2, oa, lse, ob, gates, w_o_a.astype(bf), w_o_b.astype(bf), w_out.astype(bf),
                     vec(norm_mix_post), vec(norm_mlp_pre), tm=512)
    out = _ffn(h, hn, w_ff1.astype(bf), w_ff2.astype(bf), vec(norm_mlp_post), tm=512)
    return out.reshape(batch, seq, d_model)
```

```python
import functools

import jax
import jax.numpy as jnp
from jax import lax
from jax.experimental import pallas as pl
from jax.experimental.pallas import tpu as pltpu

EPS = 1e-6
LANES = 128

A_HEAD_DIM = 64
A_HEADS_PER_GROUP = 4
A_GROUPS = ((128, 1), (512, 4), (2048, 16))
A_HEADS = A_HEADS_PER_GROUP * len(A_GROUPS)
A_WIDTH = A_HEADS * A_HEAD_DIM
A_GROUP_WIDTH = A_HEADS_PER_GROUP * A_HEAD_DIM
A_ROT_DIM = A_HEAD_DIM // 4
A_ROPE_THETA = 500000.0
A_TILE = 128
A_TILES_PER_ITER = 4
B_HEADS = 8
B_NOPE = 64
B_ROPE = 32
B_QK = B_NOPE + B_ROPE
B_V = 64
B_Q_RANK = 512
B_KV_RANK = 256
B_ROPE_THETA = 10000.0
B_WIDTH = B_HEADS * LANES
MLA_KEY_BLOCK = 256

NEG_BIG = -1e30
LOG2_E = 1.4426950408889634
VMEM_LIMIT = 56 * 1024 * 1024


def _rms(x, g):
    return x * lax.rsqrt(jnp.mean(x * x, axis=-1, keepdims=True) + EPS) * g


def _rotate_block(xb, cos, sin_signed, first_half, half):
    up = pltpu.roll(xb, LANES - half, 1)
    down = pltpu.roll(xb, half, 1)
    return xb * cos + jnp.where(first_half, up, down) * sin_signed


_C_QA, _C_KA, _C_VA = 0, A_WIDTH, 2 * A_WIDTH
_C_QC = 3 * A_WIDTH
_C_KVC = _C_QC + B_Q_RANK
_C_KPE = _C_KVC + B_KV_RANK
_C_GATE = _C_KPE + LANES


_NT_DIMS = (((1,), (1,)), ((), ()))


def _in_proj_kernel(x_ref, g_ref, w_ref, bg_ref, qn_ref, wuqt_ref, kvn_ref, wuk_ref, wuvt_ref,
                    ca_ref, sa_ref, cb_ref, sb_ref, cbt_ref, sbt_ref,
                    qa_o, ka_o, va_o, qbt_o, kb_o, vbt_o, gate_o, *, d_model):
    xn = _rms(x_ref[...], g_ref[...]).astype(jnp.bfloat16)

    def proj(lo, hi):
        return jnp.dot(xn, w_ref[:, lo:hi], preferred_element_type=jnp.float32)

    lane = lax.broadcasted_iota(jnp.int32, (1, LANES), 1)
    first_a = (lane % A_HEAD_DIM) < (A_ROT_DIM // 2)
    first_b = lane < (B_NOPE + B_ROPE // 2)
    ca, sa, cb, sb = ca_ref[...], sa_ref[...], cb_ref[...], sb_ref[...]

    qa = proj(_C_QA, _C_QA + A_WIDTH)
    ka = proj(_C_KA, _C_KA + A_WIDTH)
    a_scale = A_HEAD_DIM ** -0.5 * LOG2_E
    for j in range(A_WIDTH // LANES):
        blk = slice(j * LANES, (j + 1) * LANES)
        qa_o[:, blk] = (_rotate_block(qa[:, blk], ca, sa, first_a, A_ROT_DIM // 2) * a_scale).astype(qa_o.dtype)
        ka_o[:, blk] = _rotate_block(ka[:, blk], ca, sa, first_a, A_ROT_DIM // 2).astype(ka_o.dtype)
    va_o[...] = proj(_C_VA, _C_VA + A_WIDTH).astype(va_o.dtype)

    qcn = _rms(proj(_C_QC, _C_QC + B_Q_RANK), qn_ref[...]).astype(jnp.bfloat16)
    kvn = _rms(proj(_C_KVC, _C_KVC + B_KV_RANK), kvn_ref[...]).astype(jnp.bfloat16)
    qbt = lax.dot_general(wuqt_ref[...], qcn, _NT_DIMS, preferred_element_type=jnp.float32)
    vbt = lax.dot_general(wuvt_ref[...], kvn, _NT_DIMS, preferred_element_type=jnp.float32)
    kb = jnp.dot(kvn, wuk_ref[...], preferred_element_type=jnp.float32)
    kpe = _rotate_block(proj(_C_KPE, _C_KPE + LANES), cb, sb, first_b, B_ROPE // 2)
    b_scale = B_QK ** -0.5 * LOG2_E
    cbt, sbt = cbt_ref[...], sbt_ref[...]
    row = lax.broadcasted_iota(jnp.int32, (LANES, 1), 0)
    r1, r2 = B_NOPE + B_ROPE // 2, B_QK
    for h in range(B_HEADS):
        blk = slice(h * LANES, (h + 1) * LANES)
        xt = qbt[blk, :]
        partner = jnp.concatenate([xt[:B_NOPE], xt[r1:r2], xt[B_NOPE:r1], xt[r2:]], axis=0)
        qbt_o[0, 0, blk, :] = ((xt * cbt + partner * sbt) * b_scale).astype(qbt_o.dtype)
        kb_o[:, blk] = (kb[:, blk] + kpe).astype(kb_o.dtype)
        vbt_o[0, 0, blk, :] = jnp.where(row < B_V, vbt[blk, :], 1.0).astype(vbt_o.dtype)

    gate_o[...] = jax.nn.sigmoid(proj(_C_GATE, _C_GATE + 2 * d_model) + bg_ref[...]).astype(gate_o.dtype)


def _in_proj(x2, g, w1, bg, qn, wuqt, kvn, wuk, wuvt, ca, sa, cb, sb, cbt, sbt, *, seq, tm):
    t, d_model = x2.shape
    n1 = w1.shape[1]
    steps_per_seq = seq // tm
    row = lambda i: (i, 0)
    const = lambda i: (0, 0)
    pos = lambda i: (i % steps_per_seq, 0)
    pos_t = lambda i: (0, i % steps_per_seq)
    tile_t = lambda i: (i // steps_per_seq, i % steps_per_seq, 0, 0)
    bf = jnp.bfloat16
    transposed = jax.ShapeDtypeStruct((t // seq, steps_per_seq, B_WIDTH, tm), bf)
    out_shape = (
        jax.ShapeDtypeStruct((t, A_WIDTH), bf), jax.ShapeDtypeStruct((t, A_WIDTH), bf),
        jax.ShapeDtypeStruct((t, A_WIDTH), bf),
        transposed, jax.ShapeDtypeStruct((t, B_WIDTH), bf), transposed,
        jax.ShapeDtypeStruct((t, 2 * d_model), bf),
    )
    return pl.pallas_call(
        functools.partial(_in_proj_kernel, d_model=d_model),
        grid=(t // tm,),
        in_specs=[
            pl.BlockSpec((tm, d_model), row),
            pl.BlockSpec((1, d_model), const),
            pl.BlockSpec((d_model, n1), const),
            pl.BlockSpec((1, 2 * d_model), const),
            pl.BlockSpec((1, B_Q_RANK), const),
            pl.BlockSpec((B_WIDTH, B_Q_RANK), const),
            pl.BlockSpec((1, B_KV_RANK), const),
            pl.BlockSpec((B_KV_RANK, B_WIDTH), const),
            pl.BlockSpec((B_WIDTH, B_KV_RANK), const),
            pl.BlockSpec((tm, LANES), pos), pl.BlockSpec((tm, LANES), pos),
            pl.BlockSpec((tm, LANES), pos), pl.BlockSpec((tm, LANES), pos),
            pl.BlockSpec((LANES, tm), pos_t), pl.BlockSpec((LANES, tm), pos_t),
        ],
        out_specs=(
            pl.BlockSpec((tm, A_WIDTH), row), pl.BlockSpec((tm, A_WIDTH), row),
            pl.BlockSpec((tm, A_WIDTH), row),
            pl.BlockSpec((1, 1, B_WIDTH, tm), tile_t), pl.BlockSpec((tm, B_WIDTH), row),
            pl.BlockSpec((1, 1, B_WIDTH, tm), tile_t),
            pl.BlockSpec((tm, 2 * d_model), row),
        ),
        out_shape=out_shape,
        compiler_params=pltpu.CompilerParams(dimension_semantics=("parallel",), vmem_limit_bytes=VMEM_LIMIT),
        name="in_proj",
    )(x2, g, w1, bg, qn, wuqt, kvn, wuk, wuvt, ca, sa, cb, sb, cbt, sbt)


def _dilated_group(qf, kf, vf, m_sc, l_sc, acc_sc, *, seq, window, dilation, first):
    length = seq // dilation
    radius = window // (2 * dilation)
    win = A_TILE + 2 * radius
    tiles = length // A_TILE
    iters_per_residue = tiles // A_TILES_PER_ITER
    lane = lax.broadcasted_iota(jnp.int32, (1, LANES), 1)
    head0 = lane < A_HEAD_DIM
    delta = (lax.broadcasted_iota(jnp.int32, (2 * A_TILE, win), 0) % A_TILE
             - lax.broadcasted_iota(jnp.int32, (2 * A_TILE, win), 1))

    def strided(start, size):
        return pl.ds(start, size) if dilation == 1 else pl.ds(start, size, stride=dilation)

    def body(n, carry):
        r = n // iters_per_residue
        i0 = (n % iters_per_residue) * A_TILES_PER_ITER
        l0s = [(i0 + t) * A_TILE for t in range(A_TILES_PER_ITER)]
        wss = [jnp.clip(l0 - radius, 0, length - win) for l0 in l0s]
        ss = []
        for l0, ws in zip(l0s, wss):
            q = qf[strided(r + dilation * l0, A_TILE), :]
            k = kf[strided(r + dilation * ws, win), :].astype(jnp.bfloat16)
            q2 = jnp.concatenate([jnp.where(head0, q, 0.0), jnp.where(head0, 0.0, q)], axis=0)
            ss.append(lax.dot_general(q2.astype(jnp.bfloat16), k, _NT_DIMS, preferred_element_type=jnp.float32))
        stats = []
        for l0, ws, s in zip(l0s, wss, ss):
            dist = delta + (l0 - ws)
            s = jnp.where((dist <= radius) & (dist >= -radius), s, NEG_BIG)
            m = jnp.max(s, axis=-1, keepdims=True)
            p = jnp.exp2(s - m)
            stats.append((m, jnp.sum(p, axis=-1, keepdims=True), p.astype(jnp.bfloat16)))
        for l0, ws, (m, l, p) in zip(l0s, wss, stats):
            v = vf[strided(r + dilation * ws, win), :].astype(jnp.bfloat16)
            pv = jnp.dot(p, v, preferred_element_type=jnp.float32)
            acc_t = jnp.where(head0, pv[:A_TILE], pv[A_TILE:])
            m_t = jnp.where(head0, m[:A_TILE], m[A_TILE:])
            l_t = jnp.where(head0, l[:A_TILE], l[A_TILE:])
            rows = strided(r + dilation * l0, A_TILE)
            if first:
                m_sc[rows, :] = m_t
                l_sc[rows, :] = l_t
                acc_sc[rows, :] = acc_t
            else:
                m_old = m_sc[rows, :]
                m_new = jnp.maximum(m_old, m_t)
                a = jnp.exp2(m_old - m_new)
                b = jnp.exp2(m_t - m_new)
                m_sc[rows, :] = m_new
                l_sc[rows, :] = a * l_sc[rows, :] + b * l_t
                acc_sc[rows, :] = a * acc_sc[rows, :] + b * acc_t
        return carry

    lax.fori_loop(0, dilation * iters_per_residue, body, 0)


def _dilated_kernel(q_ref, k_ref, v_ref, o_ref, qf, kf, vf, m_sc, l_sc, acc_sc, *, seq):
    group = pl.program_id(2)
    chunk = 512

    def widen(i, carry):
        rows = pl.ds(pl.multiple_of(i * chunk, chunk), chunk)
        qf[rows, :] = q_ref[0, rows, :].astype(jnp.float32)
        kf[rows, :] = k_ref[0, rows, :].astype(jnp.float32)
        vf[rows, :] = v_ref[0, rows, :].astype(jnp.float32)
        return carry

    lax.fori_loop(0, seq // chunk, widen, 0)

    for gi, (window, dilation) in enumerate(A_GROUPS):
        @pl.when(group == gi)
        def _(window=window, dilation=dilation, gi=gi):
            _dilated_group(qf, kf, vf, m_sc, l_sc, acc_sc, seq=seq, window=window, dilation=dilation,
                           first=gi == 0)

    @pl.when(group == len(A_GROUPS) - 1)
    def _():
        def finish(i, carry):
            rows = pl.ds(pl.multiple_of(i * chunk, chunk), chunk)
            o_ref[0, rows, :] = (acc_sc[rows, :] / l_sc[rows, :]).astype(o_ref.dtype)
            return carry

        lax.fori_loop(0, seq // chunk, finish, 0)


def _dilated_attention(qa, ka, va):
    batch, seq, _ = qa.shape
    pairs = A_GROUP_WIDTH // LANES
    in_spec = pl.BlockSpec((1, seq, LANES), lambda b, hp, g: (b, 0, g * pairs + hp))
    state = pltpu.VMEM((seq, LANES), jnp.float32)
    return pl.pallas_call(
        functools.partial(_dilated_kernel, seq=seq),
        grid=(batch, pairs, len(A_GROUPS)),
        in_specs=[in_spec, in_spec, in_spec],
        out_specs=pl.BlockSpec((1, seq, LANES), lambda b, hp, g: (b, 0, hp)),
        out_shape=jax.ShapeDtypeStruct((batch, seq, A_GROUP_WIDTH), jnp.bfloat16),
        scratch_shapes=[state] * 6,
        compiler_params=pltpu.CompilerParams(dimension_semantics=("parallel", "parallel", "arbitrary"),
                                             vmem_limit_bytes=VMEM_LIMIT),
        name="dilated",
    )(qa, ka, va)


def _mla_kernel(qt_ref, k_ref, vt_ref, o_ref, sa_ref, sb_ref, pa_ref, pb_ref, *, n_groups):
    tq = qt_ref.shape[3]
    tk = vt_ref.shape[3]
    key_blocks = tk // MLA_KEY_BLOCK
    lane = lax.broadcasted_iota(jnp.int32, (1, LANES), 1)
    blks = [slice(h * LANES, (h + 1) * LANES) for h in range(2)]
    qts = [qt_ref[0, 0, blk, :] for blk in blks]

    def rows(c):
        return slice(c * MLA_KEY_BLOCK, (c + 1) * MLA_KEY_BLOCK)

    def qk(h, g, s_ref):
        base = g * tk if isinstance(g, int) else pl.multiple_of(g * tk, tk)
        for c in range(key_blocks):
            k = k_ref[0, pl.ds(base + c * MLA_KEY_BLOCK, MLA_KEY_BLOCK), blks[h]]
            s_ref[h, rows(c), :] = jnp.dot(k, qts[h], preferred_element_type=jnp.float32)

    def pv(h, g, p_ref):
        total = None
        for c in range(key_blocks):
            d = jnp.dot(vt_ref[0, g, blks[h], rows(c)], p_ref[h, rows(c), :], preferred_element_type=jnp.float32)
            total = d if total is None else total + d
        return total

    def softmax(h, m, s_ref, p_ref):
        m_new = m
        for c in range(key_blocks):
            m_new = jnp.maximum(m_new, jnp.max(s_ref[h, rows(c), :], axis=0, keepdims=True))
        for c in range(key_blocks):
            p_ref[h, rows(c), :] = jnp.exp2(s_ref[h, rows(c), :] - m_new).astype(p_ref.dtype)
        return m_new, jnp.exp2(m - m_new)

    def step(g, carry, s_read, s_write, p_read, p_write, first=False, last=False):
        if not last:
            for h in range(2):
                qk(h, g + 1, s_write)
        pvs = [None if first else pv(h, g - 1, p_read) for h in range(2)]
        new = []
        for h in range(2):
            m, acc, alpha = carry[h]
            m_new, alpha_new = softmax(h, m, s_read, p_write)
            new.append((m_new, acc if first else alpha * acc + pvs[h], alpha_new))
        return tuple(new)

    def even(g, carry, **kw):
        return step(g, carry, sa_ref, sb_ref, pb_ref, pa_ref, **kw)

    def odd(g, carry, **kw):
        return step(g, carry, sb_ref, sa_ref, pa_ref, pb_ref, **kw)

    for h in range(2):
        qk(h, 0, sa_ref)
    init = (jnp.full((1, tq), NEG_BIG, jnp.float32), jnp.zeros((LANES, tq), jnp.float32),
            jnp.ones((1, tq), jnp.float32))
    carry = even(0, (init, init), first=True)

    def body(t, carry):
        g = 2 * t + 1
        return even(g + 1, odd(g, carry))

    carry = lax.fori_loop(0, (n_groups - 2) // 2, body, carry)
    carry = odd(n_groups - 1, carry, last=True)
    accs = [(alpha * acc + pv(h, n_groups - 1, pb_ref)).T for h, (_, acc, alpha) in enumerate(carry)]
    outs = [a / pltpu.roll(a, LANES - B_V, 1) for a in accs]
    o_ref[0] = jnp.where(lane < B_V, outs[0], pltpu.roll(outs[1], B_V, 1)).astype(o_ref.dtype)


def _mla_attention(qbt, kb, vbt):
    batch, tiles, _, tile = qbt.shape
    seq = kb.shape[1]
    pairs = B_HEADS // 2
    return pl.pallas_call(
        functools.partial(_mla_kernel, n_groups=tiles),
        grid=(batch, pairs, tiles),
        in_specs=[
            pl.BlockSpec((1, 1, 2 * LANES, tile), lambda b, hp, i: (b, i, hp, 0)),
            pl.BlockSpec((1, seq, 2 * LANES), lambda b, hp, i: (b, 0, hp)),
            pl.BlockSpec((1, tiles, 2 * LANES, tile), lambda b, hp, i: (b, 0, hp, 0)),
        ],
        out_specs=pl.BlockSpec((1, tile, LANES), lambda b, hp, i: (b, i, hp)),
        out_shape=jax.ShapeDtypeStruct((batch, seq, B_HEADS * B_V), jnp.bfloat16),
        scratch_shapes=[pltpu.VMEM((2, tile, tile), jnp.float32), pltpu.VMEM((2, tile, tile), jnp.float32),
                        pltpu.VMEM((2, tile, tile), jnp.bfloat16), pltpu.VMEM((2, tile, tile), jnp.bfloat16)],
        compiler_params=pltpu.CompilerParams(dimension_semantics=("parallel", "parallel", "parallel"),
                                             vmem_limit_bytes=VMEM_LIMIT),
        name="mla_attn",
    )(qbt, kb, vbt)


def _mix_out_kernel(x_ref, oa_ref, ob_ref, gate_ref,
                    woa_ref, wob_ref, wout_ref, gpost_ref, gpre_ref, h_o, hn_o, *, d_model):
    o_a = jnp.dot(oa_ref[...], woa_ref[...], preferred_element_type=jnp.float32)
    o_b = jnp.dot(ob_ref[...], wob_ref[...], preferred_element_type=jnp.float32)
    gates = gate_ref[...].astype(jnp.float32)
    mixin = (gates[:, :d_model] * o_a + gates[:, d_model:] * o_b).astype(jnp.bfloat16)
    mix = jnp.dot(mixin, wout_ref[...], preferred_element_type=jnp.float32)
    h = x_ref[...] + _rms(mix, gpost_ref[...])
    h_o[...] = h
    hn_o[...] = _rms(h, gpre_ref[...]).astype(hn_o.dtype)


def _mix_out(x2, oa, ob, gates, woa, wob, wout, gpost, gpre, *, tm):
    t, d_model = x2.shape
    row = lambda i: (i, 0)
    const = lambda i: (0, 0)
    return pl.pallas_call(
        functools.partial(_mix_out_kernel, d_model=d_model),
        grid=(t // tm,),
        in_specs=[
            pl.BlockSpec((tm, d_model), row),
            pl.BlockSpec((tm, A_GROUP_WIDTH), row),
            pl.BlockSpec((tm, B_HEADS * B_V), row),
            pl.BlockSpec((tm, 2 * d_model), row),
            pl.BlockSpec((A_GROUP_WIDTH, d_model), const),
            pl.BlockSpec((B_HEADS * B_V, d_model), const),
            pl.BlockSpec((d_model, d_model), const),
            pl.BlockSpec((1, d_model), const),
            pl.BlockSpec((1, d_model), const),
        ],
        out_specs=(pl.BlockSpec((tm, d_model), row), pl.BlockSpec((tm, d_model), row)),
        out_shape=(jax.ShapeDtypeStruct((t, d_model), jnp.float32),
                   jax.ShapeDtypeStruct((t, d_model), jnp.bfloat16)),
        compiler_params=pltpu.CompilerParams(dimension_semantics=("parallel",), vmem_limit_bytes=VMEM_LIMIT),
        name="mix_out",
    )(x2, oa, ob, gates, woa, wob, wout, gpost, gpre)


def _ffn_kernel(h_ref, hn_ref, w1_ref, w2_ref, g_ref, out_o):
    u = jnp.dot(hn_ref[...], w1_ref[...], preferred_element_type=jnp.float32)
    u = jnp.square(jnp.maximum(u, 0.0)).astype(jnp.bfloat16)
    ff = jnp.dot(u, w2_ref[...], preferred_element_type=jnp.float32)
    out_o[...] = h_ref[...] + _rms(ff, g_ref[...])


def _ffn(h, hn, w1, w2, g, *, tm):
    t, d_model = h.shape
    d_ff = w1.shape[1]
    row = lambda i: (i, 0)
    const = lambda i: (0, 0)
    return pl.pallas_call(
        _ffn_kernel,
        grid=(t // tm,),
        in_specs=[
            pl.BlockSpec((tm, d_model), row), pl.BlockSpec((tm, d_model), row),
            pl.BlockSpec((d_model, d_ff), const), pl.BlockSpec((d_ff, d_model), const),
            pl.BlockSpec((1, d_model), const),
        ],
        out_specs=pl.BlockSpec((tm, d_model), row),
        out_shape=jax.ShapeDtypeStruct((t, d_model), jnp.float32),
        compiler_params=pltpu.CompilerParams(dimension_semantics=("parallel",), vmem_limit_bytes=VMEM_LIMIT),
        name="ffn",
    )(h, hn, w1, w2, g)


def _rotary_tables(seq, theta, rot_dim, period, offset):
    half = rot_dim // 2
    inv = jnp.float32(theta) ** (-(jnp.arange(half, dtype=jnp.float32) * 2.0 / rot_dim))
    ang = jnp.arange(seq, dtype=jnp.int32).astype(jnp.float32)[:, None] * inv[None, :]
    cos, sin = jnp.cos(ang), jnp.sin(ang)
    cos_p = jnp.ones((seq, period), jnp.float32).at[:, offset:offset + rot_dim].set(jnp.concatenate([cos, cos], 1))
    sin_p = jnp.zeros((seq, period), jnp.float32).at[:, offset:offset + rot_dim].set(jnp.concatenate([-sin, sin], 1))
    reps = LANES // period
    return jnp.tile(cos_p, (1, reps)), jnp.tile(sin_p, (1, reps))


def _pad_heads(w, heads, width):
    rows = w.shape[0]
    w = w.reshape(rows, heads, width)
    return jnp.pad(w, ((0, 0), (0, 0), (0, LANES - width))).reshape(rows, heads * LANES)


def kernel(x, norm_mix_pre, w_in, b_gate, mla_q_norm, mla_w_uq, mla_kv_norm, mla_w_ukv, w_o_a, w_o_b, w_out,
           norm_mix_post, norm_mlp_pre, w_ff1, w_ff2, norm_mlp_post):
    batch, seq, d_model = x.shape
    bf = jnp.bfloat16
    x2 = x.reshape(batch * seq, d_model)

    in_a = 3 * A_WIDTH
    c_kpe = in_a + B_Q_RANK + B_KV_RANK
    kpe_cols = jnp.pad(w_in[:, c_kpe:c_kpe + B_ROPE], ((0, 0), (B_NOPE, LANES - B_QK)))
    w1 = jnp.concatenate([w_in[:, :c_kpe], kpe_cols, w_in[:, c_kpe + B_ROPE:]], axis=1).astype(bf)
    wuqt = _pad_heads(mla_w_uq, B_HEADS, B_QK).T.astype(bf)
    wukv = mla_w_ukv.reshape(B_KV_RANK, B_HEADS, B_NOPE + B_V)
    wuk = _pad_heads(wukv[:, :, :B_NOPE].reshape(B_KV_RANK, B_HEADS * B_NOPE), B_HEADS, B_NOPE).astype(bf)
    wuvt = _pad_heads(wukv[:, :, B_NOPE:].reshape(B_KV_RANK, B_HEADS * B_V), B_HEADS, B_V).T.astype(bf)

    ca, sa = _rotary_tables(seq, A_ROPE_THETA, A_ROT_DIM, A_HEAD_DIM, 0)
    cb, sb = _rotary_tables(seq, B_ROPE_THETA, B_ROPE, LANES, B_NOPE)

    vec = lambda v: v.reshape(1, -1)
    qa, ka, va, qbt, kb, vbt, gates = _in_proj(
        x2, vec(norm_mix_pre), w1, vec(b_gate), vec(mla_q_norm), wuqt, vec(mla_kv_norm), wuk, wuvt,
        ca, sa, cb, sb, cb.T, sb.T, seq=seq, tm=512)

    shape_a = (batch, seq, A_WIDTH)
    oa = _dilated_attention(qa.reshape(shape_a), ka.reshape(shape_a), va.reshape(shape_a))
    oa = oa.reshape(batch * seq, A_GROUP_WIDTH)

    ob = _mla_attention(qbt, kb.reshape(batch, seq, B_WIDTH), vbt)
    ob = ob.reshape(batch * seq, B_HEADS * B_V)

    h, hn = _mix_out(x2, oa, ob, gates, w_o_a.astype(bf), w_o_b.astype(bf), w_out.astype(bf),
                     vec(norm_mix_post), vec(norm_mlp_pre), tm=512)
    out = _ffn(h, hn, w_ff1.astype(bf), w_ff2.astype(bf), vec(norm_mlp_post), tm=512)
    return out.reshape(batch, seq, d_model)
```

```python
import functools

import jax
import jax.numpy as jnp
from jax import lax
from jax.experimental import pallas as pl
from jax.experimental.pallas import tpu as pltpu

EPS = 1e-6
LANES = 128

A_HEAD_DIM = 64
A_HEADS_PER_GROUP = 4
A_GROUPS = ((128, 1), (512, 4), (2048, 16))
A_HEADS = A_HEADS_PER_GROUP * len(A_GROUPS)
A_WIDTH = A_HEADS * A_HEAD_DIM
A_GROUP_WIDTH = A_HEADS_PER_GROUP * A_HEAD_DIM
A_ROT_DIM = A_HEAD_DIM // 4
A_ROPE_THETA = 500000.0
A_TILE = 128
A_TILES_PER_ITER = 4
B_HEADS = 8
B_NOPE = 64
B_ROPE = 32
B_QK = B_NOPE + B_ROPE
B_V = 64
B_Q_RANK = 512
B_KV_RANK = 256
B_ROPE_THETA = 10000.0
B_WIDTH = B_HEADS * LANES
MLA_KEY_BLOCK = 256
MLA_PAIRS_PER_ITER = 1
MLA_VT_ROWS = B_V + 16

NEG_BIG = -1e30
LOG2_E = 1.4426950408889634
VMEM_LIMIT = 56 * 1024 * 1024


def _rms(x, g):
    return x * lax.rsqrt(jnp.mean(x * x, axis=-1, keepdims=True) + EPS) * g


def _rotate_block(xb, cos, sin_signed, first_half, half):
    up = pltpu.roll(xb, LANES - half, 1)
    down = pltpu.roll(xb, half, 1)
    return xb * cos + jnp.where(first_half, up, down) * sin_signed


_C_QA, _C_KA, _C_VA = 0, A_WIDTH, 2 * A_WIDTH
_C_QC = 3 * A_WIDTH
_C_KVC = _C_QC + B_Q_RANK
_C_KPE = _C_KVC + B_KV_RANK
_C_GATE = _C_KPE + LANES


_NT_DIMS = (((1,), (1,)), ((), ()))


def _in_proj_kernel(x_ref, g_ref, w_ref, bg_ref, qn_ref, wuqt_ref, kvn_ref, wuk_ref, wuvt_ref,
                    ca_ref, sa_ref, cb_ref, sb_ref, cbt_ref, sbt_ref,
                    qa_o, ka_o, va_o, qbt_o, kb_o, vbt_o, gate_o, *, d_model):
    xn = _rms(x_ref[...], g_ref[...]).astype(jnp.bfloat16)

    def proj(lo, hi):
        return jnp.dot(xn, w_ref[:, lo:hi], preferred_element_type=jnp.float32)

    lane = lax.broadcasted_iota(jnp.int32, (1, LANES), 1)
    first_a = (lane % A_HEAD_DIM) < (A_ROT_DIM // 2)
    first_b = lane < (B_NOPE + B_ROPE // 2)
    ca, sa, cb, sb = ca_ref[...], sa_ref[...], cb_ref[...], sb_ref[...]

    qcn = _rms(proj(_C_QC, _C_QC + B_Q_RANK), qn_ref[...]).astype(jnp.bfloat16)
    kvn = _rms(proj(_C_KVC, _C_KVC + B_KV_RANK), kvn_ref[...]).astype(jnp.bfloat16)
    kpe = _rotate_block(proj(_C_KPE, _C_KPE + LANES), cb, sb, first_b, B_ROPE // 2)

    z = proj(_C_GATE, _C_GATE + 2 * d_model) + bg_ref[...]
    gate_o[...] = (0.5 * jnp.tanh(0.5 * z) + 0.5).astype(gate_o.dtype)

    qa = proj(_C_QA, _C_QA + A_WIDTH)
    ka = proj(_C_KA, _C_KA + A_WIDTH)
    a_scale = A_HEAD_DIM ** -0.5 * LOG2_E
    for j in range(A_WIDTH // LANES):
        blk = slice(j * LANES, (j + 1) * LANES)
        qa_o[:, blk] = (_rotate_block(qa[:, blk], ca, sa, first_a, A_ROT_DIM // 2) * a_scale).astype(qa_o.dtype)
        ka_o[:, blk] = _rotate_block(ka[:, blk], ca, sa, first_a, A_ROT_DIM // 2).astype(ka_o.dtype)
    va_o[...] = proj(_C_VA, _C_VA + A_WIDTH).astype(va_o.dtype)

    qbt = lax.dot_general(wuqt_ref[...], qcn, _NT_DIMS, preferred_element_type=jnp.float32)
    vbt = lax.dot_general(wuvt_ref[...], kvn, _NT_DIMS, preferred_element_type=jnp.float32)
    kb = jnp.dot(kvn, wuk_ref[...], preferred_element_type=jnp.float32)
    b_scale = B_QK ** -0.5 * LOG2_E
    cbt, sbt = cbt_ref[...], sbt_ref[...]
    row = lax.broadcasted_iota(jnp.int32, (LANES, 1), 0)
    r1, r2 = B_NOPE + B_ROPE // 2, B_QK
    for h in range(B_HEADS):
        blk = slice(h * LANES, (h + 1) * LANES)
        xt = qbt[blk, :]
        partner = jnp.concatenate([xt[:B_NOPE], xt[r1:r2], xt[B_NOPE:r1], xt[r2:]], axis=0)
        qbt_o[0, 0, blk, :] = ((xt * cbt + partner * sbt) * b_scale).astype(qbt_o.dtype)
        kb_o[:, blk] = (kb[:, blk] + kpe).astype(kb_o.dtype)
        vbt_o[0, 0, blk, :] = jnp.where(row < B_V, vbt[blk, :], 1.0).astype(vbt_o.dtype)


def _in_proj(x2, g, w1, bg, qn, wuqt, kvn, wuk, wuvt, ca, sa, cb, sb, cbt, sbt, *, seq, tm):
    t, d_model = x2.shape
    n1 = w1.shape[1]
    steps_per_seq = seq // tm
    row = lambda i: (i, 0)
    const = lambda i: (0, 0)
    pos = lambda i: (i % steps_per_seq, 0)
    pos_t = lambda i: (0, i % steps_per_seq)
    tile_t = lambda i: (i // steps_per_seq, i % steps_per_seq, 0, 0)
    bf = jnp.bfloat16
    transposed = jax.ShapeDtypeStruct((t // seq, steps_per_seq, B_WIDTH, tm), bf)
    out_shape = (
        jax.ShapeDtypeStruct((t, A_WIDTH), bf), jax.ShapeDtypeStruct((t, A_WIDTH), bf),
        jax.ShapeDtypeStruct((t, A_WIDTH), bf),
        transposed, jax.ShapeDtypeStruct((t, B_WIDTH), bf), transposed,
        jax.ShapeDtypeStruct((t, 2 * d_model), bf),
    )
    return pl.pallas_call(
        functools.partial(_in_proj_kernel, d_model=d_model),
        grid=(t // tm,),
        in_specs=[
            pl.BlockSpec((tm, d_model), row),
            pl.BlockSpec((1, d_model), const),
            pl.BlockSpec((d_model, n1), const),
            pl.BlockSpec((1, 2 * d_model), const),
            pl.BlockSpec((1, B_Q_RANK), const),
            pl.BlockSpec((B_WIDTH, B_Q_RANK), const),
            pl.BlockSpec((1, B_KV_RANK), const),
            pl.BlockSpec((B_KV_RANK, B_WIDTH), const),
            pl.BlockSpec((B_WIDTH, B_KV_RANK), const),
            pl.BlockSpec((tm, LANES), pos), pl.BlockSpec((tm, LANES), pos),
            pl.BlockSpec((tm, LANES), pos), pl.BlockSpec((tm, LANES), pos),
            pl.BlockSpec((LANES, tm), pos_t), pl.BlockSpec((LANES, tm), pos_t),
        ],
        out_specs=(
            pl.BlockSpec((tm, A_WIDTH), row), pl.BlockSpec((tm, A_WIDTH), row),
            pl.BlockSpec((tm, A_WIDTH), row),
            pl.BlockSpec((1, 1, B_WIDTH, tm), tile_t), pl.BlockSpec((tm, B_WIDTH), row),
            pl.BlockSpec((1, 1, B_WIDTH, tm), tile_t),
            pl.BlockSpec((tm, 2 * d_model), row),
        ),
        out_shape=out_shape,
        compiler_params=pltpu.CompilerParams(dimension_semantics=("parallel",), vmem_limit_bytes=VMEM_LIMIT),
        name="in_proj",
    )(x2, g, w1, bg, qn, wuqt, kvn, wuk, wuvt, ca, sa, cb, sb, cbt, sbt)


def _dilated_group(qf, kf, vf, m_sc, l_sc, acc_sc, *, seq, window, dilation, first):
    length = seq // dilation
    radius = window // (2 * dilation)
    win = A_TILE + 2 * radius
    tiles = length // A_TILE
    iters_per_residue = tiles // A_TILES_PER_ITER
    lane = lax.broadcasted_iota(jnp.int32, (1, LANES), 1)
    head0 = lane < A_HEAD_DIM
    delta = (lax.broadcasted_iota(jnp.int32, (2 * A_TILE, win), 0) % A_TILE
             - lax.broadcasted_iota(jnp.int32, (2 * A_TILE, win), 1))

    def strided(start, size):
        return pl.ds(start, size) if dilation == 1 else pl.ds(start, size, stride=dilation)

    def body(n, carry):
        r = n // iters_per_residue
        i0 = (n % iters_per_residue) * A_TILES_PER_ITER
        l0s = [(i0 + t) * A_TILE for t in range(A_TILES_PER_ITER)]
        wss = [jnp.clip(l0 - radius, 0, length - win) for l0 in l0s]
        ss = []
        for l0, ws in zip(l0s, wss):
            q = qf[strided(r + dilation * l0, A_TILE), :]
            k = kf[strided(r + dilation * ws, win), :].astype(jnp.bfloat16)
            q2 = jnp.concatenate([jnp.where(head0, q, 0.0), jnp.where(head0, 0.0, q)], axis=0)
            ss.append(lax.dot_general(q2.astype(jnp.bfloat16), k, _NT_DIMS, preferred_element_type=jnp.float32))
        stats = []
        for l0, ws, s in zip(l0s, wss, ss):
            dist = delta + (l0 - ws)
            s = jnp.where((dist <= radius) & (dist >= -radius), s, NEG_BIG)
            m = jnp.max(s, axis=-1, keepdims=True)
            p = jnp.exp2(s - m)
            stats.append((m, jnp.sum(p, axis=-1, keepdims=True), p.astype(jnp.bfloat16)))
        for l0, ws, (m, l, p) in zip(l0s, wss, stats):
            v = vf[strided(r + dilation * ws, win), :].astype(jnp.bfloat16)
            pv = jnp.dot(p, v, preferred_element_type=jnp.float32)
            acc_t = jnp.where(head0, pv[:A_TILE], pv[A_TILE:])
            m_t = jnp.where(head0, m[:A_TILE], m[A_TILE:])
            l_t = jnp.where(head0, l[:A_TILE], l[A_TILE:])
            rows = strided(r + dilation * l0, A_TILE)
            if first:
                m_sc[rows, :] = m_t
                l_sc[rows, :] = l_t
                acc_sc[rows, :] = acc_t
            else:
                m_old = m_sc[rows, :]
                m_new = jnp.maximum(m_old, m_t)
                a = jnp.exp2(m_old - m_new)
                b = jnp.exp2(m_t - m_new)
                m_sc[rows, :] = m_new
                l_sc[rows, :] = a * l_sc[rows, :] + b * l_t
                acc_sc[rows, :] = a * acc_sc[rows, :] + b * acc_t
        return carry

    lax.fori_loop(0, dilation * iters_per_residue, body, 0)


def _dilated_kernel(q_ref, k_ref, v_ref, o_ref, qf, kf, vf, m_sc, l_sc, acc_sc, *, seq):
    group = pl.program_id(2)
    chunk = 512

    def widen(i, carry):
        rows = pl.ds(pl.multiple_of(i * chunk, chunk), chunk)
        qf[rows, :] = q_ref[0, rows, :].astype(jnp.float32)
        kf[rows, :] = k_ref[0, rows, :].astype(jnp.float32)
        vf[rows, :] = v_ref[0, rows, :].astype(jnp.float32)
        return carry

    lax.fori_loop(0, seq // chunk, widen, 0)

    for gi, (window, dilation) in enumerate(A_GROUPS):
        @pl.when(group == gi)
        def _(window=window, dilation=dilation, gi=gi):
            _dilated_group(qf, kf, vf, m_sc, l_sc, acc_sc, seq=seq, window=window, dilation=dilation,
                           first=gi == 0)

    @pl.when(group == len(A_GROUPS) - 1)
    def _():
        def finish(i, carry):
            rows = pl.ds(pl.multiple_of(i * chunk, chunk), chunk)
            o_ref[0, rows, :] = (acc_sc[rows, :] / l_sc[rows, :]).astype(o_ref.dtype)
            return carry

        lax.fori_loop(0, seq // chunk, finish, 0)


def _dilated_attention(qa, ka, va):
    batch, seq, _ = qa.shape
    pairs = A_GROUP_WIDTH // LANES
    in_spec = pl.BlockSpec((1, seq, LANES), lambda b, hp, g: (b, 0, g * pairs + hp))
    state = pltpu.VMEM((seq, LANES), jnp.float32)
    return pl.pallas_call(
        functools.partial(_dilated_kernel, seq=seq),
        grid=(batch, pairs, len(A_GROUPS)),
        in_specs=[in_spec, in_spec, in_spec],
        out_specs=pl.BlockSpec((1, seq, LANES), lambda b, hp, g: (b, 0, hp)),
        out_shape=jax.ShapeDtypeStruct((batch, seq, A_GROUP_WIDTH), jnp.bfloat16),
        scratch_shapes=[state] * 6,
        compiler_params=pltpu.CompilerParams(dimension_semantics=("parallel", "parallel", "arbitrary"),
                                             vmem_limit_bytes=VMEM_LIMIT),
        name="dilated",
    )(qa, ka, va)


def _mla_kernel(qt_ref, k_ref, vt_ref, o_ref, sa_ref, sb_ref, pa_ref, pb_ref, *, n_groups):
    tq = qt_ref.shape[3]
    tk = vt_ref.shape[3]
    key_blocks = tk // MLA_KEY_BLOCK
    lane = lax.broadcasted_iota(jnp.int32, (1, LANES), 1)
    blks = [slice(h * LANES, (h + 1) * LANES) for h in range(2)]
    qts = [qt_ref[0, 0, blk, :] for blk in blks]

    def rows(c):
        return slice(c * MLA_KEY_BLOCK, (c + 1) * MLA_KEY_BLOCK)

    def qk_block(h, g, c, s_ref):
        base = g * tk if isinstance(g, int) else pl.multiple_of(g * tk, tk)
        k = k_ref[0, pl.ds(base + c * MLA_KEY_BLOCK, MLA_KEY_BLOCK), blks[h]]
        s = jnp.dot(k, qts[h], preferred_element_type=jnp.float32)
        s_ref[h, rows(c), :] = s
        return jnp.max(s, axis=0, keepdims=True)

    def pv_block(h, g, c, p_ref):
        vt = vt_ref[0, g, h * LANES:h * LANES + MLA_VT_ROWS, rows(c)]
        return jnp.dot(vt, p_ref[h, rows(c), :], preferred_element_type=jnp.float32)

    def matmuls(g_qk, s_ref, g_pv, p_ref):
        tops, pvs = [None, None], [None, None]
        for h in range(2):
            for c in range(key_blocks if g_qk is not None else 0):
                top = qk_block(h, g_qk, c, s_ref)
                tops[h] = top if tops[h] is None else jnp.maximum(tops[h], top)
        for h in range(2):
            for c in range(key_blocks if g_pv is not None else 0):
                d = pv_block(h, g_pv, c, p_ref)
                pvs[h] = d if pvs[h] is None else pvs[h] + d
        return tops, pvs

    def softmax(h, m, top, s_ref, p_ref):
        m_new = jnp.maximum(m, top)
        for c in range(key_blocks):
            p_ref[h, rows(c), :] = jnp.exp2((s_ref[h, rows(c), :] - m_new).astype(p_ref.dtype))
        return m_new, jnp.exp2(m - m_new)

    def step(g, carry, s_read, s_write, p_read, p_write, first=False, last=False):
        tops, pvs = matmuls(None if last else g + 1, s_write, None if first else g - 1, p_read)
        new = []
        for h in range(2):
            m, acc, alpha, top = carry[h]
            m_new, alpha_new = softmax(h, m, top, s_read, p_write)
            new.append((m_new, acc if first else alpha * acc + pvs[h], alpha_new, tops[h]))
        return tuple(new)

    def even(g, carry, **kw):
        return step(g, carry, sa_ref, sb_ref, pb_ref, pa_ref, **kw)

    def odd(g, carry, **kw):
        return step(g, carry, sb_ref, sa_ref, pa_ref, pb_ref, **kw)

    tops, _ = matmuls(0, sa_ref, None, None)
    init = tuple((jnp.full((1, tq), NEG_BIG, jnp.float32), jnp.zeros((MLA_VT_ROWS, tq), jnp.float32),
                  jnp.ones((1, tq), jnp.float32), tops[h]) for h in range(2))
    carry = even(0, init, first=True)

    def pairs(g, carry, count):
        for i in range(count):
            carry = even(g + 2 * i + 1, odd(g + 2 * i, carry))
        return carry

    n_pairs = (n_groups - 2) // 2
    trips = n_pairs // MLA_PAIRS_PER_ITER
    carry = lax.fori_loop(
        0, trips, lambda t, c: pairs(2 * MLA_PAIRS_PER_ITER * t + 1, c, MLA_PAIRS_PER_ITER), carry)
    carry = pairs(2 * MLA_PAIRS_PER_ITER * trips + 1, carry, n_pairs - trips * MLA_PAIRS_PER_ITER)
    carry = odd(n_groups - 1, carry, last=True)
    _, pvs = matmuls(None, None, n_groups - 1, pb_ref)
    accs = [alpha * acc + pvs[h] for h, (_, acc, alpha, _) in enumerate(carry)]
    outs = [a[:B_V] / a[B_V:B_V + 1] for a in accs]
    o_ref[0] = jnp.concatenate(outs, axis=0).T.astype(o_ref.dtype)


def _mla_attention(qbt, kb, vbt):
    batch, tiles, _, tile = qbt.shape
    seq = kb.shape[1]
    pairs = B_HEADS // 2
    return pl.pallas_call(
        functools.partial(_mla_kernel, n_groups=tiles),
        grid=(batch, pairs, tiles),
        in_specs=[
            pl.BlockSpec((1, 1, 2 * LANES, tile), lambda b, hp, i: (b, i, hp, 0)),
            pl.BlockSpec((1, seq, 2 * LANES), lambda b, hp, i: (b, 0, hp)),
            pl.BlockSpec((1, tiles, 2 * LANES, tile), lambda b, hp, i: (b, 0, hp, 0)),
        ],
        out_specs=pl.BlockSpec((1, tile, LANES), lambda b, hp, i: (b, i, hp)),
        out_shape=jax.ShapeDtypeStruct((batch, seq, B_HEADS * B_V), jnp.bfloat16),
        scratch_shapes=[pltpu.VMEM((2, tile, tile), jnp.float32), pltpu.VMEM((2, tile, tile), jnp.float32),
                        pltpu.VMEM((2, tile, tile), jnp.bfloat16), pltpu.VMEM((2, tile, tile), jnp.bfloat16)],
        compiler_params=pltpu.CompilerParams(dimension_semantics=("parallel", "parallel", "parallel"),
                                             vmem_limit_bytes=VMEM_LIMIT),
        name="mla_attn",
    )(qbt, kb, vbt)


def _mix_out_kernel(x_ref, oa_ref, ob_ref, gate_ref,
                    woa_ref, wob_ref, wout_ref, gpost_ref, gpre_ref, h_o, hn_o, *, d_model):
    o_a = jnp.dot(oa_ref[...], woa_ref[...], preferred_element_type=jnp.float32)
    o_b = jnp.dot(ob_ref[...], wob_ref[...], preferred_element_type=jnp.float32)
    gates = gate_ref[...].astype(jnp.float32)
    mixin = (gates[:, :d_model] * o_a + gates[:, d_model:] * o_b).astype(jnp.bfloat16)
    mix = jnp.dot(mixin, wout_ref[...], preferred_element_type=jnp.float32)
    h = x_ref[...] + _rms(mix, gpost_ref[...])
    h_o[...] = h
    hn_o[...] = _rms(h, gpre_ref[...]).astype(hn_o.dtype)


def _mix_out(x2, oa, ob, gates, woa, wob, wout, gpost, gpre, *, tm):
    t, d_model = x2.shape
    row = lambda i: (i, 0)
    const = lambda i: (0, 0)
    return pl.pallas_call(
        functools.partial(_mix_out_kernel, d_model=d_model),
        grid=(t // tm,),
        in_specs=[
            pl.BlockSpec((tm, d_model), row),
            pl.BlockSpec((tm, A_GROUP_WIDTH), row),
            pl.BlockSpec((tm, B_HEADS * B_V), row),
            pl.BlockSpec((tm, 2 * d_model), row),
            pl.BlockSpec((A_GROUP_WIDTH, d_model), const),
            pl.BlockSpec((B_HEADS * B_V, d_model), const),
            pl.BlockSpec((d_model, d_model), const),
            pl.BlockSpec((1, d_model), const),
            pl.BlockSpec((1, d_model), const),
        ],
        out_specs=(pl.BlockSpec((tm, d_model), row), pl.BlockSpec((tm, d_model), row)),
        out_shape=(jax.ShapeDtypeStruct((t, d_model), jnp.float32),
                   jax.ShapeDtypeStruct((t, d_model), jnp.bfloat16)),
        compiler_params=pltpu.CompilerParams(dimension_semantics=("parallel",), vmem_limit_bytes=VMEM_LIMIT),
        name="mix_out",
    )(x2, oa, ob, gates, woa, wob, wout, gpost, gpre)


def _ffn_kernel(h_ref, hn_ref, w1_ref, w2_ref, g_ref, out_o):
    u = jnp.dot(hn_ref[...], w1_ref[...], preferred_element_type=jnp.float32)
    u = jnp.square(jnp.maximum(u, 0.0)).astype(jnp.bfloat16)
    ff = jnp.dot(u, w2_ref[...], preferred_element_type=jnp.float32)
    out_o[...] = h_ref[...] + _rms(ff, g_ref[...])


def _ffn(h, hn, w1, w2, g, *, tm):
    t, d_model = h.shape
    d_ff = w1.shape[1]
    row = lambda i: (i, 0)
    const = lambda i: (0, 0)
    return pl.pallas_call(
        _ffn_kernel,
        grid=(t // tm,),
        in_specs=[
            pl.BlockSpec((tm, d_model), row), pl.BlockSpec((tm, d_model), row),
            pl.BlockSpec((d_model, d_ff), const), pl.BlockSpec((d_ff, d_model), const),
            pl.BlockSpec((1, d_model), const),
        ],
        out_specs=pl.BlockSpec((tm, d_model), row),
        out_shape=jax.ShapeDtypeStruct((t, d_model), jnp.float32),
        compiler_params=pltpu.CompilerParams(dimension_semantics=("parallel",), vmem_limit_bytes=VMEM_LIMIT),
        name="ffn",
    )(h, hn, w1, w2, g)


def _rotary_tables(seq, theta, rot_dim, period, offset):
    half = rot_dim // 2
    inv = jnp.float32(theta) ** (-(jnp.arange(half, dtype=jnp.float32) * 2.0 / rot_dim))
    ang = jnp.arange(seq, dtype=jnp.int32).astype(jnp.float32)[:, None] * inv[None, :]
    cos, sin = jnp.cos(ang), jnp.sin(ang)
    cos_p = jnp.ones((seq, period), jnp.float32).at[:, offset:offset + rot_dim].set(jnp.concatenate([cos, cos], 1))
    sin_p = jnp.zeros((seq, period), jnp.float32).at[:, offset:offset + rot_dim].set(jnp.concatenate([-sin, sin], 1))
    reps = LANES // period
    return jnp.tile(cos_p, (1, reps)), jnp.tile(sin_p, (1, reps))


def _pad_heads(w, heads, width):
    rows = w.shape[0]
    w = w.reshape(rows, heads, width)
    return jnp.pad(w, ((0, 0), (0, 0), (0, LANES - width))).reshape(rows, heads * LANES)


def kernel(x, norm_mix_pre, w_in, b_gate, mla_q_norm, mla_w_uq, mla_kv_norm, mla_w_ukv, w_o_a, w_o_b, w_out,
           norm_mix_post, norm_mlp_pre, w_ff1, w_ff2, norm_mlp_post):
    batch, seq, d_model = x.shape
    bf = jnp.bfloat16
    x2 = x.reshape(batch * seq, d_model)

    in_a = 3 * A_WIDTH
    c_kpe = in_a + B_Q_RANK + B_KV_RANK
    kpe_cols = jnp.pad(w_in[:, c_kpe:c_kpe + B_ROPE], ((0, 0), (B_NOPE, LANES - B_QK)))
    w1 = jnp.concatenate([w_in[:, :c_kpe], kpe_cols, w_in[:, c_kpe + B_ROPE:]], axis=1).astype(bf)
    wuqt = _pad_heads(mla_w_uq, B_HEADS, B_QK).T.astype(bf)
    wukv = mla_w_ukv.reshape(B_KV_RANK, B_HEADS, B_NOPE + B_V)
    wuk = _pad_heads(wukv[:, :, :B_NOPE].reshape(B_KV_RANK, B_HEADS * B_NOPE), B_HEADS, B_NOPE).astype(bf)
    wuvt = _pad_heads(wukv[:, :, B_NOPE:].reshape(B_KV_RANK, B_HEADS * B_V), B_HEADS, B_V).T.astype(bf)

    ca, sa = _rotary_tables(seq, A_ROPE_THETA, A_ROT_DIM, A_HEAD_DIM, 0)
    cb, sb = _rotary_tables(seq, B_ROPE_THETA, B_ROPE, LANES, B_NOPE)

    vec = lambda v: v.reshape(1, -1)
    qa, ka, va, qbt, kb, vbt, gates = _in_proj(
        x2, vec(norm_mix_pre), w1, vec(b_gate), vec(mla_q_norm), wuqt, vec(mla_kv_norm), wuk, wuvt,
        ca, sa, cb, sb, cb.T, sb.T, seq=seq, tm=512)

    shape_a = (batch, seq, A_WIDTH)
    oa = _dilated_attention(qa.reshape(shape_a), ka.reshape(shape_a), va.reshape(shape_a))
    oa = oa.reshape(batch * seq, A_GROUP_WIDTH)

    ob = _mla_attention(qbt, kb.reshape(batch, seq, B_WIDTH), vbt)
    ob = ob.reshape(batch * seq, B_HEADS * B_V)

    h, hn = _mix_out(x2, oa, ob, gates, w_o_a.astype(bf), w_o_b.astype(bf), w_out.astype(bf),
                     vec(norm_mix_post), vec(norm_mlp_pre), tm=512)
    out = _ffn(h, hn, w_ff1.astype(bf), w_ff2.astype(bf), vec(norm_mlp_post), tm=512)
    return out.reshape(batch, seq, d_model)
```

```python
import functools

import jax
import jax.numpy as jnp
from jax import lax
from jax.experimental import pallas as pl
from jax.experimental.pallas import tpu as pltpu

EPS = 1e-6
LANES = 128

A_HEAD_DIM = 64
A_HEADS_PER_GROUP = 4
A_GROUPS = ((128, 1), (512, 4), (2048, 16))
A_HEADS = A_HEADS_PER_GROUP * len(A_GROUPS)
A_WIDTH = A_HEADS * A_HEAD_DIM
A_GROUP_WIDTH = A_HEADS_PER_GROUP * A_HEAD_DIM
A_ROT_DIM = A_HEAD_DIM // 4
A_ROPE_THETA = 500000.0
A_TILE = 128
A_TILES_PER_ITER = 4
A_GROUP_ORDER = (2, 0, 1)
B_HEADS = 8
B_NOPE = 64
B_ROPE = 32
B_QK = B_NOPE + B_ROPE
B_V = 64
B_Q_RANK = 512
B_KV_RANK = 256
B_ROPE_THETA = 10000.0
B_WIDTH = B_HEADS * LANES
MLA_KEY_BLOCK = 256
MLA_PAIRS_PER_ITER = 1
MLA_HEADS_PER_STEP = 2
MLA_Q_TILES_PER_STEP = 1
MLA_VT_ROWS = B_V + 16

NEG_BIG = -1e30
LOG2_E = 1.4426950408889634
VMEM_LIMIT = 56 * 1024 * 1024


def _rms(x, g):
    return x * lax.rsqrt(jnp.mean(x * x, axis=-1, keepdims=True) + EPS) * g


def _rotate_block(xb, cos, sin_signed, first_half, half):
    up = pltpu.roll(xb, LANES - half, 1)
    down = pltpu.roll(xb, half, 1)
    return xb * cos + jnp.where(first_half, up, down) * sin_signed


_C_QA, _C_KA, _C_VA = 0, A_WIDTH, 2 * A_WIDTH
_C_QC = 3 * A_WIDTH
_C_KVC = _C_QC + B_Q_RANK
_C_KPE = _C_KVC + B_KV_RANK
_C_GATE = _C_KPE + LANES


_NT_DIMS = (((1,), (1,)), ((), ()))


def _in_proj_kernel(x_ref, g_ref, w_ref, bg_ref, qn_ref, wuqt_ref, kvn_ref, wuk_ref, wuvt_ref,
                    ca_ref, sa_ref, cb_ref, sb_ref, cbt_ref, sbt_ref,
                    qa_o, ka_o, va_o, qbt_o, kb_o, vbt_o, gate_o, *, d_model):
    xn = _rms(x_ref[...], g_ref[...]).astype(jnp.bfloat16)

    def proj(lo, hi):
        return jnp.dot(xn, w_ref[:, lo:hi], preferred_element_type=jnp.float32)

    lane = lax.broadcasted_iota(jnp.int32, (1, LANES), 1)
    first_a = (lane % A_HEAD_DIM) < (A_ROT_DIM // 2)
    first_b = lane < (B_NOPE + B_ROPE // 2)
    ca, sa, cb, sb = ca_ref[...], sa_ref[...], cb_ref[...], sb_ref[...]

    qcn = _rms(proj(_C_QC, _C_QC + B_Q_RANK), qn_ref[...]).astype(jnp.bfloat16)
    kvn = _rms(proj(_C_KVC, _C_KVC + B_KV_RANK), kvn_ref[...]).astype(jnp.bfloat16)
    kpe = _rotate_block(proj(_C_KPE, _C_KPE + LANES), cb, sb, first_b, B_ROPE // 2)

    z = proj(_C_GATE, _C_GATE + 2 * d_model) + bg_ref[...]
    gate_o[...] = (0.5 * jnp.tanh(0.5 * z) + 0.5).astype(gate_o.dtype)

    qa = proj(_C_QA, _C_QA + A_WIDTH)
    ka = proj(_C_KA, _C_KA + A_WIDTH)
    a_scale = A_HEAD_DIM ** -0.5 * LOG2_E
    for j in range(A_WIDTH // LANES):
        blk = slice(j * LANES, (j + 1) * LANES)
        qa_o[:, blk] = (_rotate_block(qa[:, blk], ca, sa, first_a, A_ROT_DIM // 2) * a_scale).astype(qa_o.dtype)
        ka_o[:, blk] = _rotate_block(ka[:, blk], ca, sa, first_a, A_ROT_DIM // 2).astype(ka_o.dtype)
    va_o[...] = proj(_C_VA, _C_VA + A_WIDTH).astype(va_o.dtype)

    qbt = lax.dot_general(wuqt_ref[...], qcn, _NT_DIMS, preferred_element_type=jnp.float32)
    vbt = lax.dot_general(wuvt_ref[...], kvn, _NT_DIMS, preferred_element_type=jnp.float32)
    kb = jnp.dot(kvn, wuk_ref[...], preferred_element_type=jnp.float32)
    b_scale = B_QK ** -0.5 * LOG2_E
    cbt, sbt = cbt_ref[...], sbt_ref[...]
    row = lax.broadcasted_iota(jnp.int32, (LANES, 1), 0)
    r1, r2 = B_NOPE + B_ROPE // 2, B_QK
    for h in range(B_HEADS):
        blk = slice(h * LANES, (h + 1) * LANES)
        xt = qbt[blk, :]
        partner = jnp.concatenate([xt[:B_NOPE], xt[r1:r2], xt[B_NOPE:r1], xt[r2:]], axis=0)
        qbt_o[0, 0, blk, :] = ((xt * cbt + partner * sbt) * b_scale).astype(qbt_o.dtype)
        kb_o[:, blk] = (kb[:, blk] + kpe).astype(kb_o.dtype)
        vbt_o[0, 0, blk, :] = jnp.where(row < B_V, vbt[blk, :], 1.0).astype(vbt_o.dtype)


def _in_proj(x2, g, w1, bg, qn, wuqt, kvn, wuk, wuvt, ca, sa, cb, sb, cbt, sbt, *, seq, tm):
    t, d_model = x2.shape
    n1 = w1.shape[1]
    steps_per_seq = seq // tm
    row = lambda i: (i, 0)
    const = lambda i: (0, 0)
    pos = lambda i: (i % steps_per_seq, 0)
    pos_t = lambda i: (0, i % steps_per_seq)
    tile_t = lambda i: (i // steps_per_seq, i % steps_per_seq, 0, 0)
    bf = jnp.bfloat16
    transposed = jax.ShapeDtypeStruct((t // seq, steps_per_seq, B_WIDTH, tm), bf)
    out_shape = (
        jax.ShapeDtypeStruct((t, A_WIDTH), bf), jax.ShapeDtypeStruct((t, A_WIDTH), bf),
        jax.ShapeDtypeStruct((t, A_WIDTH), bf),
        transposed, jax.ShapeDtypeStruct((t, B_WIDTH), bf), transposed,
        jax.ShapeDtypeStruct((t, 2 * d_model), bf),
    )
    return pl.pallas_call(
        functools.partial(_in_proj_kernel, d_model=d_model),
        grid=(t // tm,),
        in_specs=[
            pl.BlockSpec((tm, d_model), row),
            pl.BlockSpec((1, d_model), const),
            pl.BlockSpec((d_model, n1), const),
            pl.BlockSpec((1, 2 * d_model), const),
            pl.BlockSpec((1, B_Q_RANK), const),
            pl.BlockSpec((B_WIDTH, B_Q_RANK), const),
            pl.BlockSpec((1, B_KV_RANK), const),
            pl.BlockSpec((B_KV_RANK, B_WIDTH), const),
            pl.BlockSpec((B_WIDTH, B_KV_RANK), const),
            pl.BlockSpec((tm, LANES), pos), pl.BlockSpec((tm, LANES), pos),
            pl.BlockSpec((tm, LANES), pos), pl.BlockSpec((tm, LANES), pos),
            pl.BlockSpec((LANES, tm), pos_t), pl.BlockSpec((LANES, tm), pos_t),
        ],
        out_specs=(
            pl.BlockSpec((tm, A_WIDTH), row), pl.BlockSpec((tm, A_WIDTH), row),
            pl.BlockSpec((tm, A_WIDTH), row),
            pl.BlockSpec((1, 1, B_WIDTH, tm), tile_t), pl.BlockSpec((tm, B_WIDTH), row),
            pl.BlockSpec((1, 1, B_WIDTH, tm), tile_t),
            pl.BlockSpec((tm, 2 * d_model), row),
        ),
        out_shape=out_shape,
        compiler_params=pltpu.CompilerParams(dimension_semantics=("parallel",), vmem_limit_bytes=VMEM_LIMIT),
        name="in_proj",
    )(x2, g, w1, bg, qn, wuqt, kvn, wuk, wuvt, ca, sa, cb, sb, cbt, sbt)


def _dilated_group(qf, kf, vf, m_sc, l_sc, acc_sc, *, seq, window, dilation, first):
    length = seq // dilation
    radius = window // (2 * dilation)
    win = A_TILE + 2 * radius
    tiles = length // A_TILE
    iters_per_residue = tiles // A_TILES_PER_ITER
    lane = lax.broadcasted_iota(jnp.int32, (1, LANES), 1)
    head0 = lane < A_HEAD_DIM
    delta = (lax.broadcasted_iota(jnp.int32, (2 * A_TILE, win), 0) % A_TILE
             - lax.broadcasted_iota(jnp.int32, (2 * A_TILE, win), 1))

    def strided(start, size):
        return pl.ds(start, size) if dilation == 1 else pl.ds(start, size, stride=dilation)

    def body(n, carry):
        r = n // iters_per_residue
        i0 = (n % iters_per_residue) * A_TILES_PER_ITER
        l0s = [(i0 + t) * A_TILE for t in range(A_TILES_PER_ITER)]
        wss = [jnp.clip(l0 - radius, 0, length - win) for l0 in l0s]
        ss = []
        for l0, ws in zip(l0s, wss):
            q = qf[strided(r + dilation * l0, A_TILE), :]
            k = kf[strided(r + dilation * ws, win), :].astype(jnp.bfloat16)
            q2 = jnp.concatenate([jnp.where(head0, q, 0.0), jnp.where(head0, 0.0, q)], axis=0)
            ss.append(lax.dot_general(q2.astype(jnp.bfloat16), k, _NT_DIMS, preferred_element_type=jnp.float32))
        stats = []
        for l0, ws, s in zip(l0s, wss, ss):
            dist = delta + (l0 - ws)
            s = jnp.where((dist <= radius) & (dist >= -radius), s, NEG_BIG)
            m = jnp.max(s, axis=-1, keepdims=True)
            p = jnp.exp2(s - m)
            stats.append((m, jnp.sum(p, axis=-1, keepdims=True), p.astype(jnp.bfloat16)))
        for l0, ws, (m, l, p) in zip(l0s, wss, stats):
            v = vf[strided(r + dilation * ws, win), :].astype(jnp.bfloat16)
            pv = jnp.dot(p, v, preferred_element_type=jnp.float32)
            acc_t = jnp.where(head0, pv[:A_TILE], pv[A_TILE:])
            m_t = jnp.where(head0, m[:A_TILE], m[A_TILE:])
            l_t = jnp.where(head0, l[:A_TILE], l[A_TILE:])
            rows = strided(r + dilation * l0, A_TILE)
            if first:
                m_sc[rows, :] = m_t
                l_sc[rows, :] = l_t
                acc_sc[rows, :] = acc_t
            else:
                m_old = m_sc[rows, :]
                m_new = jnp.maximum(m_old, m_t)
                a = jnp.exp2(m_old - m_new)
                b = jnp.exp2(m_t - m_new)
                m_sc[rows, :] = m_new
                l_sc[rows, :] = a * l_sc[rows, :] + b * l_t
                acc_sc[rows, :] = a * acc_sc[rows, :] + b * acc_t
        return carry

    lax.fori_loop(0, dilation * iters_per_residue, body, 0)


def _dilated_kernel(q_ref, k_ref, v_ref, o_ref, qf, kf, vf, m_sc, l_sc, acc_sc, *, seq):
    turn = pl.program_id(2)
    chunk = 512

    def widen(i, carry):
        rows = pl.ds(pl.multiple_of(i * chunk, chunk), chunk)
        qf[rows, :] = q_ref[0, rows, :].astype(jnp.float32)
        kf[rows, :] = k_ref[0, rows, :].astype(jnp.float32)
        vf[rows, :] = v_ref[0, rows, :].astype(jnp.float32)
        return carry

    lax.fori_loop(0, seq // chunk, widen, 0)

    for t, gi in enumerate(A_GROUP_ORDER):
        window, dilation = A_GROUPS[gi]

        @pl.when(turn == t)
        def _(window=window, dilation=dilation, t=t):
            _dilated_group(qf, kf, vf, m_sc, l_sc, acc_sc, seq=seq, window=window, dilation=dilation,
                           first=t == 0)

    @pl.when(turn == len(A_GROUPS) - 1)
    def _():
        def finish(i, carry):
            rows = pl.ds(pl.multiple_of(i * chunk, chunk), chunk)
            o_ref[0, rows, :] = (acc_sc[rows, :] / l_sc[rows, :]).astype(o_ref.dtype)
            return carry

        lax.fori_loop(0, seq // chunk, finish, 0)


def _dilated_attention(qa, ka, va):
    batch, seq, _ = qa.shape
    pairs = A_GROUP_WIDTH // LANES
    first, n = A_GROUP_ORDER[0], len(A_GROUPS)
    assert A_GROUP_ORDER == tuple((first + t) % n for t in range(n))
    in_spec = pl.BlockSpec((1, seq, LANES), lambda b, hp, t: (b, 0, ((first + t) % n) * pairs + hp))
    state = pltpu.VMEM((seq, LANES), jnp.float32)
    return pl.pallas_call(
        functools.partial(_dilated_kernel, seq=seq),
        grid=(batch, pairs, len(A_GROUPS)),
        in_specs=[in_spec, in_spec, in_spec],
        out_specs=pl.BlockSpec((1, seq, LANES), lambda b, hp, g: (b, 0, hp)),
        out_shape=jax.ShapeDtypeStruct((batch, seq, A_GROUP_WIDTH), jnp.bfloat16),
        scratch_shapes=[state] * 6,
        compiler_params=pltpu.CompilerParams(dimension_semantics=("parallel", "parallel", "arbitrary"),
                                             vmem_limit_bytes=VMEM_LIMIT),
        name="dilated",
    )(qa, ka, va)


def _mla_kernel(qt_ref, k_ref, vt_ref, o_ref, *scratch, n_groups):
    tq = qt_ref.shape[3]
    tk = vt_ref.shape[3]
    n_streams = MLA_HEADS_PER_STEP * MLA_Q_TILES_PER_STEP
    sa_ref, sb_ref, pa_ref, pb_ref = (scratch[i * n_streams:(i + 1) * n_streams] for i in range(4))
    key_blocks = tk // MLA_KEY_BLOCK
    lane = lax.broadcasted_iota(jnp.int32, (1, LANES), 1)
    heads = range(n_streams)
    blks = [slice((st % MLA_HEADS_PER_STEP) * LANES, (st % MLA_HEADS_PER_STEP + 1) * LANES) for st in heads]
    qts = [qt_ref[0, st // MLA_HEADS_PER_STEP, blks[st], :] for st in heads]

    def rows(c):
        return slice(c * MLA_KEY_BLOCK, (c + 1) * MLA_KEY_BLOCK)

    def qk_block(h, g, c, s_ref):
        base = g * tk if isinstance(g, int) else pl.multiple_of(g * tk, tk)
        k = k_ref[0, pl.ds(base + c * MLA_KEY_BLOCK, MLA_KEY_BLOCK), blks[h]]
        s = jnp.dot(k, qts[h], preferred_element_type=jnp.float32)
        s_ref[h][rows(c), :] = s
        return jnp.max(s, axis=0, keepdims=True)

    def pv(h, g, p_ref):
        vt = vt_ref[0, g, blks[h].start:blks[h].start + MLA_VT_ROWS, :]
        return jnp.dot(vt, p_ref[h][...], preferred_element_type=jnp.float32)

    def matmuls(g_qk, s_ref, g_pv, p_ref):
        tops, pvs = [None for _ in heads], [None for _ in heads]
        for h in heads:
            for c in range(key_blocks if g_qk is not None else 0):
                top = qk_block(h, g_qk, c, s_ref)
                tops[h] = top if tops[h] is None else jnp.maximum(tops[h], top)
        if g_pv is not None:
            pvs = [pv(h, g_pv, p_ref) for h in heads]
        return tops, pvs

    def softmax(h, m, top, s_ref, p_ref):
        m_new = jnp.maximum(m, top)
        for c in range(key_blocks):
            p_ref[h][rows(c), :] = jnp.exp2((s_ref[h][rows(c), :] - m_new).astype(p_ref[h].dtype))
        return m_new, jnp.exp2(m - m_new)

    def step(g, carry, s_read, s_write, p_read, p_write, first=False, last=False):
        tops, pvs = matmuls(None if last else g + 1, s_write, None if first else g - 1, p_read)
        new = []
        for h in heads:
            m, acc, alpha, top = carry[h]
            m_new, alpha_new = softmax(h, m, top, s_read, p_write)
            new.append((m_new, acc if first else alpha * acc + pvs[h], alpha_new, tops[h]))
        return tuple(new)

    def even(g, carry, **kw):
        return step(g, carry, sa_ref, sb_ref, pb_ref, pa_ref, **kw)

    def odd(g, carry, **kw):
        return step(g, carry, sb_ref, sa_ref, pa_ref, pb_ref, **kw)

    tops, _ = matmuls(0, sa_ref, None, None)
    init = tuple((jnp.full((1, tq), NEG_BIG, jnp.float32), jnp.zeros((MLA_VT_ROWS, tq), jnp.float32),
                  jnp.ones((1, tq), jnp.float32), tops[h]) for h in heads)
    carry = even(0, init, first=True)

    def pairs(g, carry, count):
        for i in range(count):
            carry = even(g + 2 * i + 1, odd(g + 2 * i, carry))
        return carry

    n_pairs = (n_groups - 2) // 2
    trips = n_pairs // MLA_PAIRS_PER_ITER
    carry = lax.fori_loop(
        0, trips, lambda t, c: pairs(2 * MLA_PAIRS_PER_ITER * t + 1, c, MLA_PAIRS_PER_ITER), carry)
    carry = pairs(2 * MLA_PAIRS_PER_ITER * trips + 1, carry, n_pairs - trips * MLA_PAIRS_PER_ITER)
    carry = odd(n_groups - 1, carry, last=True)
    _, pvs = matmuls(None, None, n_groups - 1, pb_ref)
    accs = [alpha * acc + pvs[h] for h, (_, acc, alpha, _) in enumerate(carry)]
    outs = [a[:B_V] / a[B_V:B_V + 1] for a in accs]
    for t in range(MLA_Q_TILES_PER_STEP):
        tile_outs = outs[t * MLA_HEADS_PER_STEP:(t + 1) * MLA_HEADS_PER_STEP]
        o_ref[0, t * tq:(t + 1) * tq, :] = jnp.concatenate(tile_outs, axis=0).T.astype(o_ref.dtype)


def _mla_attention(qbt, kb, vbt):
    batch, tiles, _, tile = qbt.shape
    seq = kb.shape[1]
    n, qt = MLA_HEADS_PER_STEP, MLA_Q_TILES_PER_STEP
    return pl.pallas_call(
        functools.partial(_mla_kernel, n_groups=tiles),
        grid=(batch, B_HEADS // n, tiles // qt),
        in_specs=[
            pl.BlockSpec((1, qt, n * LANES, tile), lambda b, hp, i: (b, i, hp, 0)),
            pl.BlockSpec((1, seq, n * LANES), lambda b, hp, i: (b, 0, hp)),
            pl.BlockSpec((1, tiles, n * LANES, tile), lambda b, hp, i: (b, 0, hp, 0)),
        ],
        out_specs=pl.BlockSpec((1, qt * tile, n * B_V), lambda b, hp, i: (b, i, hp)),
        out_shape=jax.ShapeDtypeStruct((batch, seq, B_HEADS * B_V), jnp.bfloat16),
        scratch_shapes=([pltpu.VMEM((tile, tile), jnp.float32)] * (2 * n * qt)
                        + [pltpu.VMEM((tile, tile), jnp.bfloat16)] * (2 * n * qt)),
        compiler_params=pltpu.CompilerParams(dimension_semantics=("parallel", "parallel", "parallel"),
                                             vmem_limit_bytes=VMEM_LIMIT),
        name="mla_attn",
    )(qbt, kb, vbt)


def _mix_out_kernel(x_ref, oa_ref, ob_ref, gate_ref,
                    woa_ref, wob_ref, wout_ref, gpost_ref, gpre_ref, h_o, hn_o, *, d_model):
    o_a = jnp.dot(oa_ref[...], woa_ref[...], preferred_element_type=jnp.float32)
    o_b = jnp.dot(ob_ref[...], wob_ref[...], preferred_element_type=jnp.float32)
    gates = gate_ref[...].astype(jnp.float32)
    mixin = (gates[:, :d_model] * o_a + gates[:, d_model:] * o_b).astype(jnp.bfloat16)
    mix = jnp.dot(mixin, wout_ref[...], preferred_element_type=jnp.float32)
    h = x_ref[...] + _rms(mix, gpost_ref[...])
    h_o[...] = h
    hn_o[...] = _rms(h, gpre_ref[...]).astype(hn_o.dtype)


def _mix_out(x2, oa, ob, gates, woa, wob, wout, gpost, gpre, *, tm):
    t, d_model = x2.shape
    row = lambda i: (i, 0)
    const = lambda i: (0, 0)
    return pl.pallas_call(
        functools.partial(_mix_out_kernel, d_model=d_model),
        grid=(t // tm,),
        in_specs=[
            pl.BlockSpec((tm, d_model), row),
            pl.BlockSpec((tm, A_GROUP_WIDTH), row),
            pl.BlockSpec((tm, B_HEADS * B_V), row),
            pl.BlockSpec((tm, 2 * d_model), row),
            pl.BlockSpec((A_GROUP_WIDTH, d_model), const),
            pl.BlockSpec((B_HEADS * B_V, d_model), const),
            pl.BlockSpec((d_model, d_model), const),
            pl.BlockSpec((1, d_model), const),
            pl.BlockSpec((1, d_model), const),
        ],
        out_specs=(pl.BlockSpec((tm, d_model), row), pl.BlockSpec((tm, d_model), row)),
        out_shape=(jax.ShapeDtypeStruct((t, d_model), jnp.float32),
                   jax.ShapeDtypeStruct((t, d_model), jnp.bfloat16)),
        compiler_params=pltpu.CompilerParams(dimension_semantics=("parallel",), vmem_limit_bytes=VMEM_LIMIT),
        name="mix_out",
    )(x2, oa, ob, gates, woa, wob, wout, gpost, gpre)


def _ffn_kernel(h_ref, hn_ref, w1_ref, w2_ref, g_ref, out_o):
    u = jnp.dot(hn_ref[...], w1_ref[...], preferred_element_type=jnp.float32)
    u = jnp.square(jnp.maximum(u, 0.0)).astype(jnp.bfloat16)
    ff = jnp.dot(u, w2_ref[...], preferred_element_type=jnp.float32)
    out_o[...] = h_ref[...] + _rms(ff, g_ref[...])


def _ffn(h, hn, w1, w2, g, *, tm):
    t, d_model = h.shape
    d_ff = w1.shape[1]
    row = lambda i: (i, 0)
    const = lambda i: (0, 0)
    return pl.pallas_call(
        _ffn_kernel,
        grid=(t // tm,),
        in_specs=[
            pl.BlockSpec((tm, d_model), row), pl.BlockSpec((tm, d_model), row),
            pl.BlockSpec((d_model, d_ff), const), pl.BlockSpec((d_ff, d_model), const),
            pl.BlockSpec((1, d_model), const),
        ],
        out_specs=pl.BlockSpec((tm, d_model), row),
        out_shape=jax.ShapeDtypeStruct((t, d_model), jnp.float32),
        compiler_params=pltpu.CompilerParams(dimension_semantics=("parallel",), vmem_limit_bytes=VMEM_LIMIT),
        name="ffn",
    )(h, hn, w1, w2, g)


def _rotary_tables(seq, theta, rot_dim, period, offset):
    half = rot_dim // 2
    inv = jnp.float32(theta) ** (-(jnp.arange(half, dtype=jnp.float32) * 2.0 / rot_dim))
    ang = jnp.arange(seq, dtype=jnp.int32).astype(jnp.float32)[:, None] * inv[None, :]
    cos, sin = jnp.cos(ang), jnp.sin(ang)
    cos_p = jnp.ones((seq, period), jnp.float32).at[:, offset:offset + rot_dim].set(jnp.concatenate([cos, cos], 1))
    sin_p = jnp.zeros((seq, period), jnp.float32).at[:, offset:offset + rot_dim].set(jnp.concatenate([-sin, sin], 1))
    reps = LANES // period
    return jnp.tile(cos_p, (1, reps)), jnp.tile(sin_p, (1, reps))


def _pad_heads(w, heads, width):
    rows = w.shape[0]
    w = w.reshape(rows, heads, width)
    return jnp.pad(w, ((0, 0), (0, 0), (0, LANES - width))).reshape(rows, heads * LANES)


def kernel(x, norm_mix_pre, w_in, b_gate, mla_q_norm, mla_w_uq, mla_kv_norm, mla_w_ukv, w_o_a, w_o_b, w_out,
           norm_mix_post, norm_mlp_pre, w_ff1, w_ff2, norm_mlp_post):
    batch, seq, d_model = x.shape
    bf = jnp.bfloat16
    x2 = x.reshape(batch * seq, d_model)

    in_a = 3 * A_WIDTH
    c_kpe = in_a + B_Q_RANK + B_KV_RANK
    w_in = w_in.astype(bf)
    kpe_cols = jnp.pad(w_in[:, c_kpe:c_kpe + B_ROPE], ((0, 0), (B_NOPE, LANES - B_QK)))
    w1 = jnp.concatenate([w_in[:, :c_kpe], kpe_cols, w_in[:, c_kpe + B_ROPE:]], axis=1)
    wuqt = _pad_heads(mla_w_uq, B_HEADS, B_QK).T.astype(bf)
    wukv = mla_w_ukv.reshape(B_KV_RANK, B_HEADS, B_NOPE + B_V)
    wuk = _pad_heads(wukv[:, :, :B_NOPE].reshape(B_KV_RANK, B_HEADS * B_NOPE), B_HEADS, B_NOPE).astype(bf)
    wuvt = _pad_heads(wukv[:, :, B_NOPE:].reshape(B_KV_RANK, B_HEADS * B_V), B_HEADS, B_V).T.astype(bf)

    ca, sa = _rotary_tables(seq, A_ROPE_THETA, A_ROT_DIM, A_HEAD_DIM, 0)
    cb, sb = _rotary_tables(seq, B_ROPE_THETA, B_ROPE, LANES, B_NOPE)

    vec = lambda v: v.reshape(1, -1)
    qa, ka, va, qbt, kb, vbt, gates = _in_proj(
        x2, vec(norm_mix_pre), w1, vec(b_gate), vec(mla_q_norm), wuqt, vec(mla_kv_norm), wuk, wuvt,
        ca, sa, cb, sb, cb.T, sb.T, seq=seq, tm=512)

    shape_a = (batch, seq, A_WIDTH)
    oa = _dilated_attention(qa.reshape(shape_a), ka.reshape(shape_a), va.reshape(shape_a))
    oa = oa.reshape(batch * seq, A_GROUP_WIDTH)

    ob = _mla_attention(qbt, kb.reshape(batch, seq, B_WIDTH), vbt)
    ob = ob.reshape(batch * seq, B_HEADS * B_V)

    h, hn = _mix_out(x2, oa, ob, gates, w_o_a.astype(bf), w_o_b.astype(bf), w_out.astype(bf),
                     vec(norm_mix_post), vec(norm_mlp_pre), tm=512)
    out = _ffn(h, hn, w_ff1.astype(bf), w_ff2.astype(bf), vec(norm_mlp_post), tm=512)
    return out.reshape(batch, seq, d_model)
```

```python
import functools

import jax
import jax.numpy as jnp
from jax import lax
from jax.experimental import pallas as pl
from jax.experimental.pallas import tpu as pltpu

EPS = 1e-6
LANES = 128

A_HEAD_DIM = 64
A_HEADS_PER_GROUP = 4
A_GROUPS = ((128, 1), (512, 4), (2048, 16))
A_HEADS = A_HEADS_PER_GROUP * len(A_GROUPS)
A_WIDTH = A_HEADS * A_HEAD_DIM
A_GROUP_WIDTH = A_HEADS_PER_GROUP * A_HEAD_DIM
A_ROT_DIM = A_HEAD_DIM // 4
A_ROPE_THETA = 500000.0
A_TILE = 128
A_TILES_PER_ITER = 4
A_GROUP_ORDER = (2, 0, 1)
B_HEADS = 8
B_NOPE = 64
B_ROPE = 32
B_QK = B_NOPE + B_ROPE
B_V = 64
B_Q_RANK = 512
B_KV_RANK = 256
B_ROPE_THETA = 10000.0
B_WIDTH = B_HEADS * LANES
MLA_KEY_BLOCK = 256
MLA_PAIRS_PER_ITER = 1
MLA_HEADS_PER_STEP = 2
MLA_Q_TILES_PER_STEP = 1
MLA_SCORE_STREAMS_RUN = 2
MLA_VT_ROWS = B_V + 16

NEG_BIG = -1e30
LOG2_E = 1.4426950408889634
VMEM_LIMIT = 56 * 1024 * 1024


def _rms(x, g):
    return x * lax.rsqrt(jnp.mean(x * x, axis=-1, keepdims=True) + EPS) * g


def _rotate_block(xb, cos, sin_signed, first_half, half):
    up = pltpu.roll(xb, LANES - half, 1)
    down = pltpu.roll(xb, half, 1)
    return xb * cos + jnp.where(first_half, up, down) * sin_signed


_C_QA, _C_KA, _C_VA = 0, A_WIDTH, 2 * A_WIDTH
_C_QC = 3 * A_WIDTH
_C_KVC = _C_QC + B_Q_RANK
_C_KPE = _C_KVC + B_KV_RANK
_C_GATE = _C_KPE + LANES


_NT_DIMS = (((1,), (1,)), ((), ()))


def _in_proj_kernel(x_ref, g_ref, w_ref, bg_ref, qn_ref, wuqt_ref, kvn_ref, wuk_ref, wuvt_ref,
                    ca_ref, sa_ref, cb_ref, sb_ref, cbt_ref, sbt_ref,
                    qa_o, ka_o, va_o, qbt_o, kb_o, vbt_o, gate_o, *, d_model):
    xn = _rms(x_ref[...], g_ref[...]).astype(jnp.bfloat16)

    def proj(lo, hi):
        return jnp.dot(xn, w_ref[:, lo:hi], preferred_element_type=jnp.float32)

    lane = lax.broadcasted_iota(jnp.int32, (1, LANES), 1)
    first_a = (lane % A_HEAD_DIM) < (A_ROT_DIM // 2)
    first_b = lane < (B_NOPE + B_ROPE // 2)
    ca, sa, cb, sb = ca_ref[...], sa_ref[...], cb_ref[...], sb_ref[...]

    qcn = _rms(proj(_C_QC, _C_QC + B_Q_RANK), qn_ref[...]).astype(jnp.bfloat16)
    kvn = _rms(proj(_C_KVC, _C_KVC + B_KV_RANK), kvn_ref[...]).astype(jnp.bfloat16)
    kpe = _rotate_block(proj(_C_KPE, _C_KPE + LANES), cb, sb, first_b, B_ROPE // 2)

    z = proj(_C_GATE, _C_GATE + 2 * d_model) + bg_ref[...]
    gate_o[...] = (0.5 * jnp.tanh(0.5 * z) + 0.5).astype(gate_o.dtype)

    qa = proj(_C_QA, _C_QA + A_WIDTH)
    ka = proj(_C_KA, _C_KA + A_WIDTH)
    a_scale = A_HEAD_DIM ** -0.5 * LOG2_E
    for j in range(A_WIDTH // LANES):
        blk = slice(j * LANES, (j + 1) * LANES)
        qa_o[:, blk] = (_rotate_block(qa[:, blk], ca, sa, first_a, A_ROT_DIM // 2) * a_scale).astype(qa_o.dtype)
        ka_o[:, blk] = _rotate_block(ka[:, blk], ca, sa, first_a, A_ROT_DIM // 2).astype(ka_o.dtype)
    va_o[...] = proj(_C_VA, _C_VA + A_WIDTH).astype(va_o.dtype)

    qbt = lax.dot_general(wuqt_ref[...], qcn, _NT_DIMS, preferred_element_type=jnp.float32)
    vbt = lax.dot_general(wuvt_ref[...], kvn, _NT_DIMS, preferred_element_type=jnp.float32)
    kb = jnp.dot(kvn, wuk_ref[...], preferred_element_type=jnp.float32)
    b_scale = B_QK ** -0.5 * LOG2_E
    cbt, sbt = cbt_ref[...], sbt_ref[...]
    row = lax.broadcasted_iota(jnp.int32, (LANES, 1), 0)
    r1, r2 = B_NOPE + B_ROPE // 2, B_QK
    for h in range(B_HEADS):
        blk = slice(h * LANES, (h + 1) * LANES)
        xt = qbt[blk, :]
        partner = jnp.concatenate([xt[:B_NOPE], xt[r1:r2], xt[B_NOPE:r1], xt[r2:]], axis=0)
        qbt_o[0, 0, blk, :] = ((xt * cbt + partner * sbt) * b_scale).astype(qbt_o.dtype)
        kb_o[:, blk] = (kb[:, blk] + kpe).astype(kb_o.dtype)
        vbt_o[0, 0, blk, :] = jnp.where(row < B_V, vbt[blk, :], 1.0).astype(vbt_o.dtype)


def _in_proj(x2, g, w1, bg, qn, wuqt, kvn, wuk, wuvt, ca, sa, cb, sb, cbt, sbt, *, seq, tm):
    t, d_model = x2.shape
    n1 = w1.shape[1]
    steps_per_seq = seq // tm
    row = lambda i: (i, 0)
    const = lambda i: (0, 0)
    pos = lambda i: (i % steps_per_seq, 0)
    pos_t = lambda i: (0, i % steps_per_seq)
    tile_t = lambda i: (i // steps_per_seq, i % steps_per_seq, 0, 0)
    bf = jnp.bfloat16
    transposed = jax.ShapeDtypeStruct((t // seq, steps_per_seq, B_WIDTH, tm), bf)
    out_shape = (
        jax.ShapeDtypeStruct((t, A_WIDTH), bf), jax.ShapeDtypeStruct((t, A_WIDTH), bf),
        jax.ShapeDtypeStruct((t, A_WIDTH), bf),
        transposed, jax.ShapeDtypeStruct((t, B_WIDTH), bf), transposed,
        jax.ShapeDtypeStruct((t, 2 * d_model), bf),
    )
    return pl.pallas_call(
        functools.partial(_in_proj_kernel, d_model=d_model),
        grid=(t // tm,),
        in_specs=[
            pl.BlockSpec((tm, d_model), row),
            pl.BlockSpec((1, d_model), const),
            pl.BlockSpec((d_model, n1), const),
            pl.BlockSpec((1, 2 * d_model), const),
            pl.BlockSpec((1, B_Q_RANK), const),
            pl.BlockSpec((B_WIDTH, B_Q_RANK), const),
            pl.BlockSpec((1, B_KV_RANK), const),
            pl.BlockSpec((B_KV_RANK, B_WIDTH), const),
            pl.BlockSpec((B_WIDTH, B_KV_RANK), const),
            pl.BlockSpec((tm, LANES), pos), pl.BlockSpec((tm, LANES), pos),
            pl.BlockSpec((tm, LANES), pos), pl.BlockSpec((tm, LANES), pos),
            pl.BlockSpec((LANES, tm), pos_t), pl.BlockSpec((LANES, tm), pos_t),
        ],
        out_specs=(
            pl.BlockSpec((tm, A_WIDTH), row), pl.BlockSpec((tm, A_WIDTH), row),
            pl.BlockSpec((tm, A_WIDTH), row),
            pl.BlockSpec((1, 1, B_WIDTH, tm), tile_t), pl.BlockSpec((tm, B_WIDTH), row),
            pl.BlockSpec((1, 1, B_WIDTH, tm), tile_t),
            pl.BlockSpec((tm, 2 * d_model), row),
        ),
        out_shape=out_shape,
        compiler_params=pltpu.CompilerParams(dimension_semantics=("parallel",), vmem_limit_bytes=VMEM_LIMIT),
        name="in_proj",
    )(x2, g, w1, bg, qn, wuqt, kvn, wuk, wuvt, ca, sa, cb, sb, cbt, sbt)


def _dilated_group(qf, kf, vf, m_sc, l_sc, acc_sc, *, seq, window, dilation, first):
    length = seq // dilation
    radius = window // (2 * dilation)
    win = A_TILE + 2 * radius
    tiles = length // A_TILE
    assert length % A_TILE == 0 and tiles % A_TILES_PER_ITER == 0 and win <= length
    iters_per_residue = tiles // A_TILES_PER_ITER
    lane = lax.broadcasted_iota(jnp.int32, (1, LANES), 1)
    head0 = lane < A_HEAD_DIM
    delta = (lax.broadcasted_iota(jnp.int32, (2 * A_TILE, win), 0) % A_TILE
             - lax.broadcasted_iota(jnp.int32, (2 * A_TILE, win), 1))

    def strided(start, size):
        return pl.ds(start, size) if dilation == 1 else pl.ds(start, size, stride=dilation)

    def body(n, carry):
        r = n // iters_per_residue
        i0 = (n % iters_per_residue) * A_TILES_PER_ITER
        l0s = [(i0 + t) * A_TILE for t in range(A_TILES_PER_ITER)]
        wss = [jnp.clip(l0 - radius, 0, length - win) for l0 in l0s]
        ss = []
        for l0, ws in zip(l0s, wss):
            q = qf[strided(r + dilation * l0, A_TILE), :]
            k = kf[strided(r + dilation * ws, win), :].astype(jnp.bfloat16)
            q2 = jnp.concatenate([jnp.where(head0, q, 0.0), jnp.where(head0, 0.0, q)], axis=0)
            ss.append(lax.dot_general(q2.astype(jnp.bfloat16), k, _NT_DIMS, preferred_element_type=jnp.float32))
        stats = []
        for l0, ws, s in zip(l0s, wss, ss):
            dist = delta + (l0 - ws)
            s = jnp.where((dist <= radius) & (dist >= -radius), s, NEG_BIG)
            m = jnp.max(s, axis=-1, keepdims=True)
            p = jnp.exp2(s - m)
            stats.append((m, jnp.sum(p, axis=-1, keepdims=True), p.astype(jnp.bfloat16)))
        for l0, ws, (m, l, p) in zip(l0s, wss, stats):
            v = vf[strided(r + dilation * ws, win), :].astype(jnp.bfloat16)
            pv = jnp.dot(p, v, preferred_element_type=jnp.float32)
            acc_t = jnp.where(head0, pv[:A_TILE], pv[A_TILE:])
            m_t = jnp.where(head0, m[:A_TILE], m[A_TILE:])
            l_t = jnp.where(head0, l[:A_TILE], l[A_TILE:])
            rows = strided(r + dilation * l0, A_TILE)
            if first:
                m_sc[rows, :] = m_t
                l_sc[rows, :] = l_t
                acc_sc[rows, :] = acc_t
            else:
                m_old = m_sc[rows, :]
                m_new = jnp.maximum(m_old, m_t)
                a = jnp.exp2(m_old - m_new)
                b = jnp.exp2(m_t - m_new)
                m_sc[rows, :] = m_new
                l_sc[rows, :] = a * l_sc[rows, :] + b * l_t
                acc_sc[rows, :] = a * acc_sc[rows, :] + b * acc_t
        return carry

    lax.fori_loop(0, dilation * iters_per_residue, body, 0)


def _dilated_kernel(q_ref, k_ref, v_ref, o_ref, qf, kf, vf, m_sc, l_sc, acc_sc, *, seq):
    turn = pl.program_id(2)
    chunk = 512

    def widen(i, carry):
        rows = pl.ds(pl.multiple_of(i * chunk, chunk), chunk)
        qf[rows, :] = q_ref[0, rows, :].astype(jnp.float32)
        kf[rows, :] = k_ref[0, rows, :].astype(jnp.float32)
        vf[rows, :] = v_ref[0, rows, :].astype(jnp.float32)
        return carry

    lax.fori_loop(0, seq // chunk, widen, 0)

    for t, gi in enumerate(A_GROUP_ORDER):
        window, dilation = A_GROUPS[gi]

        @pl.when(turn == t)
        def _(window=window, dilation=dilation, t=t):
            _dilated_group(qf, kf, vf, m_sc, l_sc, acc_sc, seq=seq, window=window, dilation=dilation,
                           first=t == 0)

    @pl.when(turn == len(A_GROUPS) - 1)
    def _():
        def finish(i, carry):
            rows = pl.ds(pl.multiple_of(i * chunk, chunk), chunk)
            o_ref[0, rows, :] = (acc_sc[rows, :] / l_sc[rows, :]).astype(o_ref.dtype)
            return carry

        lax.fori_loop(0, seq // chunk, finish, 0)


def _dilated_attention(qa, ka, va):
    batch, seq, _ = qa.shape
    pairs = A_GROUP_WIDTH // LANES
    first, n = A_GROUP_ORDER[0], len(A_GROUPS)
    assert A_GROUP_ORDER == tuple((first + t) % n for t in range(n))
    in_spec = pl.BlockSpec((1, seq, LANES), lambda b, hp, t: (b, 0, ((first + t) % n) * pairs + hp))
    state = pltpu.VMEM((seq, LANES), jnp.float32)
    return pl.pallas_call(
        functools.partial(_dilated_kernel, seq=seq),
        grid=(batch, pairs, len(A_GROUPS)),
        in_specs=[in_spec, in_spec, in_spec],
        out_specs=pl.BlockSpec((1, seq, LANES), lambda b, hp, g: (b, 0, hp)),
        out_shape=jax.ShapeDtypeStruct((batch, seq, A_GROUP_WIDTH), jnp.bfloat16),
        scratch_shapes=[state] * 6,
        compiler_params=pltpu.CompilerParams(dimension_semantics=("parallel", "parallel", "arbitrary"),
                                             vmem_limit_bytes=VMEM_LIMIT),
        name="dilated",
    )(qa, ka, va)


def _mla_kernel(qt_ref, k_ref, vt_ref, o_ref, *scratch, n_groups):
    tq = qt_ref.shape[3]
    tk = vt_ref.shape[3]
    n_streams = MLA_HEADS_PER_STEP * MLA_Q_TILES_PER_STEP
    sa_ref, sb_ref, pa_ref, pb_ref = (scratch[i * n_streams:(i + 1) * n_streams] for i in range(4))
    key_blocks = tk // MLA_KEY_BLOCK
    lane = lax.broadcasted_iota(jnp.int32, (1, LANES), 1)
    heads = range(n_streams)
    blks = [slice((st % MLA_HEADS_PER_STEP) * LANES, (st % MLA_HEADS_PER_STEP + 1) * LANES) for st in heads]
    qts = [qt_ref[0, st // MLA_HEADS_PER_STEP, blks[st], :] for st in heads]

    def rows(c):
        return slice(c * MLA_KEY_BLOCK, (c + 1) * MLA_KEY_BLOCK)

    def qk_block(h, g, c, s_ref):
        base = g * tk if isinstance(g, int) else pl.multiple_of(g * tk, tk)
        k = k_ref[0, pl.ds(base + c * MLA_KEY_BLOCK, MLA_KEY_BLOCK), blks[h]]
        s = jnp.dot(k, qts[h], preferred_element_type=jnp.float32)
        s_ref[h][rows(c), :] = s
        return jnp.max(s, axis=0, keepdims=True)

    def pv(h, g, p_ref):
        vt = vt_ref[0, g, blks[h].start:blks[h].start + MLA_VT_ROWS, :]
        return jnp.dot(vt, p_ref[h][...], preferred_element_type=jnp.float32)

    def matmuls(g_qk, s_ref, g_pv, p_ref):
        tops, pvs = [None for _ in heads], [None for _ in heads]

        def qk(h):
            for c in range(key_blocks if g_qk is not None else 0):
                top = qk_block(h, g_qk, c, s_ref)
                tops[h] = top if tops[h] is None else jnp.maximum(tops[h], top)

        for h in heads[:MLA_SCORE_STREAMS_RUN]:
            qk(h)
        if g_pv is not None:
            pvs = [pv(h, g_pv, p_ref) for h in heads]
        for h in heads[MLA_SCORE_STREAMS_RUN:]:
            qk(h)
        return tops, pvs

    def softmax(h, m, top, s_ref, p_ref):
        m_new = jnp.maximum(m, top)
        for c in range(key_blocks):
            p_ref[h][rows(c), :] = jnp.exp2((s_ref[h][rows(c), :] - m_new).astype(p_ref[h].dtype))
        return m_new, jnp.exp2(m - m_new)

    def step(g, carry, s_read, s_write, p_read, p_write, first=False, last=False):
        tops, pvs = matmuls(None if last else g + 1, s_write, None if first else g - 1, p_read)
        new = []
        for h in heads:
            m, acc, alpha, top = carry[h]
            m_new, alpha_new = softmax(h, m, top, s_read, p_write)
            new.append((m_new, acc if first else alpha * acc + pvs[h], alpha_new, tops[h]))
        return tuple(new)

    def even(g, carry, **kw):
        return step(g, carry, sa_ref, sb_ref, pb_ref, pa_ref, **kw)

    def odd(g, carry, **kw):
        return step(g, carry, sb_ref, sa_ref, pa_ref, pb_ref, **kw)

    tops, _ = matmuls(0, sa_ref, None, None)
    init = tuple((jnp.full((1, tq), NEG_BIG, jnp.float32), jnp.zeros((MLA_VT_ROWS, tq), jnp.float32),
                  jnp.ones((1, tq), jnp.float32), tops[h]) for h in heads)
    carry = even(0, init, first=True)

    def pairs(g, carry, count):
        for i in range(count):
            carry = even(g + 2 * i + 1, odd(g + 2 * i, carry))
        return carry

    n_pairs = (n_groups - 2) // 2
    trips = n_pairs // MLA_PAIRS_PER_ITER
    carry = lax.fori_loop(
        0, trips, lambda t, c: pairs(2 * MLA_PAIRS_PER_ITER * t + 1, c, MLA_PAIRS_PER_ITER), carry)
    carry = pairs(2 * MLA_PAIRS_PER_ITER * trips + 1, carry, n_pairs - trips * MLA_PAIRS_PER_ITER)
    carry = odd(n_groups - 1, carry, last=True)
    _, pvs = matmuls(None, None, n_groups - 1, pb_ref)
    accs = [alpha * acc + pvs[h] for h, (_, acc, alpha, _) in enumerate(carry)]
    outs = [a[:B_V] / a[B_V:B_V + 1] for a in accs]
    for t in range(MLA_Q_TILES_PER_STEP):
        tile_outs = outs[t * MLA_HEADS_PER_STEP:(t + 1) * MLA_HEADS_PER_STEP]
        o_ref[0, t * tq:(t + 1) * tq, :] = jnp.concatenate(tile_outs, axis=0).T.astype(o_ref.dtype)


def _mla_attention(qbt, kb, vbt):
    batch, tiles, _, tile = qbt.shape
    seq = kb.shape[1]
    n, qt = MLA_HEADS_PER_STEP, MLA_Q_TILES_PER_STEP
    return pl.pallas_call(
        functools.partial(_mla_kernel, n_groups=tiles),
        grid=(batch, B_HEADS // n, tiles // qt),
        in_specs=[
            pl.BlockSpec((1, qt, n * LANES, tile), lambda b, hp, i: (b, i, hp, 0)),
            pl.BlockSpec((1, seq, n * LANES), lambda b, hp, i: (b, 0, hp)),
            pl.BlockSpec((1, tiles, n * LANES, tile), lambda b, hp, i: (b, 0, hp, 0)),
        ],
        out_specs=pl.BlockSpec((1, qt * tile, n * B_V), lambda b, hp, i: (b, i, hp)),
        out_shape=jax.ShapeDtypeStruct((batch, seq, B_HEADS * B_V), jnp.bfloat16),
        scratch_shapes=([pltpu.VMEM((tile, tile), jnp.float32)] * (2 * n * qt)
                        + [pltpu.VMEM((tile, tile), jnp.bfloat16)] * (2 * n * qt)),
        compiler_params=pltpu.CompilerParams(dimension_semantics=("parallel", "parallel", "parallel"),
                                             vmem_limit_bytes=VMEM_LIMIT),
        name="mla_attn",
    )(qbt, kb, vbt)


def _mix_ffn_kernel(x_ref, oa_ref, ob_ref, gate_ref, woa_ref, wob_ref, wout_ref, gpost_ref, gpre_ref,
                    w1_ref, w2_ref, gmlp_ref, out_o, *, d_model):
    o_a = jnp.dot(oa_ref[...], woa_ref[...], preferred_element_type=jnp.float32)
    o_b = jnp.dot(ob_ref[...], wob_ref[...], preferred_element_type=jnp.float32)
    gates = gate_ref[...].astype(jnp.float32)
    mixin = (gates[:, :d_model] * o_a + gates[:, d_model:] * o_b).astype(jnp.bfloat16)
    mix = jnp.dot(mixin, wout_ref[...], preferred_element_type=jnp.float32)
    h = x_ref[...] + _rms(mix, gpost_ref[...])
    hn = _rms(h, gpre_ref[...]).astype(jnp.bfloat16)
    u = jnp.dot(hn, w1_ref[...], preferred_element_type=jnp.float32)
    u = jnp.square(jnp.maximum(u, 0.0)).astype(jnp.bfloat16)
    ff = jnp.dot(u, w2_ref[...], preferred_element_type=jnp.float32)
    out_o[...] = h + _rms(ff, gmlp_ref[...])


def _mix_ffn(x2, oa, ob, gates, woa, wob, wout, gpost, gpre, w1, w2, gmlp, *, tm):
    t, d_model = x2.shape
    d_ff = w1.shape[1]
    row = lambda i: (i, 0)
    whole = lambda shape: pl.BlockSpec(shape, lambda i: (0, 0), pipeline_mode=pl.Buffered(1))
    return pl.pallas_call(
        functools.partial(_mix_ffn_kernel, d_model=d_model),
        grid=(t // tm,),
        in_specs=[
            pl.BlockSpec((tm, d_model), row),
            pl.BlockSpec((tm, A_GROUP_WIDTH), row),
            pl.BlockSpec((tm, B_HEADS * B_V), row),
            pl.BlockSpec((tm, 2 * d_model), row),
            whole((A_GROUP_WIDTH, d_model)),
            whole((B_HEADS * B_V, d_model)),
            whole((d_model, d_model)),
            whole((1, d_model)),
            whole((1, d_model)),
            whole((d_model, d_ff)),
            whole((d_ff, d_model)),
            whole((1, d_model)),
        ],
        out_specs=pl.BlockSpec((tm, d_model), row),
        out_shape=jax.ShapeDtypeStruct((t, d_model), jnp.float32),
        compiler_params=pltpu.CompilerParams(dimension_semantics=("parallel",), vmem_limit_bytes=VMEM_LIMIT),
        name="mix_ffn",
    )(x2, oa, ob, gates, woa, wob, wout, gpost, gpre, w1, w2, gmlp)


def _rotary_tables(seq, theta, rot_dim, period, offset):
    half = rot_dim // 2
    inv = jnp.float32(theta) ** (-(jnp.arange(half, dtype=jnp.float32) * 2.0 / rot_dim))
    ang = jnp.arange(seq, dtype=jnp.int32).astype(jnp.float32)[:, None] * inv[None, :]
    cos, sin = jnp.cos(ang), jnp.sin(ang)
    cos_p = jnp.ones((seq, period), jnp.float32).at[:, offset:offset + rot_dim].set(jnp.concatenate([cos, cos], 1))
    sin_p = jnp.zeros((seq, period), jnp.float32).at[:, offset:offset + rot_dim].set(jnp.concatenate([-sin, sin], 1))
    reps = LANES // period
    return jnp.tile(cos_p, (1, reps)), jnp.tile(sin_p, (1, reps))


def _pad_heads(w, heads, width):
    rows = w.shape[0]
    w = w.reshape(rows, heads, width)
    return jnp.pad(w, ((0, 0), (0, 0), (0, LANES - width))).reshape(rows, heads * LANES)


def kernel(x, norm_mix_pre, w_in, b_gate, mla_q_norm, mla_w_uq, mla_kv_norm, mla_w_ukv, w_o_a, w_o_b, w_out,
           norm_mix_post, norm_mlp_pre, w_ff1, w_ff2, norm_mlp_post):
    batch, seq, d_model = x.shape
    bf = jnp.bfloat16
    x2 = x.reshape(batch * seq, d_model)

    in_a = 3 * A_WIDTH
    c_kpe = in_a + B_Q_RANK + B_KV_RANK
    w_in = w_in.astype(bf)
    kpe_cols = jnp.pad(w_in[:, c_kpe:c_kpe + B_ROPE], ((0, 0), (B_NOPE, LANES - B_QK)))
    w1 = jnp.concatenate([w_in[:, :c_kpe], kpe_cols, w_in[:, c_kpe + B_ROPE:]], axis=1)
    wuqt = _pad_heads(mla_w_uq, B_HEADS, B_QK).T.astype(bf)
    wukv = mla_w_ukv.reshape(B_KV_RANK, B_HEADS, B_NOPE + B_V)
    wuk = _pad_heads(wukv[:, :, :B_NOPE].reshape(B_KV_RANK, B_HEADS * B_NOPE), B_HEADS, B_NOPE).astype(bf)
    wuvt = _pad_heads(wukv[:, :, B_NOPE:].reshape(B_KV_RANK, B_HEADS * B_V), B_HEADS, B_V).T.astype(bf)

    ca, sa = _rotary_tables(seq, A_ROPE_THETA, A_ROT_DIM, A_HEAD_DIM, 0)
    cb, sb = _rotary_tables(seq, B_ROPE_THETA, B_ROPE, LANES, B_NOPE)

    vec = lambda v: v.reshape(1, -1)
    qa, ka, va, qbt, kb, vbt, gates = _in_proj(
        x2, vec(norm_mix_pre), w1, vec(b_gate), vec(mla_q_norm), wuqt, vec(mla_kv_norm), wuk, wuvt,
        ca, sa, cb, sb, cb.T, sb.T, seq=seq, tm=512)

    shape_a = (batch, seq, A_WIDTH)
    oa = _dilated_attention(qa.reshape(shape_a), ka.reshape(shape_a), va.reshape(shape_a))
    oa = oa.reshape(batch * seq, A_GROUP_WIDTH)

    ob = _mla_attention(qbt, kb.reshape(batch, seq, B_WIDTH), vbt)
    ob = ob.reshape(batch * seq, B_HEADS * B_V)

    out = _mix_ffn(x2, oa, ob, gates, w_o_a.astype(bf), w_o_b.astype(bf), w_out.astype(bf),
                   vec(norm_mix_post), vec(norm_mlp_pre), w_ff1.astype(bf), w_ff2.astype(bf),
                   vec(norm_mlp_post), tm=512)
    return out.reshape(batch, seq, d_model)
```

```python
import functools

import jax
import jax.numpy as jnp
from jax import lax
from jax.experimental import pallas as pl
from jax.experimental.pallas import tpu as pltpu

EPS = 1e-6
LANES = 128

A_HEAD_DIM = 64
A_HEADS_PER_GROUP = 4
A_GROUPS = ((128, 1), (512, 4), (2048, 16))
A_HEADS = A_HEADS_PER_GROUP * len(A_GROUPS)
A_WIDTH = A_HEADS * A_HEAD_DIM
A_GROUP_WIDTH = A_HEADS_PER_GROUP * A_HEAD_DIM
A_ROT_DIM = A_HEAD_DIM // 4
A_ROPE_THETA = 500000.0
A_TILE = 128
A_TILES_PER_ITER = 4
A_GROUP_ORDER = (2, 0, 1)
B_HEADS = 8
B_NOPE = 64
B_ROPE = 32
B_QK = B_NOPE + B_ROPE
B_V = 64
B_Q_RANK = 512
B_KV_RANK = 256
B_ROPE_THETA = 10000.0
B_WIDTH = B_HEADS * LANES
MLA_KEY_BLOCK = 256
MLA_PAIRS_PER_ITER = 1
MLA_HEADS_PER_STEP = 2
MLA_Q_TILES_PER_STEP = 2
MLA_SCORE_STREAMS_RUN = 2
MLA_VT_ROWS = B_V + 16

NEG_BIG = -1e30
LOG2_E = 1.4426950408889634
VMEM_LIMIT = 56 * 1024 * 1024


def _rms(x, g):
    return x * lax.rsqrt(jnp.mean(x * x, axis=-1, keepdims=True) + EPS) * g


def _rotate_block(xb, cos, sin_signed, first_half, half):
    up = pltpu.roll(xb, LANES - half, 1)
    down = pltpu.roll(xb, half, 1)
    return xb * cos + jnp.where(first_half, up, down) * sin_signed


_C_QA, _C_KA, _C_VA = 0, A_WIDTH, 2 * A_WIDTH
_C_QC = 3 * A_WIDTH
_C_KVC = _C_QC + B_Q_RANK
_C_KPE = _C_KVC + B_KV_RANK
_C_GATE = _C_KPE + LANES


_NT_DIMS = (((1,), (1,)), ((), ()))


def _in_proj_kernel(x_ref, g_ref, w_ref, bg_ref, qn_ref, wuqt_ref, kvn_ref, wuk_ref, wuvt_ref,
                    ca_ref, sa_ref, cb_ref, sb_ref, cbt_ref, sbt_ref,
                    qa_o, ka_o, va_o, qbt_o, kb_o, vbt_o, gate_o, *, d_model):
    xn = _rms(x_ref[...], g_ref[...]).astype(jnp.bfloat16)

    def proj(lo, hi):
        return jnp.dot(xn, w_ref[:, lo:hi], preferred_element_type=jnp.float32)

    lane = lax.broadcasted_iota(jnp.int32, (1, LANES), 1)
    first_a = (lane % A_HEAD_DIM) < (A_ROT_DIM // 2)
    first_b = lane < (B_NOPE + B_ROPE // 2)
    ca, sa, cb, sb = ca_ref[...], sa_ref[...], cb_ref[...], sb_ref[...]

    qcn = _rms(proj(_C_QC, _C_QC + B_Q_RANK), qn_ref[...]).astype(jnp.bfloat16)
    kvn = _rms(proj(_C_KVC, _C_KVC + B_KV_RANK), kvn_ref[...]).astype(jnp.bfloat16)
    kpe = _rotate_block(proj(_C_KPE, _C_KPE + LANES), cb, sb, first_b, B_ROPE // 2)

    z = proj(_C_GATE, _C_GATE + 2 * d_model) + bg_ref[...]
    gate_o[...] = (0.5 * jnp.tanh(0.5 * z) + 0.5).astype(gate_o.dtype)

    qa = proj(_C_QA, _C_QA + A_WIDTH)
    ka = proj(_C_KA, _C_KA + A_WIDTH)
    a_scale = A_HEAD_DIM ** -0.5 * LOG2_E
    for j in range(A_WIDTH // LANES):
        blk = slice(j * LANES, (j + 1) * LANES)
        qa_o[:, blk] = (_rotate_block(qa[:, blk], ca, sa, first_a, A_ROT_DIM // 2) * a_scale).astype(qa_o.dtype)
        ka_o[:, blk] = _rotate_block(ka[:, blk], ca, sa, first_a, A_ROT_DIM // 2).astype(ka_o.dtype)
    va_o[...] = proj(_C_VA, _C_VA + A_WIDTH).astype(va_o.dtype)

    qbt = lax.dot_general(wuqt_ref[...], qcn, _NT_DIMS, preferred_element_type=jnp.float32)
    vbt = lax.dot_general(wuvt_ref[...], kvn, _NT_DIMS, preferred_element_type=jnp.float32)
    kb = jnp.dot(kvn, wuk_ref[...], preferred_element_type=jnp.float32)
    b_scale = B_QK ** -0.5 * LOG2_E
    cbt, sbt = cbt_ref[...], sbt_ref[...]
    row = lax.broadcasted_iota(jnp.int32, (LANES, 1), 0)
    r1, r2 = B_NOPE + B_ROPE // 2, B_QK
    for h in range(B_HEADS):
        blk = slice(h * LANES, (h + 1) * LANES)
        xt = qbt[blk, :]
        partner = jnp.concatenate([xt[:B_NOPE], xt[r1:r2], xt[B_NOPE:r1], xt[r2:]], axis=0)
        qbt_o[0, 0, blk, :] = ((xt * cbt + partner * sbt) * b_scale).astype(qbt_o.dtype)
        kb_o[:, blk] = (kb[:, blk] + kpe).astype(kb_o.dtype)
        vbt_o[0, 0, blk, :] = jnp.where(row < B_V, vbt[blk, :], 1.0).astype(vbt_o.dtype)


def _in_proj(x2, g, w1, bg, qn, wuqt, kvn, wuk, wuvt, ca, sa, cb, sb, cbt, sbt, *, seq, tm):
    t, d_model = x2.shape
    n1 = w1.shape[1]
    steps_per_seq = seq // tm
    row = lambda i: (i, 0)
    const = lambda i: (0, 0)
    pos = lambda i: (i % steps_per_seq, 0)
    pos_t = lambda i: (0, i % steps_per_seq)
    tile_t = lambda i: (i // steps_per_seq, i % steps_per_seq, 0, 0)
    bf = jnp.bfloat16
    transposed = jax.ShapeDtypeStruct((t // seq, steps_per_seq, B_WIDTH, tm), bf)
    out_shape = (
        jax.ShapeDtypeStruct((t, A_WIDTH), bf), jax.ShapeDtypeStruct((t, A_WIDTH), bf),
        jax.ShapeDtypeStruct((t, A_WIDTH), bf),
        transposed, jax.ShapeDtypeStruct((t, B_WIDTH), bf), transposed,
        jax.ShapeDtypeStruct((t, 2 * d_model), bf),
    )
    return pl.pallas_call(
        functools.partial(_in_proj_kernel, d_model=d_model),
        grid=(t // tm,),
        in_specs=[
            pl.BlockSpec((tm, d_model), row),
            pl.BlockSpec((1, d_model), const),
            pl.BlockSpec((d_model, n1), const),
            pl.BlockSpec((1, 2 * d_model), const),
            pl.BlockSpec((1, B_Q_RANK), const),
            pl.BlockSpec((B_WIDTH, B_Q_RANK), const),
            pl.BlockSpec((1, B_KV_RANK), const),
            pl.BlockSpec((B_KV_RANK, B_WIDTH), const),
            pl.BlockSpec((B_WIDTH, B_KV_RANK), const),
            pl.BlockSpec((tm, LANES), pos), pl.BlockSpec((tm, LANES), pos),
            pl.BlockSpec((tm, LANES), pos), pl.BlockSpec((tm, LANES), pos),
            pl.BlockSpec((LANES, tm), pos_t), pl.BlockSpec((LANES, tm), pos_t),
        ],
        out_specs=(
            pl.BlockSpec((tm, A_WIDTH), row), pl.BlockSpec((tm, A_WIDTH), row),
            pl.BlockSpec((tm, A_WIDTH), row),
            pl.BlockSpec((1, 1, B_WIDTH, tm), tile_t), pl.BlockSpec((tm, B_WIDTH), row),
            pl.BlockSpec((1, 1, B_WIDTH, tm), tile_t),
            pl.BlockSpec((tm, 2 * d_model), row),
        ),
        out_shape=out_shape,
        compiler_params=pltpu.CompilerParams(dimension_semantics=("parallel",), vmem_limit_bytes=VMEM_LIMIT),
        name="in_proj",
    )(x2, g, w1, bg, qn, wuqt, kvn, wuk, wuvt, ca, sa, cb, sb, cbt, sbt)


def _dilated_group(qf, kf, vf, m_sc, l_sc, acc_sc, *, seq, window, dilation, first):
    length = seq // dilation
    radius = window // (2 * dilation)
    win = A_TILE + 2 * radius
    tiles = length // A_TILE
    assert length % A_TILE == 0 and tiles % A_TILES_PER_ITER == 0 and win <= length
    iters_per_residue = tiles // A_TILES_PER_ITER
    lane = lax.broadcasted_iota(jnp.int32, (1, LANES), 1)
    head0 = lane < A_HEAD_DIM
    delta = (lax.broadcasted_iota(jnp.int32, (2 * A_TILE, win), 0) % A_TILE
             - lax.broadcasted_iota(jnp.int32, (2 * A_TILE, win), 1))

    def strided(start, size):
        return pl.ds(start, size) if dilation == 1 else pl.ds(start, size, stride=dilation)

    def body(n, carry):
        r = n // iters_per_residue
        i0 = (n % iters_per_residue) * A_TILES_PER_ITER
        l0s = [(i0 + t) * A_TILE for t in range(A_TILES_PER_ITER)]
        wss = [jnp.clip(l0 - radius, 0, length - win) for l0 in l0s]
        ss = []
        for l0, ws in zip(l0s, wss):
            q = qf[strided(r + dilation * l0, A_TILE), :]
            k = kf[strided(r + dilation * ws, win), :].astype(jnp.bfloat16)
            q2 = jnp.concatenate([jnp.where(head0, q, 0.0), jnp.where(head0, 0.0, q)], axis=0)
            ss.append(lax.dot_general(q2.astype(jnp.bfloat16), k, _NT_DIMS, preferred_element_type=jnp.float32))
        stats = []
        for l0, ws, s in zip(l0s, wss, ss):
            dist = delta + (l0 - ws)
            s = jnp.where((dist <= radius) & (dist >= -radius), s, NEG_BIG)
            m = jnp.max(s, axis=-1, keepdims=True)
            p = jnp.exp2(s - m)
            stats.append((m, jnp.sum(p, axis=-1, keepdims=True), p.astype(jnp.bfloat16)))
        for l0, ws, (m, l, p) in zip(l0s, wss, stats):
            v = vf[strided(r + dilation * ws, win), :].astype(jnp.bfloat16)
            pv = jnp.dot(p, v, preferred_element_type=jnp.float32)
            acc_t = jnp.where(head0, pv[:A_TILE], pv[A_TILE:])
            m_t = jnp.where(head0, m[:A_TILE], m[A_TILE:])
            l_t = jnp.where(head0, l[:A_TILE], l[A_TILE:])
            rows = strided(r + dilation * l0, A_TILE)
            if first:
                m_sc[rows, :] = m_t
                l_sc[rows, :] = l_t
                acc_sc[rows, :] = acc_t
            else:
                m_old = m_sc[rows, :]
                m_new = jnp.maximum(m_old, m_t)
                a = jnp.exp2(m_old - m_new)
                b = jnp.exp2(m_t - m_new)
                m_sc[rows, :] = m_new
                l_sc[rows, :] = a * l_sc[rows, :] + b * l_t
                acc_sc[rows, :] = a * acc_sc[rows, :] + b * acc_t
        return carry

    lax.fori_loop(0, dilation * iters_per_residue, body, 0)


def _dilated_kernel(q_ref, k_ref, v_ref, o_ref, qf, kf, vf, m_sc, l_sc, acc_sc, *, seq):
    turn = pl.program_id(2)
    chunk = 512

    def widen(i, carry):
        rows = pl.ds(pl.multiple_of(i * chunk, chunk), chunk)
        qf[rows, :] = q_ref[0, rows, :].astype(jnp.float32)
        kf[rows, :] = k_ref[0, rows, :].astype(jnp.float32)
        vf[rows, :] = v_ref[0, rows, :].astype(jnp.float32)
        return carry

    lax.fori_loop(0, seq // chunk, widen, 0)

    for t, gi in enumerate(A_GROUP_ORDER):
        window, dilation = A_GROUPS[gi]

        @pl.when(turn == t)
        def _(window=window, dilation=dilation, t=t):
            _dilated_group(qf, kf, vf, m_sc, l_sc, acc_sc, seq=seq, window=window, dilation=dilation,
                           first=t == 0)

    @pl.when(turn == len(A_GROUPS) - 1)
    def _():
        def finish(i, carry):
            rows = pl.ds(pl.multiple_of(i * chunk, chunk), chunk)
            o_ref[0, rows, :] = (acc_sc[rows, :] / l_sc[rows, :]).astype(o_ref.dtype)
            return carry

        lax.fori_loop(0, seq // chunk, finish, 0)


def _dilated_attention(qa, ka, va):
    batch, seq, _ = qa.shape
    pairs = A_GROUP_WIDTH // LANES
    first, n = A_GROUP_ORDER[0], len(A_GROUPS)
    assert A_GROUP_ORDER == tuple((first + t) % n for t in range(n))
    in_spec = pl.BlockSpec((1, seq, LANES), lambda b, hp, t: (b, 0, ((first + t) % n) * pairs + hp))
    state = pltpu.VMEM((seq, LANES), jnp.float32)
    return pl.pallas_call(
        functools.partial(_dilated_kernel, seq=seq),
        grid=(batch, pairs, len(A_GROUPS)),
        in_specs=[in_spec, in_spec, in_spec],
        out_specs=pl.BlockSpec((1, seq, LANES), lambda b, hp, g: (b, 0, hp)),
        out_shape=jax.ShapeDtypeStruct((batch, seq, A_GROUP_WIDTH), jnp.bfloat16),
        scratch_shapes=[state] * 6,
        compiler_params=pltpu.CompilerParams(dimension_semantics=("parallel", "parallel", "arbitrary"),
                                             vmem_limit_bytes=VMEM_LIMIT),
        name="dilated",
    )(qa, ka, va)


def _mla_kernel(qt_ref, k_ref, vt_ref, o_ref, *scratch, n_groups):
    tq = qt_ref.shape[3]
    tk = vt_ref.shape[3]
    n_streams = MLA_HEADS_PER_STEP * MLA_Q_TILES_PER_STEP
    sa_ref, sb_ref, pa_ref, pb_ref = (scratch[i * n_streams:(i + 1) * n_streams] for i in range(4))
    key_blocks = tk // MLA_KEY_BLOCK
    lane = lax.broadcasted_iota(jnp.int32, (1, LANES), 1)
    heads = range(n_streams)
    blks = [slice((st % MLA_HEADS_PER_STEP) * LANES, (st % MLA_HEADS_PER_STEP + 1) * LANES) for st in heads]
    qts = [qt_ref[0, st // MLA_HEADS_PER_STEP, blks[st], :] for st in heads]

    def rows(c):
        return slice(c * MLA_KEY_BLOCK, (c + 1) * MLA_KEY_BLOCK)

    def qk_block(h, g, c, s_ref):
        base = g * tk if isinstance(g, int) else pl.multiple_of(g * tk, tk)
        k = k_ref[0, pl.ds(base + c * MLA_KEY_BLOCK, MLA_KEY_BLOCK), blks[h]]
        s = jnp.dot(k, qts[h], preferred_element_type=jnp.float32)
        s_ref[h][rows(c), :] = s
        return jnp.max(s, axis=0, keepdims=True)

    def pv(h, g, p_ref):
        vt = vt_ref[0, g, blks[h].start:blks[h].start + MLA_VT_ROWS, :]
        return jnp.dot(vt, p_ref[h][...], preferred_element_type=jnp.float32)

    def matmuls(g_qk, s_ref, g_pv, p_ref):
        tops, pvs = [None for _ in heads], [None for _ in heads]

        def qk(h):
            for c in range(key_blocks if g_qk is not None else 0):
                top = qk_block(h, g_qk, c, s_ref)
                tops[h] = top if tops[h] is None else jnp.maximum(tops[h], top)

        for lo in range(0, n_streams, MLA_SCORE_STREAMS_RUN):
            run = heads[lo:lo + MLA_SCORE_STREAMS_RUN]
            for h in run:
                qk(h)
            if g_pv is not None:
                for h in run:
                    pvs[h] = pv(h, g_pv, p_ref)
        return tops, pvs

    def softmax(h, m, top, s_ref, p_ref):
        m_new = jnp.maximum(m, top)
        for c in range(key_blocks):
            p_ref[h][rows(c), :] = jnp.exp2((s_ref[h][rows(c), :] - m_new).astype(p_ref[h].dtype))
        return m_new, jnp.exp2(m - m_new)

    def step(g, carry, s_read, s_write, p_read, p_write, first=False, last=False):
        tops, pvs = matmuls(None if last else g + 1, s_write, None if first else g - 1, p_read)
        new = []
        for h in heads:
            m, acc, alpha, top = carry[h]
            m_new, alpha_new = softmax(h, m, top, s_read, p_write)
            new.append((m_new, acc if first else alpha * acc + pvs[h], alpha_new, tops[h]))
        return tuple(new)

    def even(g, carry, **kw):
        return step(g, carry, sa_ref, sb_ref, pb_ref, pa_ref, **kw)

    def odd(g, carry, **kw):
        return step(g, carry, sb_ref, sa_ref, pa_ref, pb_ref, **kw)

    tops, _ = matmuls(0, sa_ref, None, None)
    init = tuple((jnp.full((1, tq), NEG_BIG, jnp.float32), jnp.zeros((MLA_VT_ROWS, tq), jnp.float32),
                  jnp.ones((1, tq), jnp.float32), tops[h]) for h in heads)
    carry = even(0, init, first=True)

    def pairs(g, carry, count):
        for i in range(count):
            carry = even(g + 2 * i + 1, odd(g + 2 * i, carry))
        return carry

    n_pairs = (n_groups - 2) // 2
    trips = n_pairs // MLA_PAIRS_PER_ITER
    carry = lax.fori_loop(
        0, trips, lambda t, c: pairs(2 * MLA_PAIRS_PER_ITER * t + 1, c, MLA_PAIRS_PER_ITER), carry)
    carry = pairs(2 * MLA_PAIRS_PER_ITER * trips + 1, carry, n_pairs - trips * MLA_PAIRS_PER_ITER)
    carry = odd(n_groups - 1, carry, last=True)
    _, pvs = matmuls(None, None, n_groups - 1, pb_ref)
    accs = [alpha * acc + pvs[h] for h, (_, acc, alpha, _) in enumerate(carry)]
    outs = [a[:B_V] / a[B_V:B_V + 1] for a in accs]
    for t in range(MLA_Q_TILES_PER_STEP):
        tile_outs = outs[t * MLA_HEADS_PER_STEP:(t + 1) * MLA_HEADS_PER_STEP]
        o_ref[0, t * tq:(t + 1) * tq, :] = jnp.concatenate(tile_outs, axis=0).T.astype(o_ref.dtype)


def _mla_attention(qbt, kb, vbt):
    batch, tiles, _, tile = qbt.shape
    seq = kb.shape[1]
    n, qt = MLA_HEADS_PER_STEP, MLA_Q_TILES_PER_STEP
    return pl.pallas_call(
        functools.partial(_mla_kernel, n_groups=tiles),
        grid=(batch, B_HEADS // n, tiles // qt),
        in_specs=[
            pl.BlockSpec((1, qt, n * LANES, tile), lambda b, hp, i: (b, i, hp, 0)),
            pl.BlockSpec((1, seq, n * LANES), lambda b, hp, i: (b, 0, hp)),
            pl.BlockSpec((1, tiles, n * LANES, tile), lambda b, hp, i: (b, 0, hp, 0)),
        ],
        out_specs=pl.BlockSpec((1, qt * tile, n * B_V), lambda b, hp, i: (b, i, hp)),
        out_shape=jax.ShapeDtypeStruct((batch, seq, B_HEADS * B_V), jnp.bfloat16),
        scratch_shapes=([pltpu.VMEM((tile, tile), jnp.float32)] * (2 * n * qt)
                        + [pltpu.VMEM((tile, tile), jnp.bfloat16)] * (2 * n * qt)),
        compiler_params=pltpu.CompilerParams(dimension_semantics=("parallel", "parallel", "parallel"),
                                             vmem_limit_bytes=VMEM_LIMIT),
        name="mla_attn",
    )(qbt, kb, vbt)


def _mix_ffn_kernel(x_ref, oa_ref, ob_ref, gate_ref, woa_ref, wob_ref, wout_ref, gpost_ref, gpre_ref,
                    w1_ref, w2_ref, gmlp_ref, out_o, *, d_model):
    o_a = jnp.dot(oa_ref[...], woa_ref[...], preferred_element_type=jnp.float32)
    o_b = jnp.dot(ob_ref[...], wob_ref[...], preferred_element_type=jnp.float32)
    gates = gate_ref[...].astype(jnp.float32)
    mixin = (gates[:, :d_model] * o_a + gates[:, d_model:] * o_b).astype(jnp.bfloat16)
    mix = jnp.dot(mixin, wout_ref[...], preferred_element_type=jnp.float32)
    h = x_ref[...] + _rms(mix, gpost_ref[...])
    hn = _rms(h, gpre_ref[...]).astype(jnp.bfloat16)
    u = jnp.dot(hn, w1_ref[...], preferred_element_type=jnp.float32)
    u = jnp.square(jnp.maximum(u, 0.0)).astype(jnp.bfloat16)
    ff = jnp.dot(u, w2_ref[...], preferred_element_type=jnp.float32)
    out_o[...] = h + _rms(ff, gmlp_ref[...])


def _mix_ffn(x2, oa, ob, gates, woa, wob, wout, gpost, gpre, w1, w2, gmlp, *, tm):
    t, d_model = x2.shape
    d_ff = w1.shape[1]
    row = lambda i: (i, 0)
    whole = lambda shape: pl.BlockSpec(shape, lambda i: (0, 0), pipeline_mode=pl.Buffered(1))
    return pl.pallas_call(
        functools.partial(_mix_ffn_kernel, d_model=d_model),
        grid=(t // tm,),
        in_specs=[
            pl.BlockSpec((tm, d_model), row),
            pl.BlockSpec((tm, A_GROUP_WIDTH), row),
            pl.BlockSpec((tm, B_HEADS * B_V), row),
            pl.BlockSpec((tm, 2 * d_model), row),
            whole((A_GROUP_WIDTH, d_model)),
            whole((B_HEADS * B_V, d_model)),
            whole((d_model, d_model)),
            whole((1, d_model)),
            whole((1, d_model)),
            whole((d_model, d_ff)),
            whole((d_ff, d_model)),
            whole((1, d_model)),
        ],
        out_specs=pl.BlockSpec((tm, d_model), row),
        out_shape=jax.ShapeDtypeStruct((t, d_model), jnp.float32),
        compiler_params=pltpu.CompilerParams(dimension_semantics=("parallel",), vmem_limit_bytes=VMEM_LIMIT),
        name="mix_ffn",
    )(x2, oa, ob, gates, woa, wob, wout, gpost, gpre, w1, w2, gmlp)


def _rotary_tables(seq, theta, rot_dim, period, offset):
    half = rot_dim // 2
    inv = jnp.float32(theta) ** (-(jnp.arange(half, dtype=jnp.float32) * 2.0 / rot_dim))
    ang = jnp.arange(seq, dtype=jnp.int32).astype(jnp.float32)[:, None] * inv[None, :]
    cos, sin = jnp.cos(ang), jnp.sin(ang)
    cos_p = jnp.ones((seq, period), jnp.float32).at[:, offset:offset + rot_dim].set(jnp.concatenate([cos, cos], 1))
    sin_p = jnp.zeros((seq, period), jnp.float32).at[:, offset:offset + rot_dim].set(jnp.concatenate([-sin, sin], 1))
    reps = LANES // period
    return jnp.tile(cos_p, (1, reps)), jnp.tile(sin_p, (1, reps))


def _pad_heads(w, heads, width):
    rows = w.shape[0]
    w = w.reshape(rows, heads, width)
    return jnp.pad(w, ((0, 0), (0, 0), (0, LANES - width))).reshape(rows, heads * LANES)


def kernel(x, norm_mix_pre, w_in, b_gate, mla_q_norm, mla_w_uq, mla_kv_norm, mla_w_ukv, w_o_a, w_o_b, w_out,
           norm_mix_post, norm_mlp_pre, w_ff1, w_ff2, norm_mlp_post):
    batch, seq, d_model = x.shape
    bf = jnp.bfloat16
    x2 = x.reshape(batch * seq, d_model)

    in_a = 3 * A_WIDTH
    c_kpe = in_a + B_Q_RANK + B_KV_RANK
    w_in = w_in.astype(bf)
    kpe_cols = jnp.pad(w_in[:, c_kpe:c_kpe + B_ROPE], ((0, 0), (B_NOPE, LANES - B_QK)))
    w1 = jnp.concatenate([w_in[:, :c_kpe], kpe_cols, w_in[:, c_kpe + B_ROPE:]], axis=1)
    wuqt = _pad_heads(mla_w_uq, B_HEADS, B_QK).T.astype(bf)
    wukv = mla_w_ukv.reshape(B_KV_RANK, B_HEADS, B_NOPE + B_V)
    wuk = _pad_heads(wukv[:, :, :B_NOPE].reshape(B_KV_RANK, B_HEADS * B_NOPE), B_HEADS, B_NOPE).astype(bf)
    wuvt = _pad_heads(wukv[:, :, B_NOPE:].reshape(B_KV_RANK, B_HEADS * B_V), B_HEADS, B_V).T.astype(bf)

    ca, sa = _rotary_tables(seq, A_ROPE_THETA, A_ROT_DIM, A_HEAD_DIM, 0)
    cb, sb = _rotary_tables(seq, B_ROPE_THETA, B_ROPE, LANES, B_NOPE)

    vec = lambda v: v.reshape(1, -1)
    qa, ka, va, qbt, kb, vbt, gates = _in_proj(
        x2, vec(norm_mix_pre), w1, vec(b_gate), vec(mla_q_norm), wuqt, vec(mla_kv_norm), wuk, wuvt,
        ca, sa, cb, sb, cb.T, sb.T, seq=seq, tm=512)

    shape_a = (batch, seq, A_WIDTH)
    oa = _dilated_attention(qa.reshape(shape_a), ka.reshape(shape_a), va.reshape(shape_a))
    oa = oa.reshape(batch * seq, A_GROUP_WIDTH)

    ob = _mla_attention(qbt, kb.reshape(batch, seq, B_WIDTH), vbt)
    ob = ob.reshape(batch * seq, B_HEADS * B_V)

    out = _mix_ffn(x2, oa, ob, gates, w_o_a.astype(bf), w_o_b.astype(bf), w_out.astype(bf),
                   vec(norm_mix_post), vec(norm_mlp_pre), w_ff1.astype(bf), w_ff2.astype(bf),
                   vec(norm_mlp_post), tm=512)
    return out.reshape(batch, seq, d_model)
```

```python
import functools

import jax
import jax.numpy as jnp
from jax import lax
from jax.experimental import pallas as pl
from jax.experimental.pallas import tpu as pltpu

EPS = 1e-6
LANES = 128

A_HEAD_DIM = 64
A_HEADS_PER_GROUP = 4
A_GROUPS = ((128, 1), (512, 4), (2048, 16))
A_HEADS = A_HEADS_PER_GROUP * len(A_GROUPS)
A_WIDTH = A_HEADS * A_HEAD_DIM
A_GROUP_WIDTH = A_HEADS_PER_GROUP * A_HEAD_DIM
A_ROT_DIM = A_HEAD_DIM // 4
A_ROPE_THETA = 500000.0
A_TILE = 128
A_TILES_PER_ITER = 4
A_GROUP_ORDER = (2, 0, 1)
B_HEADS = 8
B_NOPE = 64
B_ROPE = 32
B_QK = B_NOPE + B_ROPE
B_V = 64
B_Q_RANK = 512
B_KV_RANK = 256
B_ROPE_THETA = 10000.0
B_WIDTH = B_HEADS * LANES
MLA_KEY_BLOCK = 256
MLA_PAIRS_PER_ITER = 1
MLA_HEADS_PER_STEP = 2
MLA_Q_TILES_PER_STEP = 4
MLA_SCORE_STREAMS_RUN = 2
MLA_VT_ROWS = B_V + 16

NEG_BIG = -1e30
LOG2_E = 1.4426950408889634
VMEM_LIMIT = 56 * 1024 * 1024


def _rms(x, g):
    return x * lax.rsqrt(jnp.mean(x * x, axis=-1, keepdims=True) + EPS) * g


def _rotate_block(xb, cos, sin_signed, first_half, half):
    up = pltpu.roll(xb, LANES - half, 1)
    down = pltpu.roll(xb, half, 1)
    return xb * cos + jnp.where(first_half, up, down) * sin_signed


_C_QA, _C_KA, _C_VA = 0, A_WIDTH, 2 * A_WIDTH
_C_QC = 3 * A_WIDTH
_C_KVC = _C_QC + B_Q_RANK
_C_KPE = _C_KVC + B_KV_RANK
_C_GATE = _C_KPE + LANES


_NT_DIMS = (((1,), (1,)), ((), ()))


def _in_proj_kernel(x_ref, g_ref, w_ref, bg_ref, qn_ref, wuqt_ref, kvn_ref, wuk_ref, wuvt_ref,
                    ca_ref, sa_ref, cb_ref, sb_ref, cbt_ref, sbt_ref,
                    qa_o, ka_o, va_o, qbt_o, kb_o, vbt_o, gate_o, *, d_model):
    xn = _rms(x_ref[...], g_ref[...]).astype(jnp.bfloat16)

    def proj(lo, hi):
        return jnp.dot(xn, w_ref[:, lo:hi], preferred_element_type=jnp.float32)

    lane = lax.broadcasted_iota(jnp.int32, (1, LANES), 1)
    first_a = (lane % A_HEAD_DIM) < (A_ROT_DIM // 2)
    first_b = lane < (B_NOPE + B_ROPE // 2)
    ca, sa, cb, sb = ca_ref[...], sa_ref[...], cb_ref[...], sb_ref[...]

    qcn = _rms(proj(_C_QC, _C_QC + B_Q_RANK), qn_ref[...]).astype(jnp.bfloat16)
    kvn = _rms(proj(_C_KVC, _C_KVC + B_KV_RANK), kvn_ref[...]).astype(jnp.bfloat16)
    kpe = _rotate_block(proj(_C_KPE, _C_KPE + LANES), cb, sb, first_b, B_ROPE // 2)

    z = proj(_C_GATE, _C_GATE + 2 * d_model) + bg_ref[...]
    gate_o[...] = (0.5 * jnp.tanh(0.5 * z) + 0.5).astype(gate_o.dtype)

    qa = proj(_C_QA, _C_QA + A_WIDTH)
    ka = proj(_C_KA, _C_KA + A_WIDTH)
    a_scale = A_HEAD_DIM ** -0.5 * LOG2_E
    for j in range(A_WIDTH // LANES):
        blk = slice(j * LANES, (j + 1) * LANES)
        qa_o[:, blk] = (_rotate_block(qa[:, blk], ca, sa, first_a, A_ROT_DIM // 2) * a_scale).astype(qa_o.dtype)
        ka_o[:, blk] = _rotate_block(ka[:, blk], ca, sa, first_a, A_ROT_DIM // 2).astype(ka_o.dtype)
    va_o[...] = proj(_C_VA, _C_VA + A_WIDTH).astype(va_o.dtype)

    qbt = lax.dot_general(wuqt_ref[...], qcn, _NT_DIMS, preferred_element_type=jnp.float32)
    vbt = lax.dot_general(wuvt_ref[...], kvn, _NT_DIMS, preferred_element_type=jnp.float32)
    kb = jnp.dot(kvn, wuk_ref[...], preferred_element_type=jnp.float32)
    b_scale = B_QK ** -0.5 * LOG2_E
    cbt, sbt = cbt_ref[...], sbt_ref[...]
    row = lax.broadcasted_iota(jnp.int32, (LANES, 1), 0)
    r1, r2 = B_NOPE + B_ROPE // 2, B_QK
    for h in range(B_HEADS):
        blk = slice(h * LANES, (h + 1) * LANES)
        xt = qbt[blk, :]
        partner = jnp.concatenate([xt[:B_NOPE], xt[r1:r2], xt[B_NOPE:r1], xt[r2:]], axis=0)
        qbt_o[0, 0, blk, :] = ((xt * cbt + partner * sbt) * b_scale).astype(qbt_o.dtype)
        kb_o[:, blk] = (kb[:, blk] + kpe).astype(kb_o.dtype)
        vbt_o[0, 0, blk, :] = jnp.where(row < B_V, vbt[blk, :], 1.0).astype(vbt_o.dtype)


def _in_proj(x2, g, w1, bg, qn, wuqt, kvn, wuk, wuvt, ca, sa, cb, sb, cbt, sbt, *, seq, tm):
    t, d_model = x2.shape
    n1 = w1.shape[1]
    steps_per_seq = seq // tm
    row = lambda i: (i, 0)
    const = lambda i: (0, 0)
    pos = lambda i: (i % steps_per_seq, 0)
    pos_t = lambda i: (0, i % steps_per_seq)
    tile_t = lambda i: (i // steps_per_seq, i % steps_per_seq, 0, 0)
    bf = jnp.bfloat16
    transposed = jax.ShapeDtypeStruct((t // seq, steps_per_seq, B_WIDTH, tm), bf)
    out_shape = (
        jax.ShapeDtypeStruct((t, A_WIDTH), bf), jax.ShapeDtypeStruct((t, A_WIDTH), bf),
        jax.ShapeDtypeStruct((t, A_WIDTH), bf),
        transposed, jax.ShapeDtypeStruct((t, B_WIDTH), bf), transposed,
        jax.ShapeDtypeStruct((t, 2 * d_model), bf),
    )
    return pl.pallas_call(
        functools.partial(_in_proj_kernel, d_model=d_model),
        grid=(t // tm,),
        in_specs=[
            pl.BlockSpec((tm, d_model), row),
            pl.BlockSpec((1, d_model), const),
            pl.BlockSpec((d_model, n1), const),
            pl.BlockSpec((1, 2 * d_model), const),
            pl.BlockSpec((1, B_Q_RANK), const),
            pl.BlockSpec((B_WIDTH, B_Q_RANK), const),
            pl.BlockSpec((1, B_KV_RANK), const),
            pl.BlockSpec((B_KV_RANK, B_WIDTH), const),
            pl.BlockSpec((B_WIDTH, B_KV_RANK), const),
            pl.BlockSpec((tm, LANES), pos), pl.BlockSpec((tm, LANES), pos),
            pl.BlockSpec((tm, LANES), pos), pl.BlockSpec((tm, LANES), pos),
            pl.BlockSpec((LANES, tm), pos_t), pl.BlockSpec((LANES, tm), pos_t),
        ],
        out_specs=(
            pl.BlockSpec((tm, A_WIDTH), row), pl.BlockSpec((tm, A_WIDTH), row),
            pl.BlockSpec((tm, A_WIDTH), row),
            pl.BlockSpec((1, 1, B_WIDTH, tm), tile_t), pl.BlockSpec((tm, B_WIDTH), row),
            pl.BlockSpec((1, 1, B_WIDTH, tm), tile_t),
            pl.BlockSpec((tm, 2 * d_model), row),
        ),
        out_shape=out_shape,
        compiler_params=pltpu.CompilerParams(dimension_semantics=("parallel",), vmem_limit_bytes=VMEM_LIMIT),
        name="in_proj",
    )(x2, g, w1, bg, qn, wuqt, kvn, wuk, wuvt, ca, sa, cb, sb, cbt, sbt)


def _dilated_group(qf, kf, vf, m_sc, l_sc, acc_sc, *, seq, window, dilation, first):
    length = seq // dilation
    radius = window // (2 * dilation)
    win = A_TILE + 2 * radius
    tiles = length // A_TILE
    assert length % A_TILE == 0 and tiles % A_TILES_PER_ITER == 0 and win <= length
    iters_per_residue = tiles // A_TILES_PER_ITER
    lane = lax.broadcasted_iota(jnp.int32, (1, LANES), 1)
    head0 = lane < A_HEAD_DIM
    delta = (lax.broadcasted_iota(jnp.int32, (2 * A_TILE, win), 0) % A_TILE
             - lax.broadcasted_iota(jnp.int32, (2 * A_TILE, win), 1))

    def strided(start, size):
        return pl.ds(start, size) if dilation == 1 else pl.ds(start, size, stride=dilation)

    def body(n, carry):
        r = n // iters_per_residue
        i0 = (n % iters_per_residue) * A_TILES_PER_ITER
        l0s = [(i0 + t) * A_TILE for t in range(A_TILES_PER_ITER)]
        wss = [jnp.clip(l0 - radius, 0, length - win) for l0 in l0s]
        ss = []
        for l0, ws in zip(l0s, wss):
            q = qf[strided(r + dilation * l0, A_TILE), :]
            k = kf[strided(r + dilation * ws, win), :].astype(jnp.bfloat16)
            q2 = jnp.concatenate([jnp.where(head0, q, 0.0), jnp.where(head0, 0.0, q)], axis=0)
            ss.append(lax.dot_general(q2.astype(jnp.bfloat16), k, _NT_DIMS, preferred_element_type=jnp.float32))
        stats = []
        for l0, ws, s in zip(l0s, wss, ss):
            dist = delta + (l0 - ws)
            s = jnp.where((dist <= radius) & (dist >= -radius), s, NEG_BIG)
            m = jnp.max(s, axis=-1, keepdims=True)
            p = jnp.exp2(s - m)
            stats.append((m, jnp.sum(p, axis=-1, keepdims=True), p.astype(jnp.bfloat16)))
        for l0, ws, (m, l, p) in zip(l0s, wss, stats):
            v = vf[strided(r + dilation * ws, win), :].astype(jnp.bfloat16)
            pv = jnp.dot(p, v, preferred_element_type=jnp.float32)
            acc_t = jnp.where(head0, pv[:A_TILE], pv[A_TILE:])
            m_t = jnp.where(head0, m[:A_TILE], m[A_TILE:])
            l_t = jnp.where(head0, l[:A_TILE], l[A_TILE:])
            rows = strided(r + dilation * l0, A_TILE)
            if first:
                m_sc[rows, :] = m_t
                l_sc[rows, :] = l_t
                acc_sc[rows, :] = acc_t
            else:
                m_old = m_sc[rows, :]
                m_new = jnp.maximum(m_old, m_t)
                a = jnp.exp2(m_old - m_new)
                b = jnp.exp2(m_t - m_new)
                m_sc[rows, :] = m_new
                l_sc[rows, :] = a * l_sc[rows, :] + b * l_t
                acc_sc[rows, :] = a * acc_sc[rows, :] + b * acc_t
        return carry

    lax.fori_loop(0, dilation * iters_per_residue, body, 0)


def _dilated_kernel(q_ref, k_ref, v_ref, o_ref, qf, kf, vf, m_sc, l_sc, acc_sc, *, seq):
    turn = pl.program_id(2)
    chunk = 512

    def widen(i, carry):
        rows = pl.ds(pl.multiple_of(i * chunk, chunk), chunk)
        qf[rows, :] = q_ref[0, rows, :].astype(jnp.float32)
        kf[rows, :] = k_ref[0, rows, :].astype(jnp.float32)
        vf[rows, :] = v_ref[0, rows, :].astype(jnp.float32)
        return carry

    lax.fori_loop(0, seq // chunk, widen, 0)

    for t, gi in enumerate(A_GROUP_ORDER):
        window, dilation = A_GROUPS[gi]

        @pl.when(turn == t)
        def _(window=window, dilation=dilation, t=t):
            _dilated_group(qf, kf, vf, m_sc, l_sc, acc_sc, seq=seq, window=window, dilation=dilation,
                           first=t == 0)

    @pl.when(turn == len(A_GROUPS) - 1)
    def _():
        def finish(i, carry):
            rows = pl.ds(pl.multiple_of(i * chunk, chunk), chunk)
            o_ref[0, rows, :] = (acc_sc[rows, :] / l_sc[rows, :]).astype(o_ref.dtype)
            return carry

        lax.fori_loop(0, seq // chunk, finish, 0)


def _dilated_attention(qa, ka, va):
    batch, seq, _ = qa.shape
    pairs = A_GROUP_WIDTH // LANES
    first, n = A_GROUP_ORDER[0], len(A_GROUPS)
    assert A_GROUP_ORDER == tuple((first + t) % n for t in range(n))
    in_spec = pl.BlockSpec((1, seq, LANES), lambda b, hp, t: (b, 0, ((first + t) % n) * pairs + hp))
    state = pltpu.VMEM((seq, LANES), jnp.float32)
    return pl.pallas_call(
        functools.partial(_dilated_kernel, seq=seq),
        grid=(batch, pairs, len(A_GROUPS)),
        in_specs=[in_spec, in_spec, in_spec],
        out_specs=pl.BlockSpec((1, seq, LANES), lambda b, hp, g: (b, 0, hp)),
        out_shape=jax.ShapeDtypeStruct((batch, seq, A_GROUP_WIDTH), jnp.bfloat16),
        scratch_shapes=[state] * 6,
        compiler_params=pltpu.CompilerParams(dimension_semantics=("parallel", "parallel", "arbitrary"),
                                             vmem_limit_bytes=VMEM_LIMIT),
        name="dilated",
    )(qa, ka, va)


def _mla_kernel(qt_ref, k_ref, vt_ref, o_ref, *scratch, n_groups):
    tq = qt_ref.shape[3]
    tk = vt_ref.shape[3]
    n_streams = MLA_HEADS_PER_STEP * MLA_Q_TILES_PER_STEP
    sa_ref, sb_ref, pa_ref, pb_ref = (scratch[i * n_streams:(i + 1) * n_streams] for i in range(4))
    key_blocks = tk // MLA_KEY_BLOCK
    lane = lax.broadcasted_iota(jnp.int32, (1, LANES), 1)
    heads = range(n_streams)
    blks = [slice((st % MLA_HEADS_PER_STEP) * LANES, (st % MLA_HEADS_PER_STEP + 1) * LANES) for st in heads]
    qts = [qt_ref[0, st // MLA_HEADS_PER_STEP, blks[st], :] for st in heads]

    def rows(c):
        return slice(c * MLA_KEY_BLOCK, (c + 1) * MLA_KEY_BLOCK)

    def qk_block(h, g, c, s_ref):
        base = g * tk if isinstance(g, int) else pl.multiple_of(g * tk, tk)
        k = k_ref[0, pl.ds(base + c * MLA_KEY_BLOCK, MLA_KEY_BLOCK), blks[h]]
        s = jnp.dot(k, qts[h], preferred_element_type=jnp.float32)
        s_ref[h][rows(c), :] = s
        return jnp.max(s, axis=0, keepdims=True)

    def pv(h, g, p_ref):
        vt = vt_ref[0, g, blks[h].start:blks[h].start + MLA_VT_ROWS, :]
        return jnp.dot(vt, p_ref[h][...], preferred_element_type=jnp.float32)

    def matmuls(g_qk, s_ref, g_pv, p_ref):
        tops, pvs = [None for _ in heads], [None for _ in heads]

        def qk(h):
            for c in range(key_blocks if g_qk is not None else 0):
                top = qk_block(h, g_qk, c, s_ref)
                tops[h] = top if tops[h] is None else jnp.maximum(tops[h], top)

        for lo in range(0, n_streams, MLA_SCORE_STREAMS_RUN):
            run = heads[lo:lo + MLA_SCORE_STREAMS_RUN]
            for h in run:
                qk(h)
            if g_pv is not None:
                for h in run:
                    pvs[h] = pv(h, g_pv, p_ref)
        return tops, pvs

    def softmax(h, m, top, s_ref, p_ref):
        m_new = jnp.maximum(m, top)
        for c in range(key_blocks):
            p_ref[h][rows(c), :] = jnp.exp2((s_ref[h][rows(c), :] - m_new).astype(p_ref[h].dtype))
        return m_new, jnp.exp2(m - m_new)

    def step(g, carry, s_read, s_write, p_read, p_write, first=False, last=False):
        tops, pvs = matmuls(None if last else g + 1, s_write, None if first else g - 1, p_read)
        new = []
        for h in heads:
            m, acc, alpha, top = carry[h]
            m_new, alpha_new = softmax(h, m, top, s_read, p_write)
            new.append((m_new, acc if first else alpha * acc + pvs[h], alpha_new, tops[h]))
        return tuple(new)

    def even(g, carry, **kw):
        return step(g, carry, sa_ref, sb_ref, pb_ref, pa_ref, **kw)

    def odd(g, carry, **kw):
        return step(g, carry, sb_ref, sa_ref, pa_ref, pb_ref, **kw)

    tops, _ = matmuls(0, sa_ref, None, None)
    init = tuple((jnp.full((1, tq), NEG_BIG, jnp.float32), jnp.zeros((MLA_VT_ROWS, tq), jnp.float32),
                  jnp.ones((1, tq), jnp.float32), tops[h]) for h in heads)
    carry = even(0, init, first=True)

    def pairs(g, carry, count):
        for i in range(count):
            carry = even(g + 2 * i + 1, odd(g + 2 * i, carry))
        return carry

    n_pairs = (n_groups - 2) // 2
    trips = n_pairs // MLA_PAIRS_PER_ITER
    carry = lax.fori_loop(
        0, trips, lambda t, c: pairs(2 * MLA_PAIRS_PER_ITER * t + 1, c, MLA_PAIRS_PER_ITER), carry)
    carry = pairs(2 * MLA_PAIRS_PER_ITER * trips + 1, carry, n_pairs - trips * MLA_PAIRS_PER_ITER)
    carry = odd(n_groups - 1, carry, last=True)
    _, pvs = matmuls(None, None, n_groups - 1, pb_ref)
    accs = [alpha * acc + pvs[h] for h, (_, acc, alpha, _) in enumerate(carry)]
    outs = [a[:B_V] / a[B_V:B_V + 1] for a in accs]
    for t in range(MLA_Q_TILES_PER_STEP):
        tile_outs = outs[t * MLA_HEADS_PER_STEP:(t + 1) * MLA_HEADS_PER_STEP]
        o_ref[0, t * tq:(t + 1) * tq, :] = jnp.concatenate(tile_outs, axis=0).T.astype(o_ref.dtype)


def _mla_attention(qbt, kb, vbt):
    batch, tiles, _, tile = qbt.shape
    seq = kb.shape[1]
    n, qt = MLA_HEADS_PER_STEP, MLA_Q_TILES_PER_STEP
    return pl.pallas_call(
        functools.partial(_mla_kernel, n_groups=tiles),
        grid=(batch, B_HEADS // n, tiles // qt),
        in_specs=[
            pl.BlockSpec((1, qt, n * LANES, tile), lambda b, hp, i: (b, i, hp, 0)),
            pl.BlockSpec((1, seq, n * LANES), lambda b, hp, i: (b, 0, hp)),
            pl.BlockSpec((1, tiles, n * LANES, tile), lambda b, hp, i: (b, 0, hp, 0)),
        ],
        out_specs=pl.BlockSpec((1, qt * tile, n * B_V), lambda b, hp, i: (b, i, hp)),
        out_shape=jax.ShapeDtypeStruct((batch, seq, B_HEADS * B_V), jnp.bfloat16),
        scratch_shapes=([pltpu.VMEM((tile, tile), jnp.float32)] * (2 * n * qt)
                        + [pltpu.VMEM((tile, tile), jnp.bfloat16)] * (2 * n * qt)),
        compiler_params=pltpu.CompilerParams(dimension_semantics=("parallel", "parallel", "parallel"),
                                             vmem_limit_bytes=VMEM_LIMIT),
        name="mla_attn",
    )(qbt, kb, vbt)


def _mix_ffn_kernel(x_ref, oa_ref, ob_ref, gate_ref, woa_ref, wob_ref, wout_ref, gpost_ref, gpre_ref,
                    w1_ref, w2_ref, gmlp_ref, out_o, *, d_model):
    o_a = jnp.dot(oa_ref[...], woa_ref[...], preferred_element_type=jnp.float32)
    o_b = jnp.dot(ob_ref[...], wob_ref[...], preferred_element_type=jnp.float32)
    gates = gate_ref[...].astype(jnp.float32)
    mixin = (gates[:, :d_model] * o_a + gates[:, d_model:] * o_b).astype(jnp.bfloat16)
    mix = jnp.dot(mixin, wout_ref[...], preferred_element_type=jnp.float32)
    h = x_ref[...] + _rms(mix, gpost_ref[...])
    hn = _rms(h, gpre_ref[...]).astype(jnp.bfloat16)
    u = jnp.dot(hn, w1_ref[...], preferred_element_type=jnp.float32)
    u = jnp.square(jnp.maximum(u, 0.0)).astype(jnp.bfloat16)
    ff = jnp.dot(u, w2_ref[...], preferred_element_type=jnp.float32)
    out_o[...] = h + _rms(ff, gmlp_ref[...])


def _mix_ffn(x2, oa, ob, gates, woa, wob, wout, gpost, gpre, w1, w2, gmlp, *, tm):
    t, d_model = x2.shape
    d_ff = w1.shape[1]
    row = lambda i: (i, 0)
    whole = lambda shape: pl.BlockSpec(shape, lambda i: (0, 0), pipeline_mode=pl.Buffered(1))
    return pl.pallas_call(
        functools.partial(_mix_ffn_kernel, d_model=d_model),
        grid=(t // tm,),
        in_specs=[
            pl.BlockSpec((tm, d_model), row),
            pl.BlockSpec((tm, A_GROUP_WIDTH), row),
            pl.BlockSpec((tm, B_HEADS * B_V), row),
            pl.BlockSpec((tm, 2 * d_model), row),
            whole((A_GROUP_WIDTH, d_model)),
            whole((B_HEADS * B_V, d_model)),
            whole((d_model, d_model)),
            whole((1, d_model)),
            whole((1, d_model)),
            whole((d_model, d_ff)),
            whole((d_ff, d_model)),
            whole((1, d_model)),
        ],
        out_specs=pl.BlockSpec((tm, d_model), row),
        out_shape=jax.ShapeDtypeStruct((t, d_model), jnp.float32),
        compiler_params=pltpu.CompilerParams(dimension_semantics=("parallel",), vmem_limit_bytes=VMEM_LIMIT),
        name="mix_ffn",
    )(x2, oa, ob, gates, woa, wob, wout, gpost, gpre, w1, w2, gmlp)


def _rotary_tables(seq, theta, rot_dim, period, offset):
    half = rot_dim // 2
    inv = jnp.float32(theta) ** (-(jnp.arange(half, dtype=jnp.float32) * 2.0 / rot_dim))
    ang = jnp.arange(seq, dtype=jnp.int32).astype(jnp.float32)[:, None] * inv[None, :]
    cos, sin = jnp.cos(ang), jnp.sin(ang)
    cos_p = jnp.ones((seq, period), jnp.float32).at[:, offset:offset + rot_dim].set(jnp.concatenate([cos, cos], 1))
    sin_p = jnp.zeros((seq, period), jnp.float32).at[:, offset:offset + rot_dim].set(jnp.concatenate([-sin, sin], 1))
    reps = LANES // period
    return jnp.tile(cos_p, (1, reps)), jnp.tile(sin_p, (1, reps))


def _pad_heads(w, heads, width):
    rows = w.shape[0]
    w = w.reshape(rows, heads, width)
    return jnp.pad(w, ((0, 0), (0, 0), (0, LANES - width))).reshape(rows, heads * LANES)


def kernel(x, norm_mix_pre, w_in, b_gate, mla_q_norm, mla_w_uq, mla_kv_norm, mla_w_ukv, w_o_a, w_o_b, w_out,
           norm_mix_post, norm_mlp_pre, w_ff1, w_ff2, norm_mlp_post):
    batch, seq, d_model = x.shape
    bf = jnp.bfloat16
    x2 = x.reshape(batch * seq, d_model)

    in_a = 3 * A_WIDTH
    c_kpe = in_a + B_Q_RANK + B_KV_RANK
    w_in = w_in.astype(bf)
    kpe_cols = jnp.pad(w_in[:, c_kpe:c_kpe + B_ROPE], ((0, 0), (B_NOPE, LANES - B_QK)))
    w1 = jnp.concatenate([w_in[:, :c_kpe], kpe_cols, w_in[:, c_kpe + B_ROPE:]], axis=1)
    wuqt = _pad_heads(mla_w_uq, B_HEADS, B_QK).T.astype(bf)
    wukv = mla_w_ukv.reshape(B_KV_RANK, B_HEADS, B_NOPE + B_V)
    wuk = _pad_heads(wukv[:, :, :B_NOPE].reshape(B_KV_RANK, B_HEADS * B_NOPE), B_HEADS, B_NOPE).astype(bf)
    wuvt = _pad_heads(wukv[:, :, B_NOPE:].reshape(B_KV_RANK, B_HEADS * B_V), B_HEADS, B_V).T.astype(bf)

    ca, sa = _rotary_tables(seq, A_ROPE_THETA, A_ROT_DIM, A_HEAD_DIM, 0)
    cb, sb = _rotary_tables(seq, B_ROPE_THETA, B_ROPE, LANES, B_NOPE)

    vec = lambda v: v.reshape(1, -1)
    qa, ka, va, qbt, kb, vbt, gates = _in_proj(
        x2, vec(norm_mix_pre), w1, vec(b_gate), vec(mla_q_norm), wuqt, vec(mla_kv_norm), wuk, wuvt,
        ca, sa, cb, sb, cb.T, sb.T, seq=seq, tm=512)

    shape_a = (batch, seq, A_WIDTH)
    oa = _dilated_attention(qa.reshape(shape_a), ka.reshape(shape_a), va.reshape(shape_a))
    oa = oa.reshape(batch * seq, A_GROUP_WIDTH)

    ob = _mla_attention(qbt, kb.reshape(batch, seq, B_WIDTH), vbt)
    ob = ob.reshape(batch * seq, B_HEADS * B_V)

    out = _mix_ffn(x2, oa, ob, gates, w_o_a.astype(bf), w_o_b.astype(bf), w_out.astype(bf),
                   vec(norm_mix_post), vec(norm_mlp_pre), w_ff1.astype(bf), w_ff2.astype(bf),
                   vec(norm_mlp_post), tm=512)
    return out.reshape(batch, seq, d_model)
```

```python
import functools

import jax
import jax.numpy as jnp
import numpy as np
from jax import lax
from jax.experimental import pallas as pl
from jax.experimental.pallas import tpu as pltpu

EPS = 1e-6
LANES = 128

A_HEAD_DIM = 64
A_HEADS_PER_GROUP = 4
A_GROUPS = ((128, 1), (512, 4), (2048, 16))
A_HEADS = A_HEADS_PER_GROUP * len(A_GROUPS)
A_WIDTH = A_HEADS * A_HEAD_DIM
A_GROUP_WIDTH = A_HEADS_PER_GROUP * A_HEAD_DIM
A_ROT_DIM = A_HEAD_DIM // 4
A_ROPE_THETA = 500000.0
A_TILE = 128
A_TILES_PER_ITER = 4
A_GROUP_ORDER = (2, 0, 1)
B_HEADS = 8
B_NOPE = 64
B_ROPE = 32
B_QK = B_NOPE + B_ROPE
B_V = 64
B_Q_RANK = 512
B_KV_RANK = 256
B_ROPE_THETA = 10000.0
B_WIDTH = B_HEADS * LANES
MLA_KEY_BLOCK = 256
MLA_PAIRS_PER_ITER = 1
MLA_HEADS_PER_STEP = 2
MLA_Q_TILES_PER_STEP = 4
MLA_SCORE_STREAMS_RUN = 2
MLA_VT_ROWS = B_V + 16

NEG_BIG = -1e30
LOG2_E = 1.4426950408889634
VMEM_LIMIT = 56 * 1024 * 1024


def _rms(x, g):
    return x * lax.rsqrt(jnp.mean(x * x, axis=-1, keepdims=True) + EPS) * g


def _rotate_block(xb, cos, sin_signed, first_half, half):
    up = pltpu.roll(xb, LANES - half, 1)
    down = pltpu.roll(xb, half, 1)
    return xb * cos + jnp.where(first_half, up, down) * sin_signed


_C_QA, _C_KA, _C_VA = 0, A_WIDTH, 2 * A_WIDTH
_C_QC = 3 * A_WIDTH
_C_KVC = _C_QC + B_Q_RANK
_C_KPE = _C_KVC + B_KV_RANK


_NT_DIMS = (((1,), (1,)), ((), ()))


def _in_proj_kernel(x_ref, g_ref, w_ref, wkpe_ref, wgate_ref, bg_ref, qn_ref, wuqt_ref, kvn_ref, wuk_ref, wuvt_ref,
                    ca_ref, sa_ref, cb_ref, sb_ref, cbt_ref, sbt_ref,
                    qa_o, ka_o, va_o, qbt_o, kb_o, vbt_o, gate_o, *, d_model):
    xn = _rms(x_ref[...], g_ref[...]).astype(jnp.bfloat16)

    def proj(lo, hi):
        return jnp.dot(xn, w_ref[:, lo:hi], preferred_element_type=jnp.float32)

    lane = lax.broadcasted_iota(jnp.int32, (1, LANES), 1)
    first_a = (lane % A_HEAD_DIM) < (A_ROT_DIM // 2)
    first_b = lane < (B_NOPE + B_ROPE // 2)
    ca, sa, cb, sb = ca_ref[...], sa_ref[...], cb_ref[...], sb_ref[...]

    qcn = _rms(proj(_C_QC, _C_QC + B_Q_RANK), qn_ref[...]).astype(jnp.bfloat16)
    kvn = _rms(proj(_C_KVC, _C_KVC + B_KV_RANK), kvn_ref[...]).astype(jnp.bfloat16)
    kpe = jnp.dot(xn, wkpe_ref[...], preferred_element_type=jnp.float32)
    kpe = _rotate_block(kpe, cb, sb, first_b, B_ROPE // 2)

    z = jnp.dot(xn, wgate_ref[...], preferred_element_type=jnp.float32) + bg_ref[...]
    gate_o[...] = (0.5 * jnp.tanh(0.5 * z) + 0.5).astype(gate_o.dtype)

    qa = proj(_C_QA, _C_QA + A_WIDTH)
    ka = proj(_C_KA, _C_KA + A_WIDTH)
    a_scale = A_HEAD_DIM ** -0.5 * LOG2_E
    for j in range(A_WIDTH // LANES):
        blk = slice(j * LANES, (j + 1) * LANES)
        qa_o[:, blk] = (_rotate_block(qa[:, blk], ca, sa, first_a, A_ROT_DIM // 2) * a_scale).astype(qa_o.dtype)
        ka_o[:, blk] = _rotate_block(ka[:, blk], ca, sa, first_a, A_ROT_DIM // 2).astype(ka_o.dtype)
    va_o[...] = proj(_C_VA, _C_VA + A_WIDTH).astype(va_o.dtype)

    qbt = lax.dot_general(wuqt_ref[...], qcn, _NT_DIMS, preferred_element_type=jnp.float32)
    vbt = lax.dot_general(wuvt_ref[...], kvn, _NT_DIMS, preferred_element_type=jnp.float32)
    kb = jnp.dot(kvn, wuk_ref[...], preferred_element_type=jnp.float32)
    b_scale = B_QK ** -0.5 * LOG2_E
    cbt, sbt = cbt_ref[...], sbt_ref[...]
    row = lax.broadcasted_iota(jnp.int32, (LANES, 1), 0)
    r1, r2 = B_NOPE + B_ROPE // 2, B_QK
    for h in range(B_HEADS):
        blk = slice(h * LANES, (h + 1) * LANES)
        xt = qbt[blk, :]
        partner = jnp.concatenate([xt[:B_NOPE], xt[r1:r2], xt[B_NOPE:r1], xt[r2:]], axis=0)
        qbt_o[0, 0, blk, :] = ((xt * cbt + partner * sbt) * b_scale).astype(qbt_o.dtype)
        kb_o[:, blk] = (kb[:, blk] + kpe).astype(kb_o.dtype)
        vbt_o[0, 0, blk, :] = jnp.where(row < B_V, vbt[blk, :], 1.0).astype(vbt_o.dtype)


def _in_proj(x2, g, w_in, w_kpe, w_gate, bg, qn, wuqt, kvn, wuk, wuvt, ca, sa, cb, sb, cbt, sbt, *, seq, tm):
    t, d_model = x2.shape
    steps_per_seq = seq // tm
    row = lambda i: (i, 0)
    const = lambda i: (0, 0)
    pos = lambda i: (i % steps_per_seq, 0)
    pos_t = lambda i: (0, i % steps_per_seq)
    tile_t = lambda i: (i // steps_per_seq, i % steps_per_seq, 0, 0)
    bf = jnp.bfloat16
    transposed = jax.ShapeDtypeStruct((t // seq, steps_per_seq, B_WIDTH, tm), bf)
    out_shape = (
        jax.ShapeDtypeStruct((t, A_WIDTH), bf), jax.ShapeDtypeStruct((t, A_WIDTH), bf),
        jax.ShapeDtypeStruct((t, A_WIDTH), bf),
        transposed, jax.ShapeDtypeStruct((t, B_WIDTH), bf), transposed,
        jax.ShapeDtypeStruct((t, 2 * d_model), bf),
    )
    return pl.pallas_call(
        functools.partial(_in_proj_kernel, d_model=d_model),
        grid=(t // tm,),
        in_specs=[
            pl.BlockSpec((tm, d_model), row),
            pl.BlockSpec((1, d_model), const),
            pl.BlockSpec((d_model, _C_KPE), const),
            pl.BlockSpec((d_model, LANES), const),
            pl.BlockSpec((d_model, 2 * d_model), const),
            pl.BlockSpec((1, 2 * d_model), const),
            pl.BlockSpec((1, B_Q_RANK), const),
            pl.BlockSpec((B_WIDTH, B_Q_RANK), const),
            pl.BlockSpec((1, B_KV_RANK), const),
            pl.BlockSpec((B_KV_RANK, B_WIDTH), const),
            pl.BlockSpec((B_WIDTH, B_KV_RANK), const),
            pl.BlockSpec((tm, LANES), pos), pl.BlockSpec((tm, LANES), pos),
            pl.BlockSpec((tm, LANES), pos), pl.BlockSpec((tm, LANES), pos),
            pl.BlockSpec((LANES, tm), pos_t), pl.BlockSpec((LANES, tm), pos_t),
        ],
        out_specs=(
            pl.BlockSpec((tm, A_WIDTH), row), pl.BlockSpec((tm, A_WIDTH), row),
            pl.BlockSpec((tm, A_WIDTH), row),
            pl.BlockSpec((1, 1, B_WIDTH, tm), tile_t), pl.BlockSpec((tm, B_WIDTH), row),
            pl.BlockSpec((1, 1, B_WIDTH, tm), tile_t),
            pl.BlockSpec((tm, 2 * d_model), row),
        ),
        out_shape=out_shape,
        compiler_params=pltpu.CompilerParams(dimension_semantics=("parallel",), vmem_limit_bytes=VMEM_LIMIT),
        name="in_proj",
    )(x2, g, w_in, w_kpe, w_gate, bg, qn, wuqt, kvn, wuk, wuvt, ca, sa, cb, sb, cbt, sbt)


def _dilated_group(qf, kf, vf, m_sc, l_sc, acc_sc, *, seq, window, dilation, first):
    length = seq // dilation
    radius = window // (2 * dilation)
    win = A_TILE + 2 * radius
    tiles = length // A_TILE
    assert length % A_TILE == 0 and tiles % A_TILES_PER_ITER == 0 and win <= length
    iters_per_residue = tiles // A_TILES_PER_ITER
    lane = lax.broadcasted_iota(jnp.int32, (1, LANES), 1)
    head0 = lane < A_HEAD_DIM
    delta = (lax.broadcasted_iota(jnp.int32, (2 * A_TILE, win), 0) % A_TILE
             - lax.broadcasted_iota(jnp.int32, (2 * A_TILE, win), 1))

    def strided(start, size):
        return pl.ds(start, size) if dilation == 1 else pl.ds(start, size, stride=dilation)

    def body(n, carry):
        r = n // iters_per_residue
        i0 = (n % iters_per_residue) * A_TILES_PER_ITER
        l0s = [(i0 + t) * A_TILE for t in range(A_TILES_PER_ITER)]
        wss = [jnp.clip(l0 - radius, 0, length - win) for l0 in l0s]
        ss = []
        for l0, ws in zip(l0s, wss):
            q = qf[strided(r + dilation * l0, A_TILE), :]
            k = kf[strided(r + dilation * ws, win), :].astype(jnp.bfloat16)
            q2 = jnp.concatenate([jnp.where(head0, q, 0.0), jnp.where(head0, 0.0, q)], axis=0)
            ss.append(lax.dot_general(q2.astype(jnp.bfloat16), k, _NT_DIMS, preferred_element_type=jnp.float32))
        stats = []
        for l0, ws, s in zip(l0s, wss, ss):
            dist = delta + (l0 - ws)
            s = jnp.where((dist <= radius) & (dist >= -radius), s, NEG_BIG)
            m = jnp.max(s, axis=-1, keepdims=True)
            p = jnp.exp2(s - m)
            stats.append((m, jnp.sum(p, axis=-1, keepdims=True), p.astype(jnp.bfloat16)))
        for l0, ws, (m, l, p) in zip(l0s, wss, stats):
            v = vf[strided(r + dilation * ws, win), :].astype(jnp.bfloat16)
            pv = jnp.dot(p, v, preferred_element_type=jnp.float32)
            acc_t = jnp.where(head0, pv[:A_TILE], pv[A_TILE:])
            m_t = jnp.where(head0, m[:A_TILE], m[A_TILE:])
            l_t = jnp.where(head0, l[:A_TILE], l[A_TILE:])
            rows = strided(r + dilation * l0, A_TILE)
            if first:
                m_sc[rows, :] = m_t
                l_sc[rows, :] = l_t
                acc_sc[rows, :] = acc_t
            else:
                m_old = m_sc[rows, :]
                m_new = jnp.maximum(m_old, m_t)
                a = jnp.exp2(m_old - m_new)
                b = jnp.exp2(m_t - m_new)
                m_sc[rows, :] = m_new
                l_sc[rows, :] = a * l_sc[rows, :] + b * l_t
                acc_sc[rows, :] = a * acc_sc[rows, :] + b * acc_t
        return carry

    lax.fori_loop(0, dilation * iters_per_residue, body, 0)


def _dilated_kernel(q_ref, k_ref, v_ref, o_ref, qf, kf, vf, m_sc, l_sc, acc_sc, *, seq):
    turn = pl.program_id(2)
    chunk = 512

    def widen(i, carry):
        rows = pl.ds(pl.multiple_of(i * chunk, chunk), chunk)
        qf[rows, :] = q_ref[0, rows, :].astype(jnp.float32)
        kf[rows, :] = k_ref[0, rows, :].astype(jnp.float32)
        vf[rows, :] = v_ref[0, rows, :].astype(jnp.float32)
        return carry

    lax.fori_loop(0, seq // chunk, widen, 0)

    for t, gi in enumerate(A_GROUP_ORDER):
        window, dilation = A_GROUPS[gi]

        @pl.when(turn == t)
        def _(window=window, dilation=dilation, t=t):
            _dilated_group(qf, kf, vf, m_sc, l_sc, acc_sc, seq=seq, window=window, dilation=dilation,
                           first=t == 0)

    @pl.when(turn == len(A_GROUPS) - 1)
    def _():
        def finish(i, carry):
            rows = pl.ds(pl.multiple_of(i * chunk, chunk), chunk)
            o_ref[0, rows, :] = (acc_sc[rows, :] / l_sc[rows, :]).astype(o_ref.dtype)
            return carry

        lax.fori_loop(0, seq // chunk, finish, 0)


def _dilated_attention(qa, ka, va):
    batch, seq, _ = qa.shape
    pairs = A_GROUP_WIDTH // LANES
    first, n = A_GROUP_ORDER[0], len(A_GROUPS)
    assert A_GROUP_ORDER == tuple((first + t) % n for t in range(n))
    in_spec = pl.BlockSpec((1, seq, LANES), lambda b, hp, t: (b, 0, ((first + t) % n) * pairs + hp))
    state = pltpu.VMEM((seq, LANES), jnp.float32)
    return pl.pallas_call(
        functools.partial(_dilated_kernel, seq=seq),
        grid=(batch, pairs, len(A_GROUPS)),
        in_specs=[in_spec, in_spec, in_spec],
        out_specs=pl.BlockSpec((1, seq, LANES), lambda b, hp, g: (b, 0, hp)),
        out_shape=jax.ShapeDtypeStruct((batch, seq, A_GROUP_WIDTH), jnp.bfloat16),
        scratch_shapes=[state] * 6,
        compiler_params=pltpu.CompilerParams(dimension_semantics=("parallel", "parallel", "arbitrary"),
                                             vmem_limit_bytes=VMEM_LIMIT),
        name="dilated",
    )(qa, ka, va)


def _mla_kernel(qt_ref, k_ref, vt_ref, o_ref, *scratch, n_groups):
    tq = qt_ref.shape[3]
    tk = vt_ref.shape[3]
    n_streams = MLA_HEADS_PER_STEP * MLA_Q_TILES_PER_STEP
    sa_ref, sb_ref, pa_ref, pb_ref = (scratch[i * n_streams:(i + 1) * n_streams] for i in range(4))
    key_blocks = tk // MLA_KEY_BLOCK
    lane = lax.broadcasted_iota(jnp.int32, (1, LANES), 1)
    heads = range(n_streams)
    blks = [slice((st % MLA_HEADS_PER_STEP) * LANES, (st % MLA_HEADS_PER_STEP + 1) * LANES) for st in heads]
    qts = [qt_ref[0, st // MLA_HEADS_PER_STEP, blks[st], :] for st in heads]

    def rows(c):
        return slice(c * MLA_KEY_BLOCK, (c + 1) * MLA_KEY_BLOCK)

    def qk_block(h, g, c, s_ref):
        base = g * tk if isinstance(g, int) else pl.multiple_of(g * tk, tk)
        k = k_ref[0, pl.ds(base + c * MLA_KEY_BLOCK, MLA_KEY_BLOCK), blks[h]]
        s = jnp.dot(k, qts[h], preferred_element_type=jnp.float32)
        s_ref[h][rows(c), :] = s
        return jnp.max(s, axis=0, keepdims=True)

    def pv(h, g, p_ref):
        vt = vt_ref[0, g, blks[h].start:blks[h].start + MLA_VT_ROWS, :]
        return jnp.dot(vt, p_ref[h][...], preferred_element_type=jnp.float32)

    def matmuls(g_qk, s_ref, g_pv, p_ref):
        tops, pvs = [None for _ in heads], [None for _ in heads]

        def qk(h):
            for c in range(key_blocks if g_qk is not None else 0):
                top = qk_block(h, g_qk, c, s_ref)
                tops[h] = top if tops[h] is None else jnp.maximum(tops[h], top)

        for lo in range(0, n_streams, MLA_SCORE_STREAMS_RUN):
            run = heads[lo:lo + MLA_SCORE_STREAMS_RUN]
            for h in run:
                qk(h)
            if g_pv is not None:
                for h in run:
                    pvs[h] = pv(h, g_pv, p_ref)
        return tops, pvs

    def softmax(h, m, top, s_ref, p_ref):
        m_new = jnp.maximum(m, top)
        for c in range(key_blocks):
            p_ref[h][rows(c), :] = jnp.exp2((s_ref[h][rows(c), :] - m_new).astype(p_ref[h].dtype))
        return m_new, jnp.exp2(m - m_new)

    def step(g, carry, s_read, s_write, p_read, p_write, first=False, last=False):
        tops, pvs = matmuls(None if last else g + 1, s_write, None if first else g - 1, p_read)
        new = []
        for h in heads:
            m, acc, alpha, top = carry[h]
            m_new, alpha_new = softmax(h, m, top, s_read, p_write)
            new.append((m_new, acc if first else alpha * acc + pvs[h], alpha_new, tops[h]))
        return tuple(new)

    def even(g, carry, **kw):
        return step(g, carry, sa_ref, sb_ref, pb_ref, pa_ref, **kw)

    def odd(g, carry, **kw):
        return step(g, carry, sb_ref, sa_ref, pa_ref, pb_ref, **kw)

    tops, _ = matmuls(0, sa_ref, None, None)
    init = tuple((jnp.full((1, tq), NEG_BIG, jnp.float32), jnp.zeros((MLA_VT_ROWS, tq), jnp.float32),
                  jnp.ones((1, tq), jnp.float32), tops[h]) for h in heads)
    carry = even(0, init, first=True)

    def pairs(g, carry, count):
        for i in range(count):
            carry = even(g + 2 * i + 1, odd(g + 2 * i, carry))
        return carry

    n_pairs = (n_groups - 2) // 2
    trips = n_pairs // MLA_PAIRS_PER_ITER
    carry = lax.fori_loop(
        0, trips, lambda t, c: pairs(2 * MLA_PAIRS_PER_ITER * t + 1, c, MLA_PAIRS_PER_ITER), carry)
    carry = pairs(2 * MLA_PAIRS_PER_ITER * trips + 1, carry, n_pairs - trips * MLA_PAIRS_PER_ITER)
    carry = odd(n_groups - 1, carry, last=True)
    _, pvs = matmuls(None, None, n_groups - 1, pb_ref)
    accs = [alpha * acc + pvs[h] for h, (_, acc, alpha, _) in enumerate(carry)]
    outs = [a[:B_V] / a[B_V:B_V + 1] for a in accs]
    for t in range(MLA_Q_TILES_PER_STEP):
        tile_outs = outs[t * MLA_HEADS_PER_STEP:(t + 1) * MLA_HEADS_PER_STEP]
        o_ref[0, t * tq:(t + 1) * tq, :] = jnp.concatenate(tile_outs, axis=0).T.astype(o_ref.dtype)


def _mla_attention(qbt, kb, vbt):
    batch, tiles, _, tile = qbt.shape
    seq = kb.shape[1]
    n, qt = MLA_HEADS_PER_STEP, MLA_Q_TILES_PER_STEP
    return pl.pallas_call(
        functools.partial(_mla_kernel, n_groups=tiles),
        grid=(batch, B_HEADS // n, tiles // qt),
        in_specs=[
            pl.BlockSpec((1, qt, n * LANES, tile), lambda b, hp, i: (b, i, hp, 0)),
            pl.BlockSpec((1, seq, n * LANES), lambda b, hp, i: (b, 0, hp)),
            pl.BlockSpec((1, tiles, n * LANES, tile), lambda b, hp, i: (b, 0, hp, 0)),
        ],
        out_specs=pl.BlockSpec((1, qt * tile, n * B_V), lambda b, hp, i: (b, i, hp)),
        out_shape=jax.ShapeDtypeStruct((batch, seq, B_HEADS * B_V), jnp.bfloat16),
        scratch_shapes=([pltpu.VMEM((tile, tile), jnp.float32)] * (2 * n * qt)
                        + [pltpu.VMEM((tile, tile), jnp.bfloat16)] * (2 * n * qt)),
        compiler_params=pltpu.CompilerParams(dimension_semantics=("parallel", "parallel", "parallel"),
                                             vmem_limit_bytes=VMEM_LIMIT),
        name="mla_attn",
    )(qbt, kb, vbt)


def _mix_ffn_kernel(x_ref, oa_ref, ob_ref, gate_ref, woa_ref, wob_ref, wout_ref, gpost_ref, gpre_ref,
                    w1_ref, w2_ref, gmlp_ref, out_o, *, d_model):
    o_a = jnp.dot(oa_ref[...], woa_ref[...], preferred_element_type=jnp.float32)
    o_b = jnp.dot(ob_ref[...], wob_ref[...], preferred_element_type=jnp.float32)
    gates = gate_ref[...].astype(jnp.float32)
    mixin = (gates[:, :d_model] * o_a + gates[:, d_model:] * o_b).astype(jnp.bfloat16)
    mix = jnp.dot(mixin, wout_ref[...], preferred_element_type=jnp.float32)
    h = x_ref[...] + _rms(mix, gpost_ref[...])
    hn = _rms(h, gpre_ref[...]).astype(jnp.bfloat16)
    u = jnp.dot(hn, w1_ref[...], preferred_element_type=jnp.float32)
    u = jnp.square(jnp.maximum(u, 0.0)).astype(jnp.bfloat16)
    ff = jnp.dot(u, w2_ref[...], preferred_element_type=jnp.float32)
    out_o[...] = h + _rms(ff, gmlp_ref[...])


def _mix_ffn(x2, oa, ob, gates, woa, wob, wout, gpost, gpre, w1, w2, gmlp, *, tm):
    t, d_model = x2.shape
    d_ff = w1.shape[1]
    row = lambda i: (i, 0)
    whole = lambda shape: pl.BlockSpec(shape, lambda i: (0, 0), pipeline_mode=pl.Buffered(1))
    return pl.pallas_call(
        functools.partial(_mix_ffn_kernel, d_model=d_model),
        grid=(t // tm,),
        in_specs=[
            pl.BlockSpec((tm, d_model), row),
            pl.BlockSpec((tm, A_GROUP_WIDTH), row),
            pl.BlockSpec((tm, B_HEADS * B_V), row),
            pl.BlockSpec((tm, 2 * d_model), row),
            whole((A_GROUP_WIDTH, d_model)),
            whole((B_HEADS * B_V, d_model)),
            whole((d_model, d_model)),
            whole((1, d_model)),
            whole((1, d_model)),
            whole((d_model, d_ff)),
            whole((d_ff, d_model)),
            whole((1, d_model)),
        ],
        out_specs=pl.BlockSpec((tm, d_model), row),
        out_shape=jax.ShapeDtypeStruct((t, d_model), jnp.float32),
        compiler_params=pltpu.CompilerParams(dimension_semantics=("parallel",), vmem_limit_bytes=VMEM_LIMIT),
        name="mix_ffn",
    )(x2, oa, ob, gates, woa, wob, wout, gpost, gpre, w1, w2, gmlp)


def _rotary_tables(seq, theta, rot_dim, period, offset):
    half = rot_dim // 2
    inv = np.float64(np.float32(theta)) ** (-(np.arange(half, dtype=np.float64) * 2.0 / rot_dim))
    ang = np.arange(seq, dtype=np.float64)[:, None] * inv[None, :]
    cos, sin = np.cos(ang), np.sin(ang)
    cos_p = np.ones((seq, period))
    sin_p = np.zeros((seq, period))
    cos_p[:, offset:offset + rot_dim] = np.concatenate([cos, cos], 1)
    sin_p[:, offset:offset + rot_dim] = np.concatenate([-sin, sin], 1)
    reps = LANES // period
    return np.tile(cos_p, (1, reps)).astype(np.float32), np.tile(sin_p, (1, reps)).astype(np.float32)


def _pad_heads(w, heads, width):
    rows = w.shape[0]
    w = w.reshape(rows, heads, width)
    return jnp.pad(w, ((0, 0), (0, 0), (0, LANES - width))).reshape(rows, heads * LANES)


def kernel(x, norm_mix_pre, w_in, b_gate, mla_q_norm, mla_w_uq, mla_kv_norm, mla_w_ukv, w_o_a, w_o_b, w_out,
           norm_mix_post, norm_mlp_pre, w_ff1, w_ff2, norm_mlp_post):
    batch, seq, d_model = x.shape
    bf = jnp.bfloat16
    x2 = x.reshape(batch * seq, d_model)

    w_in = w_in.astype(bf)
    w_kpe = jnp.pad(w_in[:, _C_KPE:_C_KPE + B_ROPE], ((0, 0), (B_NOPE, LANES - B_QK)))
    w_gate = w_in[:, _C_KPE + B_ROPE:]
    wuqt = _pad_heads(mla_w_uq, B_HEADS, B_QK).T.astype(bf)
    wukv = mla_w_ukv.reshape(B_KV_RANK, B_HEADS, B_NOPE + B_V)
    wuk = _pad_heads(wukv[:, :, :B_NOPE].reshape(B_KV_RANK, B_HEADS * B_NOPE), B_HEADS, B_NOPE).astype(bf)
    wuvt = _pad_heads(wukv[:, :, B_NOPE:].reshape(B_KV_RANK, B_HEADS * B_V), B_HEADS, B_V).T.astype(bf)

    ca, sa = _rotary_tables(seq, A_ROPE_THETA, A_ROT_DIM, A_HEAD_DIM, 0)
    cb, sb = _rotary_tables(seq, B_ROPE_THETA, B_ROPE, LANES, B_NOPE)
    cbt, sbt = np.ascontiguousarray(cb.T), np.ascontiguousarray(sb.T)

    vec = lambda v: v.reshape(1, -1)
    qa, ka, va, qbt, kb, vbt, gates = _in_proj(
        x2, vec(norm_mix_pre), w_in, w_kpe, w_gate, vec(b_gate), vec(mla_q_norm), wuqt, vec(mla_kv_norm), wuk, wuvt,
        ca, sa, cb, sb, cbt, sbt, seq=seq, tm=512)

    shape_a = (batch, seq, A_WIDTH)
    oa = _dilated_attention(qa.reshape(shape_a), ka.reshape(shape_a), va.reshape(shape_a))
    oa = oa.reshape(batch * seq, A_GROUP_WIDTH)

    ob = _mla_attention(qbt, kb.reshape(batch, seq, B_WIDTH), vbt)
    ob = ob.reshape(batch * seq, B_HEADS * B_V)

    out = _mix_ffn(x2, oa, ob, gates, w_o_a.astype(bf), w_o_b.astype(bf), w_out.astype(bf),
                   vec(norm_mix_post), vec(norm_mlp_pre), w_ff1.astype(bf), w_ff2.astype(bf),
                   vec(norm_mlp_post), tm=512)
    return out.reshape(batch, seq, d_model)
```

```python
import functools

import jax
import jax.numpy as jnp
import numpy as np
from jax import lax
from jax.experimental import pallas as pl
from jax.experimental.pallas import tpu as pltpu

EPS = 1e-6
LANES = 128

A_HEAD_DIM = 64
A_HEADS_PER_GROUP = 4
A_GROUPS = ((128, 1), (512, 4), (2048, 16))
A_HEADS = A_HEADS_PER_GROUP * len(A_GROUPS)
A_WIDTH = A_HEADS * A_HEAD_DIM
A_GROUP_WIDTH = A_HEADS_PER_GROUP * A_HEAD_DIM
A_ROT_DIM = A_HEAD_DIM // 4
A_ROPE_THETA = 500000.0
A_TILE = 128
A_TILES_PER_ITER = 4
A_GROUP_ORDER = (2, 0, 1)
B_HEADS = 8
B_NOPE = 64
B_ROPE = 32
B_QK = B_NOPE + B_ROPE
B_V = 64
B_Q_RANK = 512
B_KV_RANK = 256
B_ROPE_THETA = 10000.0
B_WIDTH = B_HEADS * LANES
MLA_KEY_BLOCK = 256
MLA_PAIRS_PER_ITER = 1
MLA_HEADS_PER_STEP = 2
MLA_Q_TILES_PER_STEP = 4
MLA_SCORE_STREAMS_RUN = 2
MLA_VT_ROWS = B_V + 16

NEG_BIG = -1e30
LOG2_E = 1.4426950408889634
VMEM_LIMIT = 56 * 1024 * 1024


def _rms(x, g):
    return x * lax.rsqrt(jnp.mean(x * x, axis=-1, keepdims=True) + EPS) * g


def _rotate_block(xb, cos, sin_signed, first_half, half):
    up = pltpu.roll(xb, LANES - half, 1)
    down = pltpu.roll(xb, half, 1)
    return xb * cos + jnp.where(first_half, up, down) * sin_signed


_C_QA, _C_KA, _C_VA = 0, A_WIDTH, 2 * A_WIDTH
_C_QC = 3 * A_WIDTH
_C_KVC = _C_QC + B_Q_RANK
_C_KPE = _C_KVC + B_KV_RANK


_NT_DIMS = (((1,), (1,)), ((), ()))


def _in_proj_kernel(x_ref, g_ref, w_ref, wkpe_ref, wgate_ref, bg_ref, qn_ref, wuqt_ref, kvn_ref, wuk_ref, wuvt_ref,
                    ca_ref, sa_ref, cb_ref, sb_ref, cbt_ref, sbt_ref,
                    qa_o, ka_o, va_o, qbt_o, kb_o, vbt_o, gate_o, *, d_model):
    xn = _rms(x_ref[...], g_ref[...]).astype(jnp.bfloat16)

    def proj(lo, hi):
        return jnp.dot(xn, w_ref[:, lo:hi], preferred_element_type=jnp.float32)

    lane = lax.broadcasted_iota(jnp.int32, (1, LANES), 1)
    first_a = (lane % A_HEAD_DIM) < (A_ROT_DIM // 2)
    first_b = lane < (B_NOPE + B_ROPE // 2)
    ca, sa, cb, sb = ca_ref[...], sa_ref[...], cb_ref[...], sb_ref[...]

    qcn = _rms(proj(_C_QC, _C_QC + B_Q_RANK), qn_ref[...]).astype(jnp.bfloat16)
    kvn = _rms(proj(_C_KVC, _C_KVC + B_KV_RANK), kvn_ref[...]).astype(jnp.bfloat16)
    kpe = jnp.dot(xn, wkpe_ref[...], preferred_element_type=jnp.float32)
    kpe = _rotate_block(kpe, cb, sb, first_b, B_ROPE // 2)

    z = jnp.dot(xn, wgate_ref[...], preferred_element_type=jnp.float32) + bg_ref[...]
    gate_o[...] = (0.5 * jnp.tanh(0.5 * z) + 0.5).astype(gate_o.dtype)

    qa = proj(_C_QA, _C_QA + A_WIDTH)
    ka = proj(_C_KA, _C_KA + A_WIDTH)
    a_scale = A_HEAD_DIM ** -0.5 * LOG2_E
    for j in range(A_WIDTH // LANES):
        blk = slice(j * LANES, (j + 1) * LANES)
        qa_o[:, blk] = (_rotate_block(qa[:, blk], ca, sa, first_a, A_ROT_DIM // 2) * a_scale).astype(qa_o.dtype)
        ka_o[:, blk] = _rotate_block(ka[:, blk], ca, sa, first_a, A_ROT_DIM // 2).astype(ka_o.dtype)
    va_o[...] = proj(_C_VA, _C_VA + A_WIDTH).astype(va_o.dtype)

    qbt = lax.dot_general(wuqt_ref[...], qcn, _NT_DIMS, preferred_element_type=jnp.float32)
    vbt = lax.dot_general(wuvt_ref[...], kvn, _NT_DIMS, preferred_element_type=jnp.float32)
    kb = jnp.dot(kvn, wuk_ref[...], preferred_element_type=jnp.float32)
    b_scale = B_QK ** -0.5 * LOG2_E
    cbt, sbt = cbt_ref[...], sbt_ref[...]
    row = lax.broadcasted_iota(jnp.int32, (LANES, 1), 0)
    r1, r2 = B_NOPE + B_ROPE // 2, B_QK
    for h in range(B_HEADS):
        blk = slice(h * LANES, (h + 1) * LANES)
        xt = qbt[blk, :]
        partner = jnp.concatenate([xt[:B_NOPE], xt[r1:r2], xt[B_NOPE:r1], xt[r2:]], axis=0)
        qbt_o[0, 0, blk, :] = ((xt * cbt + partner * sbt) * b_scale).astype(qbt_o.dtype)
        kb_o[:, blk] = (kb[:, blk] + kpe).astype(kb_o.dtype)
        vbt_o[0, 0, blk, :] = jnp.where(row < B_V, vbt[blk, :], 1.0).astype(vbt_o.dtype)


def _in_proj(x2, g, w_in, w_kpe, w_gate, bg, qn, wuqt, kvn, wuk, wuvt, ca, sa, cb, sb, cbt, sbt, *, seq, tm):
    t, d_model = x2.shape
    steps_per_seq = seq // tm
    row = lambda i: (i, 0)
    const = lambda i: (0, 0)
    pos = lambda i: (i % steps_per_seq, 0)
    pos_t = lambda i: (0, i % steps_per_seq)
    tile_t = lambda i: (i // steps_per_seq, i % steps_per_seq, 0, 0)
    bf = jnp.bfloat16
    transposed = jax.ShapeDtypeStruct((t // seq, steps_per_seq, B_WIDTH, tm), bf)
    out_shape = (
        jax.ShapeDtypeStruct((t, A_WIDTH), bf), jax.ShapeDtypeStruct((t, A_WIDTH), bf),
        jax.ShapeDtypeStruct((t, A_WIDTH), bf),
        transposed, jax.ShapeDtypeStruct((t, B_WIDTH), bf), transposed,
        jax.ShapeDtypeStruct((t, 2 * d_model), bf),
    )
    return pl.pallas_call(
        functools.partial(_in_proj_kernel, d_model=d_model),
        grid=(t // tm,),
        in_specs=[
            pl.BlockSpec((tm, d_model), row),
            pl.BlockSpec((1, d_model), const),
            pl.BlockSpec((d_model, _C_KPE), const),
            pl.BlockSpec((d_model, LANES), const),
            pl.BlockSpec((d_model, 2 * d_model), const),
            pl.BlockSpec((1, 2 * d_model), const),
            pl.BlockSpec((1, B_Q_RANK), const),
            pl.BlockSpec((B_WIDTH, B_Q_RANK), const),
            pl.BlockSpec((1, B_KV_RANK), const),
            pl.BlockSpec((B_KV_RANK, B_WIDTH), const),
            pl.BlockSpec((B_WIDTH, B_KV_RANK), const),
            pl.BlockSpec((tm, LANES), pos), pl.BlockSpec((tm, LANES), pos),
            pl.BlockSpec((tm, LANES), pos), pl.BlockSpec((tm, LANES), pos),
            pl.BlockSpec((LANES, tm), pos_t), pl.BlockSpec((LANES, tm), pos_t),
        ],
        out_specs=(
            pl.BlockSpec((tm, A_WIDTH), row), pl.BlockSpec((tm, A_WIDTH), row),
            pl.BlockSpec((tm, A_WIDTH), row),
            pl.BlockSpec((1, 1, B_WIDTH, tm), tile_t), pl.BlockSpec((tm, B_WIDTH), row),
            pl.BlockSpec((1, 1, B_WIDTH, tm), tile_t),
            pl.BlockSpec((tm, 2 * d_model), row),
        ),
        out_shape=out_shape,
        compiler_params=pltpu.CompilerParams(dimension_semantics=("parallel",), vmem_limit_bytes=VMEM_LIMIT),
        name="in_proj",
    )(x2, g, w_in, w_kpe, w_gate, bg, qn, wuqt, kvn, wuk, wuvt, ca, sa, cb, sb, cbt, sbt)


def _dilated_group(qf, kf, vf, m_sc, l_sc, acc_sc, *, seq, window, dilation, first):
    length = seq // dilation
    radius = window // (2 * dilation)
    win = A_TILE + 2 * radius
    tiles = length // A_TILE
    assert length % A_TILE == 0 and tiles % A_TILES_PER_ITER == 0 and win <= length
    iters_per_residue = tiles // A_TILES_PER_ITER
    lane = lax.broadcasted_iota(jnp.int32, (1, LANES), 1)
    head0 = lane < A_HEAD_DIM
    delta = (lax.broadcasted_iota(jnp.int32, (2 * A_TILE, win), 0) % A_TILE
             - lax.broadcasted_iota(jnp.int32, (2 * A_TILE, win), 1))

    def strided(start, size):
        if dilation == 1:
            return pl.ds(pl.multiple_of(start, radius), size)
        return pl.ds(start, size, stride=dilation)

    def body(n, carry):
        r = n // iters_per_residue
        i0 = (n % iters_per_residue) * A_TILES_PER_ITER
        l0s = [(i0 + t) * A_TILE for t in range(A_TILES_PER_ITER)]
        wss = [jnp.clip(l0 - radius, 0, length - win) for l0 in l0s]
        ss = []
        for l0, ws in zip(l0s, wss):
            q = qf[strided(r + dilation * l0, A_TILE), :]
            k = kf[strided(r + dilation * ws, win), :].astype(jnp.bfloat16)
            q2 = jnp.concatenate([jnp.where(head0, q, 0), jnp.where(head0, 0, q)], axis=0)
            ss.append(lax.dot_general(q2.astype(jnp.bfloat16), k, _NT_DIMS, preferred_element_type=jnp.float32))
        stats = []
        for l0, ws, s in zip(l0s, wss, ss):
            dist = delta + (l0 - ws)
            s = jnp.where((dist <= radius) & (dist >= -radius), s, NEG_BIG)
            m = jnp.max(s, axis=-1, keepdims=True)
            p = jnp.exp2(s - m)
            stats.append((m, jnp.sum(p, axis=-1, keepdims=True), p.astype(jnp.bfloat16)))
        for l0, ws, (m, l, p) in zip(l0s, wss, stats):
            v = vf[strided(r + dilation * ws, win), :].astype(jnp.bfloat16)
            pv = jnp.dot(p, v, preferred_element_type=jnp.float32)
            acc_t = jnp.where(head0, pv[:A_TILE], pv[A_TILE:])
            m_t = jnp.where(head0, m[:A_TILE], m[A_TILE:])
            l_t = jnp.where(head0, l[:A_TILE], l[A_TILE:])
            rows = strided(r + dilation * l0, A_TILE)
            if first:
                m_sc[rows, :] = m_t
                l_sc[rows, :] = l_t
                acc_sc[rows, :] = acc_t
            else:
                m_old = m_sc[rows, :]
                m_new = jnp.maximum(m_old, m_t)
                a = jnp.exp2(m_old - m_new)
                b = jnp.exp2(m_t - m_new)
                m_sc[rows, :] = m_new
                l_sc[rows, :] = a * l_sc[rows, :] + b * l_t
                acc_sc[rows, :] = a * acc_sc[rows, :] + b * acc_t
        return carry

    lax.fori_loop(0, dilation * iters_per_residue, body, 0)


def _dilated_kernel(q_ref, k_ref, v_ref, o_ref, qf, kf, vf, m_sc, l_sc, acc_sc, *, seq):
    turn = pl.program_id(2)
    chunk = 512

    def widen(i, carry):
        rows = pl.ds(pl.multiple_of(i * chunk, chunk), chunk)
        qf[rows, :] = q_ref[0, rows, :].astype(jnp.float32)
        kf[rows, :] = k_ref[0, rows, :].astype(jnp.float32)
        vf[rows, :] = v_ref[0, rows, :].astype(jnp.float32)
        return carry

    for t, gi in enumerate(A_GROUP_ORDER):
        window, dilation = A_GROUPS[gi]

        @pl.when(turn == t)
        def _(window=window, dilation=dilation, t=t):
            if dilation == 1:
                srcs = (q_ref.at[0], k_ref.at[0], v_ref.at[0])
            else:
                lax.fori_loop(0, seq // chunk, widen, 0)
                srcs = (qf, kf, vf)
            _dilated_group(*srcs, m_sc, l_sc, acc_sc, seq=seq, window=window, dilation=dilation, first=t == 0)

    @pl.when(turn == len(A_GROUPS) - 1)
    def _():
        def finish(i, carry):
            rows = pl.ds(pl.multiple_of(i * chunk, chunk), chunk)
            o_ref[0, rows, :] = (acc_sc[rows, :] / l_sc[rows, :]).astype(o_ref.dtype)
            return carry

        lax.fori_loop(0, seq // chunk, finish, 0)


def _dilated_attention(qa, ka, va):
    batch, seq, _ = qa.shape
    pairs = A_GROUP_WIDTH // LANES
    first, n = A_GROUP_ORDER[0], len(A_GROUPS)
    assert A_GROUP_ORDER == tuple((first + t) % n for t in range(n))
    in_spec = pl.BlockSpec((1, seq, LANES), lambda b, hp, t: (b, 0, ((first + t) % n) * pairs + hp))
    state = pltpu.VMEM((seq, LANES), jnp.float32)
    return pl.pallas_call(
        functools.partial(_dilated_kernel, seq=seq),
        grid=(batch, pairs, len(A_GROUPS)),
        in_specs=[in_spec, in_spec, in_spec],
        out_specs=pl.BlockSpec((1, seq, LANES), lambda b, hp, g: (b, 0, hp)),
        out_shape=jax.ShapeDtypeStruct((batch, seq, A_GROUP_WIDTH), jnp.bfloat16),
        scratch_shapes=[state] * 6,
        compiler_params=pltpu.CompilerParams(dimension_semantics=("parallel", "parallel", "arbitrary"),
                                             vmem_limit_bytes=VMEM_LIMIT),
        name="dilated",
    )(qa, ka, va)


def _mla_kernel(qt_ref, k_ref, vt_ref, o_ref, *scratch, n_groups):
    tq = qt_ref.shape[3]
    tk = vt_ref.shape[3]
    n_streams = MLA_HEADS_PER_STEP * MLA_Q_TILES_PER_STEP
    sa_ref, sb_ref, pa_ref, pb_ref = (scratch[i * n_streams:(i + 1) * n_streams] for i in range(4))
    key_blocks = tk // MLA_KEY_BLOCK
    lane = lax.broadcasted_iota(jnp.int32, (1, LANES), 1)
    heads = range(n_streams)
    blks = [slice((st % MLA_HEADS_PER_STEP) * LANES, (st % MLA_HEADS_PER_STEP + 1) * LANES) for st in heads]
    qts = [qt_ref[0, st // MLA_HEADS_PER_STEP, blks[st], :] for st in heads]

    def rows(c):
        return slice(c * MLA_KEY_BLOCK, (c + 1) * MLA_KEY_BLOCK)

    def qk_block(h, g, c, s_ref):
        base = g * tk if isinstance(g, int) else pl.multiple_of(g * tk, tk)
        k = k_ref[0, pl.ds(base + c * MLA_KEY_BLOCK, MLA_KEY_BLOCK), blks[h]]
        s = jnp.dot(k, qts[h], preferred_element_type=jnp.float32)
        s_ref[h][rows(c), :] = s
        return jnp.max(s, axis=0, keepdims=True)

    def pv(h, g, p_ref):
        vt = vt_ref[0, g, blks[h].start:blks[h].start + MLA_VT_ROWS, :]
        return jnp.dot(vt, p_ref[h][...], preferred_element_type=jnp.float32)

    def matmuls(g_qk, s_ref, g_pv, p_ref):
        tops, pvs = [None for _ in heads], [None for _ in heads]

        def qk(h):
            for c in range(key_blocks if g_qk is not None else 0):
                top = qk_block(h, g_qk, c, s_ref)
                tops[h] = top if tops[h] is None else jnp.maximum(tops[h], top)

        for lo in range(0, n_streams, MLA_SCORE_STREAMS_RUN):
            run = heads[lo:lo + MLA_SCORE_STREAMS_RUN]
            for h in run:
                qk(h)
            if g_pv is not None:
                for h in run:
                    pvs[h] = pv(h, g_pv, p_ref)
        return tops, pvs

    def softmax(h, m, top, s_ref, p_ref):
        m_new = jnp.maximum(m, top)
        for c in range(key_blocks):
            p_ref[h][rows(c), :] = jnp.exp2((s_ref[h][rows(c), :] - m_new).astype(p_ref[h].dtype))
        return m_new, jnp.exp2(m - m_new)

    def step(g, carry, s_read, s_write, p_read, p_write, first=False, last=False):
        tops, pvs = matmuls(None if last else g + 1, s_write, None if first else g - 1, p_read)
        new = []
        for h in heads:
            m, acc, alpha, top = carry[h]
            m_new, alpha_new = softmax(h, m, top, s_read, p_write)
            new.append((m_new, acc if first else alpha * acc + pvs[h], alpha_new, tops[h]))
        return tuple(new)

    def even(g, carry, **kw):
        return step(g, carry, sa_ref, sb_ref, pb_ref, pa_ref, **kw)

    def odd(g, carry, **kw):
        return step(g, carry, sb_ref, sa_ref, pa_ref, pb_ref, **kw)

    tops, _ = matmuls(0, sa_ref, None, None)
    init = tuple((jnp.full((1, tq), NEG_BIG, jnp.float32), jnp.zeros((MLA_VT_ROWS, tq), jnp.float32),
                  jnp.ones((1, tq), jnp.float32), tops[h]) for h in heads)
    carry = even(0, init, first=True)

    def pairs(g, carry, count):
        for i in range(count):
            carry = even(g + 2 * i + 1, odd(g + 2 * i, carry))
        return carry

    n_pairs = (n_groups - 2) // 2
    trips = n_pairs // MLA_PAIRS_PER_ITER
    carry = lax.fori_loop(
        0, trips, lambda t, c: pairs(2 * MLA_PAIRS_PER_ITER * t + 1, c, MLA_PAIRS_PER_ITER), carry)
    carry = pairs(2 * MLA_PAIRS_PER_ITER * trips + 1, carry, n_pairs - trips * MLA_PAIRS_PER_ITER)
    carry = odd(n_groups - 1, carry, last=True)
    _, pvs = matmuls(None, None, n_groups - 1, pb_ref)
    accs = [alpha * acc + pvs[h] for h, (_, acc, alpha, _) in enumerate(carry)]
    outs = [a[:B_V] / a[B_V:B_V + 1] for a in accs]
    for t in range(MLA_Q_TILES_PER_STEP):
        tile_outs = outs[t * MLA_HEADS_PER_STEP:(t + 1) * MLA_HEADS_PER_STEP]
        o_ref[0, t * tq:(t + 1) * tq, :] = jnp.concatenate(tile_outs, axis=0).T.astype(o_ref.dtype)


def _mla_attention(qbt, kb, vbt):
    batch, tiles, _, tile = qbt.shape
    seq = kb.shape[1]
    n, qt = MLA_HEADS_PER_STEP, MLA_Q_TILES_PER_STEP
    return pl.pallas_call(
        functools.partial(_mla_kernel, n_groups=tiles),
        grid=(batch, B_HEADS // n, tiles // qt),
        in_specs=[
            pl.BlockSpec((1, qt, n * LANES, tile), lambda b, hp, i: (b, i, hp, 0)),
            pl.BlockSpec((1, seq, n * LANES), lambda b, hp, i: (b, 0, hp)),
            pl.BlockSpec((1, tiles, n * LANES, tile), lambda b, hp, i: (b, 0, hp, 0)),
        ],
        out_specs=pl.BlockSpec((1, qt * tile, n * B_V), lambda b, hp, i: (b, i, hp)),
        out_shape=jax.ShapeDtypeStruct((batch, seq, B_HEADS * B_V), jnp.bfloat16),
        scratch_shapes=([pltpu.VMEM((tile, tile), jnp.float32)] * (2 * n * qt)
                        + [pltpu.VMEM((tile, tile), jnp.bfloat16)] * (2 * n * qt)),
        compiler_params=pltpu.CompilerParams(dimension_semantics=("parallel", "parallel", "parallel"),
                                             vmem_limit_bytes=VMEM_LIMIT),
        name="mla_attn",
    )(qbt, kb, vbt)


def _mix_ffn_kernel(x_ref, oa_ref, ob_ref, gate_ref, woa_ref, wob_ref, wout_ref, gpost_ref, gpre_ref,
                    w1_ref, w2_ref, gmlp_ref, out_o, *, d_model):
    o_a = jnp.dot(oa_ref[...], woa_ref[...], preferred_element_type=jnp.float32)
    o_b = jnp.dot(ob_ref[...], wob_ref[...], preferred_element_type=jnp.float32)
    gates = gate_ref[...].astype(jnp.float32)
    mixin = (gates[:, :d_model] * o_a + gates[:, d_model:] * o_b).astype(jnp.bfloat16)
    mix = jnp.dot(mixin, wout_ref[...], preferred_element_type=jnp.float32)
    h = x_ref[...] + _rms(mix, gpost_ref[...])
    hn = _rms(h, gpre_ref[...]).astype(jnp.bfloat16)
    u = jnp.dot(hn, w1_ref[...], preferred_element_type=jnp.float32)
    u = jnp.square(jnp.maximum(u, 0.0)).astype(jnp.bfloat16)
    ff = jnp.dot(u, w2_ref[...], preferred_element_type=jnp.float32)
    out_o[...] = h + _rms(ff, gmlp_ref[...])


def _mix_ffn(x2, oa, ob, gates, woa, wob, wout, gpost, gpre, w1, w2, gmlp, *, tm):
    t, d_model = x2.shape
    d_ff = w1.shape[1]
    row = lambda i: (i, 0)
    whole = lambda shape: pl.BlockSpec(shape, lambda i: (0, 0), pipeline_mode=pl.Buffered(1))
    return pl.pallas_call(
        functools.partial(_mix_ffn_kernel, d_model=d_model),
        grid=(t // tm,),
        in_specs=[
            pl.BlockSpec((tm, d_model), row),
            pl.BlockSpec((tm, A_GROUP_WIDTH), row),
            pl.BlockSpec((tm, B_HEADS * B_V), row),
            pl.BlockSpec((tm, 2 * d_model), row),
            whole((A_GROUP_WIDTH, d_model)),
            whole((B_HEADS * B_V, d_model)),
            whole((d_model, d_model)),
            whole((1, d_model)),
            whole((1, d_model)),
            whole((d_model, d_ff)),
            whole((d_ff, d_model)),
            whole((1, d_model)),
        ],
        out_specs=pl.BlockSpec((tm, d_model), row),
        out_shape=jax.ShapeDtypeStruct((t, d_model), jnp.float32),
        compiler_params=pltpu.CompilerParams(dimension_semantics=("parallel",), vmem_limit_bytes=VMEM_LIMIT),
        name="mix_ffn",
    )(x2, oa, ob, gates, woa, wob, wout, gpost, gpre, w1, w2, gmlp)


def _rotary_tables(seq, theta, rot_dim, period, offset):
    half = rot_dim // 2
    inv = np.float64(np.float32(theta)) ** (-(np.arange(half, dtype=np.float64) * 2.0 / rot_dim))
    ang = np.arange(seq, dtype=np.float64)[:, None] * inv[None, :]
    cos, sin = np.cos(ang), np.sin(ang)
    cos_p = np.ones((seq, period))
    sin_p = np.zeros((seq, period))
    cos_p[:, offset:offset + rot_dim] = np.concatenate([cos, cos], 1)
    sin_p[:, offset:offset + rot_dim] = np.concatenate([-sin, sin], 1)
    reps = LANES // period
    return np.tile(cos_p, (1, reps)).astype(np.float32), np.tile(sin_p, (1, reps)).astype(np.float32)


def _pad_heads(w, heads, width):
    rows = w.shape[0]
    w = w.reshape(rows, heads, width)
    return jnp.pad(w, ((0, 0), (0, 0), (0, LANES - width))).reshape(rows, heads * LANES)


def kernel(x, norm_mix_pre, w_in, b_gate, mla_q_norm, mla_w_uq, mla_kv_norm, mla_w_ukv, w_o_a, w_o_b, w_out,
           norm_mix_post, norm_mlp_pre, w_ff1, w_ff2, norm_mlp_post):
    batch, seq, d_model = x.shape
    bf = jnp.bfloat16
    x2 = x.reshape(batch * seq, d_model)

    w_in = w_in.astype(bf)
    w_kpe = jnp.pad(w_in[:, _C_KPE:_C_KPE + B_ROPE], ((0, 0), (B_NOPE, LANES - B_QK)))
    w_gate = w_in[:, _C_KPE + B_ROPE:]
    wuqt = _pad_heads(mla_w_uq, B_HEADS, B_QK).T.astype(bf)
    wukv = mla_w_ukv.reshape(B_KV_RANK, B_HEADS, B_NOPE + B_V)
    wuk = _pad_heads(wukv[:, :, :B_NOPE].reshape(B_KV_RANK, B_HEADS * B_NOPE), B_HEADS, B_NOPE).astype(bf)
    wuvt = _pad_heads(wukv[:, :, B_NOPE:].reshape(B_KV_RANK, B_HEADS * B_V), B_HEADS, B_V).T.astype(bf)

    ca, sa = _rotary_tables(seq, A_ROPE_THETA, A_ROT_DIM, A_HEAD_DIM, 0)
    cb, sb = _rotary_tables(seq, B_ROPE_THETA, B_ROPE, LANES, B_NOPE)
    cbt, sbt = np.ascontiguousarray(cb.T), np.ascontiguousarray(sb.T)

    vec = lambda v: v.reshape(1, -1)
    qa, ka, va, qbt, kb, vbt, gates = _in_proj(
        x2, vec(norm_mix_pre), w_in, w_kpe, w_gate, vec(b_gate), vec(mla_q_norm), wuqt, vec(mla_kv_norm), wuk, wuvt,
        ca, sa, cb, sb, cbt, sbt, seq=seq, tm=512)

    shape_a = (batch, seq, A_WIDTH)
    oa = _dilated_attention(qa.reshape(shape_a), ka.reshape(shape_a), va.reshape(shape_a))
    oa = oa.reshape(batch * seq, A_GROUP_WIDTH)

    ob = _mla_attention(qbt, kb.reshape(batch, seq, B_WIDTH), vbt)
    ob = ob.reshape(batch * seq, B_HEADS * B_V)

    out = _mix_ffn(x2, oa, ob, gates, w_o_a.astype(bf), w_o_b.astype(bf), w_out.astype(bf),
                   vec(norm_mix_post), vec(norm_mlp_pre), w_ff1.astype(bf), w_ff2.astype(bf),
                   vec(norm_mlp_post), tm=512)
    return out.reshape(batch, seq, d_model)
```

```python
import functools

import jax
import jax.numpy as jnp
import numpy as np
from jax import lax
from jax.experimental import pallas as pl
from jax.experimental.pallas import tpu as pltpu

EPS = 1e-6
LANES = 128

A_HEAD_DIM = 64
A_HEADS_PER_GROUP = 4
A_GROUPS = ((128, 1), (512, 4), (2048, 16))
A_HEADS = A_HEADS_PER_GROUP * len(A_GROUPS)
A_WIDTH = A_HEADS * A_HEAD_DIM
A_GROUP_WIDTH = A_HEADS_PER_GROUP * A_HEAD_DIM
A_ROT_DIM = A_HEAD_DIM // 4
A_ROPE_THETA = 500000.0
A_TILE = 128
A_TILES_PER_ITER = 4
A_GROUP_ORDER = (2, 0, 1)
A_DATA_INTERLEAVE = 4
B_HEADS = 8
B_NOPE = 64
B_ROPE = 32
B_QK = B_NOPE + B_ROPE
B_V = 64
B_Q_RANK = 512
B_KV_RANK = 256
B_ROPE_THETA = 10000.0
B_WIDTH = B_HEADS * LANES
MLA_KEY_BLOCK = 256
MLA_PAIRS_PER_ITER = 1
MLA_HEADS_PER_STEP = 2
MLA_Q_TILES_PER_STEP = 4
MLA_SCORE_STREAMS_RUN = 2
MLA_VT_ROWS = B_V + 16

NEG_BIG = -1e30
LOG2_E = 1.4426950408889634
VMEM_LIMIT = 56 * 1024 * 1024


def _rms(x, g):
    return x * lax.rsqrt(jnp.mean(x * x, axis=-1, keepdims=True) + EPS) * g


def _rotate_block(xb, cos, sin_signed, first_half, half):
    up = pltpu.roll(xb, LANES - half, 1)
    down = pltpu.roll(xb, half, 1)
    return xb * cos + jnp.where(first_half, up, down) * sin_signed


_C_QA, _C_KA, _C_VA = 0, A_WIDTH, 2 * A_WIDTH
_C_QC = 3 * A_WIDTH
_C_KVC = _C_QC + B_Q_RANK
_C_KPE = _C_KVC + B_KV_RANK


_NT_DIMS = (((1,), (1,)), ((), ()))


def _in_proj_kernel(x_ref, g_ref, w_ref, wkpe_ref, wgate_ref, bg_ref, qn_ref, wuqt_ref, kvn_ref, wuk_ref, wuvt_ref,
                    ca_ref, sa_ref, cb_ref, sb_ref, cbt_ref, sbt_ref,
                    qa_o, ka_o, va_o, qbt_o, kb_o, vbt_o, gate_o, *, d_model):
    xn = _rms(x_ref[...], g_ref[...]).astype(jnp.bfloat16)

    def proj(lo, hi):
        return jnp.dot(xn, w_ref[:, lo:hi], preferred_element_type=jnp.float32)

    lane = lax.broadcasted_iota(jnp.int32, (1, LANES), 1)
    first_a = (lane % A_HEAD_DIM) < (A_ROT_DIM // 2)
    first_b = lane < (B_NOPE + B_ROPE // 2)
    ca, sa, cb, sb = ca_ref[...], sa_ref[...], cb_ref[...], sb_ref[...]

    qcn = _rms(proj(_C_QC, _C_QC + B_Q_RANK), qn_ref[...]).astype(jnp.bfloat16)
    kvn = _rms(proj(_C_KVC, _C_KVC + B_KV_RANK), kvn_ref[...]).astype(jnp.bfloat16)
    kpe = jnp.dot(xn, wkpe_ref[...], preferred_element_type=jnp.float32)
    kpe = _rotate_block(kpe, cb, sb, first_b, B_ROPE // 2)

    z = jnp.dot(xn, wgate_ref[...], preferred_element_type=jnp.float32) + bg_ref[...]
    gate_o[...] = (0.5 * jnp.tanh(0.5 * z) + 0.5).astype(gate_o.dtype)

    qa = proj(_C_QA, _C_QA + A_WIDTH)
    ka = proj(_C_KA, _C_KA + A_WIDTH)
    a_scale = A_HEAD_DIM ** -0.5 * LOG2_E
    for j in range(A_WIDTH // LANES):
        blk = slice(j * LANES, (j + 1) * LANES)
        qa_o[:, blk] = (_rotate_block(qa[:, blk], ca, sa, first_a, A_ROT_DIM // 2) * a_scale).astype(qa_o.dtype)
        ka_o[:, blk] = _rotate_block(ka[:, blk], ca, sa, first_a, A_ROT_DIM // 2).astype(ka_o.dtype)
    va_o[...] = proj(_C_VA, _C_VA + A_WIDTH).astype(va_o.dtype)

    qbt = lax.dot_general(wuqt_ref[...], qcn, _NT_DIMS, preferred_element_type=jnp.float32)
    vbt = lax.dot_general(wuvt_ref[...], kvn, _NT_DIMS, preferred_element_type=jnp.float32)
    kb = jnp.dot(kvn, wuk_ref[...], preferred_element_type=jnp.float32)
    b_scale = B_QK ** -0.5 * LOG2_E
    cbt, sbt = cbt_ref[...], sbt_ref[...]
    row = lax.broadcasted_iota(jnp.int32, (LANES, 1), 0)
    r1, r2 = B_NOPE + B_ROPE // 2, B_QK
    for h in range(B_HEADS):
        blk = slice(h * LANES, (h + 1) * LANES)
        xt = qbt[blk, :]
        partner = jnp.concatenate([xt[:B_NOPE], xt[r1:r2], xt[B_NOPE:r1], xt[r2:]], axis=0)
        qbt_o[0, 0, blk, :] = ((xt * cbt + partner * sbt) * b_scale).astype(qbt_o.dtype)
        kb_o[:, blk] = (kb[:, blk] + kpe).astype(kb_o.dtype)
        vbt_o[0, 0, blk, :] = jnp.where(row < B_V, vbt[blk, :], 1.0).astype(vbt_o.dtype)


def _in_proj(x2, g, w_in, w_kpe, w_gate, bg, qn, wuqt, kvn, wuk, wuvt, ca, sa, cb, sb, cbt, sbt, *, seq, tm):
    t, d_model = x2.shape
    steps_per_seq = seq // tm
    row = lambda i: (i, 0)
    const = lambda i: (0, 0)
    pos = lambda i: (i % steps_per_seq, 0)
    pos_t = lambda i: (0, i % steps_per_seq)
    tile_t = lambda i: (i // steps_per_seq, i % steps_per_seq, 0, 0)
    bf = jnp.bfloat16
    transposed = jax.ShapeDtypeStruct((t // seq, steps_per_seq, B_WIDTH, tm), bf)
    out_shape = (
        jax.ShapeDtypeStruct((t, A_WIDTH), bf), jax.ShapeDtypeStruct((t, A_WIDTH), bf),
        jax.ShapeDtypeStruct((t, A_WIDTH), bf),
        transposed, jax.ShapeDtypeStruct((t, B_WIDTH), bf), transposed,
        jax.ShapeDtypeStruct((t, 2 * d_model), bf),
    )
    return pl.pallas_call(
        functools.partial(_in_proj_kernel, d_model=d_model),
        grid=(t // tm,),
        in_specs=[
            pl.BlockSpec((tm, d_model), row),
            pl.BlockSpec((1, d_model), const),
            pl.BlockSpec((d_model, _C_KPE), const),
            pl.BlockSpec((d_model, LANES), const),
            pl.BlockSpec((d_model, 2 * d_model), const),
            pl.BlockSpec((1, 2 * d_model), const),
            pl.BlockSpec((1, B_Q_RANK), const),
            pl.BlockSpec((B_WIDTH, B_Q_RANK), const),
            pl.BlockSpec((1, B_KV_RANK), const),
            pl.BlockSpec((B_KV_RANK, B_WIDTH), const),
            pl.BlockSpec((B_WIDTH, B_KV_RANK), const),
            pl.BlockSpec((tm, LANES), pos), pl.BlockSpec((tm, LANES), pos),
            pl.BlockSpec((tm, LANES), pos), pl.BlockSpec((tm, LANES), pos),
            pl.BlockSpec((LANES, tm), pos_t), pl.BlockSpec((LANES, tm), pos_t),
        ],
        out_specs=(
            pl.BlockSpec((tm, A_WIDTH), row), pl.BlockSpec((tm, A_WIDTH), row),
            pl.BlockSpec((tm, A_WIDTH), row),
            pl.BlockSpec((1, 1, B_WIDTH, tm), tile_t), pl.BlockSpec((tm, B_WIDTH), row),
            pl.BlockSpec((1, 1, B_WIDTH, tm), tile_t),
            pl.BlockSpec((tm, 2 * d_model), row),
        ),
        out_shape=out_shape,
        compiler_params=pltpu.CompilerParams(dimension_semantics=("parallel",), vmem_limit_bytes=VMEM_LIMIT),
        name="in_proj",
    )(x2, g, w_in, w_kpe, w_gate, bg, qn, wuqt, kvn, wuk, wuvt, ca, sa, cb, sb, cbt, sbt)


def _dilated_group(qf, kf, vf, m_sc, l_sc, acc_sc, *, seq, window, dilation, first, interleave=1):
    length = seq // dilation
    radius = window // (2 * dilation)
    win = A_TILE + 2 * radius
    tiles = length // A_TILE
    assert length % A_TILE == 0 and tiles % A_TILES_PER_ITER == 0 and win <= length
    iters_per_residue = tiles // A_TILES_PER_ITER
    lane = lax.broadcasted_iota(jnp.int32, (1, LANES), 1)
    head0 = lane < A_HEAD_DIM
    delta = (lax.broadcasted_iota(jnp.int32, (2 * A_TILE, win), 0) % A_TILE
             - lax.broadcasted_iota(jnp.int32, (2 * A_TILE, win), 1))

    def strided(start, size):
        if dilation == 1:
            return pl.ds(pl.multiple_of(start, radius), size)
        return pl.ds(start, size, stride=dilation)

    def data_rows(r, pos, size):
        if interleave == 1:
            return strided(r + dilation * pos, size)
        step = dilation // interleave
        return pl.ds((r % interleave) * (seq // interleave) + r // interleave + step * pos, size, stride=step)

    def body(n, carry):
        r = n // iters_per_residue
        i0 = (n % iters_per_residue) * A_TILES_PER_ITER
        l0s = [(i0 + t) * A_TILE for t in range(A_TILES_PER_ITER)]
        wss = [jnp.clip(l0 - radius, 0, length - win) for l0 in l0s]
        ss = []
        for l0, ws in zip(l0s, wss):
            q = qf[data_rows(r, l0, A_TILE), :]
            k = kf[data_rows(r, ws, win), :].astype(jnp.bfloat16)
            q2 = jnp.concatenate([jnp.where(head0, q, 0), jnp.where(head0, 0, q)], axis=0)
            ss.append(lax.dot_general(q2.astype(jnp.bfloat16), k, _NT_DIMS, preferred_element_type=jnp.float32))
        stats = []
        for l0, ws, s in zip(l0s, wss, ss):
            dist = delta + (l0 - ws)
            s = jnp.where((dist <= radius) & (dist >= -radius), s, NEG_BIG)
            m = jnp.max(s, axis=-1, keepdims=True)
            p = jnp.exp2(s - m)
            stats.append((m, jnp.sum(p, axis=-1, keepdims=True), p.astype(jnp.bfloat16)))
        for l0, ws, (m, l, p) in zip(l0s, wss, stats):
            v = vf[data_rows(r, ws, win), :].astype(jnp.bfloat16)
            pv = jnp.dot(p, v, preferred_element_type=jnp.float32)
            acc_t = jnp.where(head0, pv[:A_TILE], pv[A_TILE:])
            m_t = jnp.where(head0, m[:A_TILE], m[A_TILE:])
            l_t = jnp.where(head0, l[:A_TILE], l[A_TILE:])
            rows = strided(r + dilation * l0, A_TILE)
            if first:
                m_sc[rows, :] = m_t
                l_sc[rows, :] = l_t
                acc_sc[rows, :] = acc_t
            else:
                m_old = m_sc[rows, :]
                m_new = jnp.maximum(m_old, m_t)
                a = jnp.exp2(m_old - m_new)
                b = jnp.exp2(m_t - m_new)
                m_sc[rows, :] = m_new
                l_sc[rows, :] = a * l_sc[rows, :] + b * l_t
                acc_sc[rows, :] = a * acc_sc[rows, :] + b * acc_t
        return carry

    lax.fori_loop(0, dilation * iters_per_residue, body, 0)


def _dilated_kernel(q_ref, k_ref, v_ref, o_ref, qf, kf, vf, m_sc, l_sc, acc_sc, tmp, *, seq):
    turn = pl.program_id(2)
    chunk = 512

    def widen(i, carry, interleave=1):
        rows = pl.ds(pl.multiple_of(i * chunk, chunk), chunk)
        for src, dst in ((q_ref, qf), (k_ref, kf), (v_ref, vf)):
            if interleave == 1:
                dst[rows, :] = src[0, rows, :].astype(jnp.float32)
            else:
                tmp[...] = src[0, rows, :].astype(jnp.float32)
                part = chunk // interleave
                for c in range(interleave):
                    out = pl.ds(pl.multiple_of(c * (seq // interleave) + i * part, part), part)
                    dst[out, :] = tmp[pl.ds(c, part, stride=interleave), :]
        return carry

    for t, gi in enumerate(A_GROUP_ORDER):
        window, dilation = A_GROUPS[gi]

        @pl.when(turn == t)
        def _(window=window, dilation=dilation, t=t):
            interleave = A_DATA_INTERLEAVE if dilation % (4 * A_DATA_INTERLEAVE) == 0 else 1
            if dilation == 1:
                srcs = (q_ref.at[0], k_ref.at[0], v_ref.at[0])
            else:
                lax.fori_loop(0, seq // chunk, functools.partial(widen, interleave=interleave), 0)
                srcs = (qf, kf, vf)
            _dilated_group(*srcs, m_sc, l_sc, acc_sc, seq=seq, window=window, dilation=dilation, first=t == 0,
                           interleave=interleave)

    @pl.when(turn == len(A_GROUPS) - 1)
    def _():
        def finish(i, carry):
            rows = pl.ds(pl.multiple_of(i * chunk, chunk), chunk)
            o_ref[0, rows, :] = (acc_sc[rows, :] / l_sc[rows, :]).astype(o_ref.dtype)
            return carry

        lax.fori_loop(0, seq // chunk, finish, 0)


def _dilated_attention(qa, ka, va):
    batch, seq, _ = qa.shape
    pairs = A_GROUP_WIDTH // LANES
    first, n = A_GROUP_ORDER[0], len(A_GROUPS)
    assert A_GROUP_ORDER == tuple((first + t) % n for t in range(n))
    in_spec = pl.BlockSpec((1, seq, LANES), lambda b, hp, t: (b, 0, ((first + t) % n) * pairs + hp))
    state = pltpu.VMEM((seq, LANES), jnp.float32)
    return pl.pallas_call(
        functools.partial(_dilated_kernel, seq=seq),
        grid=(batch, pairs, len(A_GROUPS)),
        in_specs=[in_spec, in_spec, in_spec],
        out_specs=pl.BlockSpec((1, seq, LANES), lambda b, hp, g: (b, 0, hp)),
        out_shape=jax.ShapeDtypeStruct((batch, seq, A_GROUP_WIDTH), jnp.bfloat16),
        scratch_shapes=[state] * 6 + [pltpu.VMEM((512, LANES), jnp.float32)],
        compiler_params=pltpu.CompilerParams(dimension_semantics=("parallel", "parallel", "arbitrary"),
                                             vmem_limit_bytes=VMEM_LIMIT),
        name="dilated",
    )(qa, ka, va)


def _mla_kernel(qt_ref, k_ref, vt_ref, o_ref, *scratch, n_groups):
    tq = qt_ref.shape[3]
    tk = vt_ref.shape[3]
    n_streams = MLA_HEADS_PER_STEP * MLA_Q_TILES_PER_STEP
    sa_ref, sb_ref, pa_ref, pb_ref = (scratch[i * n_streams:(i + 1) * n_streams] for i in range(4))
    key_blocks = tk // MLA_KEY_BLOCK
    lane = lax.broadcasted_iota(jnp.int32, (1, LANES), 1)
    heads = range(n_streams)
    blks = [slice((st % MLA_HEADS_PER_STEP) * LANES, (st % MLA_HEADS_PER_STEP + 1) * LANES) for st in heads]
    qts = [qt_ref[0, st // MLA_HEADS_PER_STEP, blks[st], :] for st in heads]

    def rows(c):
        return slice(c * MLA_KEY_BLOCK, (c + 1) * MLA_KEY_BLOCK)

    def qk_block(h, g, c, s_ref):
        base = g * tk if isinstance(g, int) else pl.multiple_of(g * tk, tk)
        k = k_ref[0, pl.ds(base + c * MLA_KEY_BLOCK, MLA_KEY_BLOCK), blks[h]]
        s = jnp.dot(k, qts[h], preferred_element_type=jnp.float32)
        s_ref[h][rows(c), :] = s
        return jnp.max(s, axis=0, keepdims=True)

    def pv(h, g, p_ref):
        vt = vt_ref[0, g, blks[h].start:blks[h].start + MLA_VT_ROWS, :]
        return jnp.dot(vt, p_ref[h][...], preferred_element_type=jnp.float32)

    def matmuls(g_qk, s_ref, g_pv, p_ref):
        tops, pvs = [None for _ in heads], [None for _ in heads]

        def qk(h):
            for c in range(key_blocks if g_qk is not None else 0):
                top = qk_block(h, g_qk, c, s_ref)
                tops[h] = top if tops[h] is None else jnp.maximum(tops[h], top)

        for lo in range(0, n_streams, MLA_SCORE_STREAMS_RUN):
            run = heads[lo:lo + MLA_SCORE_STREAMS_RUN]
            for h in run:
                qk(h)
            if g_pv is not None:
                for h in run:
                    pvs[h] = pv(h, g_pv, p_ref)
        return tops, pvs

    def softmax(h, m, top, s_ref, p_ref):
        m_new = jnp.maximum(m, top)
        for c in range(key_blocks):
            p_ref[h][rows(c), :] = jnp.exp2((s_ref[h][rows(c), :] - m_new).astype(p_ref[h].dtype))
        return m_new, jnp.exp2(m - m_new)

    def step(g, carry, s_read, s_write, p_read, p_write, first=False, last=False):
        tops, pvs = matmuls(None if last else g + 1, s_write, None if first else g - 1, p_read)
        new = []
        for h in heads:
            m, acc, alpha, top = carry[h]
            m_new, alpha_new = softmax(h, m, top, s_read, p_write)
            new.append((m_new, acc if first else alpha * acc + pvs[h], alpha_new, tops[h]))
        return tuple(new)

    def even(g, carry, **kw):
        return step(g, carry, sa_ref, sb_ref, pb_ref, pa_ref, **kw)

    def odd(g, carry, **kw):
        return step(g, carry, sb_ref, sa_ref, pa_ref, pb_ref, **kw)

    tops, _ = matmuls(0, sa_ref, None, None)
    init = tuple((jnp.full((1, tq), NEG_BIG, jnp.float32), jnp.zeros((MLA_VT_ROWS, tq), jnp.float32),
                  jnp.ones((1, tq), jnp.float32), tops[h]) for h in heads)
    carry = even(0, init, first=True)

    def pairs(g, carry, count):
        for i in range(count):
            carry = even(g + 2 * i + 1, odd(g + 2 * i, carry))
        return carry

    n_pairs = (n_groups - 2) // 2
    trips = n_pairs // MLA_PAIRS_PER_ITER
    carry = lax.fori_loop(
        0, trips, lambda t, c: pairs(2 * MLA_PAIRS_PER_ITER * t + 1, c, MLA_PAIRS_PER_ITER), carry)
    carry = pairs(2 * MLA_PAIRS_PER_ITER * trips + 1, carry, n_pairs - trips * MLA_PAIRS_PER_ITER)
    carry = odd(n_groups - 1, carry, last=True)
    _, pvs = matmuls(None, None, n_groups - 1, pb_ref)
    accs = [alpha * acc + pvs[h] for h, (_, acc, alpha, _) in enumerate(carry)]
    outs = [a[:B_V] / a[B_V:B_V + 1] for a in accs]
    for t in range(MLA_Q_TILES_PER_STEP):
        tile_outs = outs[t * MLA_HEADS_PER_STEP:(t + 1) * MLA_HEADS_PER_STEP]
        o_ref[0, t * tq:(t + 1) * tq, :] = jnp.concatenate(tile_outs, axis=0).T.astype(o_ref.dtype)


def _mla_attention(qbt, kb, vbt):
    batch, tiles, _, tile = qbt.shape
    seq = kb.shape[1]
    n, qt = MLA_HEADS_PER_STEP, MLA_Q_TILES_PER_STEP
    return pl.pallas_call(
        functools.partial(_mla_kernel, n_groups=tiles),
        grid=(batch, B_HEADS // n, tiles // qt),
        in_specs=[
            pl.BlockSpec((1, qt, n * LANES, tile), lambda b, hp, i: (b, i, hp, 0)),
            pl.BlockSpec((1, seq, n * LANES), lambda b, hp, i: (b, 0, hp)),
            pl.BlockSpec((1, tiles, n * LANES, tile), lambda b, hp, i: (b, 0, hp, 0)),
        ],
        out_specs=pl.BlockSpec((1, qt * tile, n * B_V), lambda b, hp, i: (b, i, hp)),
        out_shape=jax.ShapeDtypeStruct((batch, seq, B_HEADS * B_V), jnp.bfloat16),
        scratch_shapes=([pltpu.VMEM((tile, tile), jnp.float32)] * (2 * n * qt)
                        + [pltpu.VMEM((tile, tile), jnp.bfloat16)] * (2 * n * qt)),
        compiler_params=pltpu.CompilerParams(dimension_semantics=("parallel", "parallel", "parallel"),
                                             vmem_limit_bytes=VMEM_LIMIT),
        name="mla_attn",
    )(qbt, kb, vbt)


def _mix_ffn_kernel(x_ref, oa_ref, ob_ref, gate_ref, woa_ref, wob_ref, wout_ref, gpost_ref, gpre_ref,
                    w1_ref, w2_ref, gmlp_ref, out_o, *, d_model):
    o_a = jnp.dot(oa_ref[...], woa_ref[...], preferred_element_type=jnp.float32)
    o_b = jnp.dot(ob_ref[...], wob_ref[...], preferred_element_type=jnp.float32)
    gates = gate_ref[...].astype(jnp.float32)
    mixin = (gates[:, :d_model] * o_a + gates[:, d_model:] * o_b).astype(jnp.bfloat16)
    mix = jnp.dot(mixin, wout_ref[...], preferred_element_type=jnp.float32)
    h = x_ref[...] + _rms(mix, gpost_ref[...])
    hn = _rms(h, gpre_ref[...]).astype(jnp.bfloat16)
    u = jnp.dot(hn, w1_ref[...], preferred_element_type=jnp.float32)
    u = jnp.square(jnp.maximum(u, 0.0)).astype(jnp.bfloat16)
    ff = jnp.dot(u, w2_ref[...], preferred_element_type=jnp.float32)
    out_o[...] = h + _rms(ff, gmlp_ref[...])


def _mix_ffn(x2, oa, ob, gates, woa, wob, wout, gpost, gpre, w1, w2, gmlp, *, tm):
    t, d_model = x2.shape
    d_ff = w1.shape[1]
    row = lambda i: (i, 0)
    whole = lambda shape: pl.BlockSpec(shape, lambda i: (0, 0), pipeline_mode=pl.Buffered(1))
    return pl.pallas_call(
        functools.partial(_mix_ffn_kernel, d_model=d_model),
        grid=(t // tm,),
        in_specs=[
            pl.BlockSpec((tm, d_model), row),
            pl.BlockSpec((tm, A_GROUP_WIDTH), row),
            pl.BlockSpec((tm, B_HEADS * B_V), row),
            pl.BlockSpec((tm, 2 * d_model), row),
            whole((A_GROUP_WIDTH, d_model)),
            whole((B_HEADS * B_V, d_model)),
            whole((d_model, d_model)),
            whole((1, d_model)),
            whole((1, d_model)),
            whole((d_model, d_ff)),
            whole((d_ff, d_model)),
            whole((1, d_model)),
        ],
        out_specs=pl.BlockSpec((tm, d_model), row),
        out_shape=jax.ShapeDtypeStruct((t, d_model), jnp.float32),
        compiler_params=pltpu.CompilerParams(dimension_semantics=("parallel",), vmem_limit_bytes=VMEM_LIMIT),
        name="mix_ffn",
    )(x2, oa, ob, gates, woa, wob, wout, gpost, gpre, w1, w2, gmlp)


def _rotary_tables(seq, theta, rot_dim, period, offset):
    half = rot_dim // 2
    inv = np.float64(np.float32(theta)) ** (-(np.arange(half, dtype=np.float64) * 2.0 / rot_dim))
    ang = np.arange(seq, dtype=np.float64)[:, None] * inv[None, :]
    cos, sin = np.cos(ang), np.sin(ang)
    cos_p = np.ones((seq, period))
    sin_p = np.zeros((seq, period))
    cos_p[:, offset:offset + rot_dim] = np.concatenate([cos, cos], 1)
    sin_p[:, offset:offset + rot_dim] = np.concatenate([-sin, sin], 1)
    reps = LANES // period
    return np.tile(cos_p, (1, reps)).astype(np.float32), np.tile(sin_p, (1, reps)).astype(np.float32)


def _pad_heads(w, heads, width):
    rows = w.shape[0]
    w = w.reshape(rows, heads, width)
    return jnp.pad(w, ((0, 0), (0, 0), (0, LANES - width))).reshape(rows, heads * LANES)


def kernel(x, norm_mix_pre, w_in, b_gate, mla_q_norm, mla_w_uq, mla_kv_norm, mla_w_ukv, w_o_a, w_o_b, w_out,
           norm_mix_post, norm_mlp_pre, w_ff1, w_ff2, norm_mlp_post):
    batch, seq, d_model = x.shape
    bf = jnp.bfloat16
    x2 = x.reshape(batch * seq, d_model)

    w_in = w_in.astype(bf)
    w_kpe = jnp.pad(w_in[:, _C_KPE:_C_KPE + B_ROPE], ((0, 0), (B_NOPE, LANES - B_QK)))
    w_gate = w_in[:, _C_KPE + B_ROPE:]
    wuqt = _pad_heads(mla_w_uq, B_HEADS, B_QK).T.astype(bf)
    wukv = mla_w_ukv.reshape(B_KV_RANK, B_HEADS, B_NOPE + B_V)
    wuk = _pad_heads(wukv[:, :, :B_NOPE].reshape(B_KV_RANK, B_HEADS * B_NOPE), B_HEADS, B_NOPE).astype(bf)
    wuvt = _pad_heads(wukv[:, :, B_NOPE:].reshape(B_KV_RANK, B_HEADS * B_V), B_HEADS, B_V).T.astype(bf)

    ca, sa = _rotary_tables(seq, A_ROPE_THETA, A_ROT_DIM, A_HEAD_DIM, 0)
    cb, sb = _rotary_tables(seq, B_ROPE_THETA, B_ROPE, LANES, B_NOPE)
    cbt, sbt = np.ascontiguousarray(cb.T), np.ascontiguousarray(sb.T)

    vec = lambda v: v.reshape(1, -1)
    qa, ka, va, qbt, kb, vbt, gates = _in_proj(
        x2, vec(norm_mix_pre), w_in, w_kpe, w_gate, vec(b_gate), vec(mla_q_norm), wuqt, vec(mla_kv_norm), wuk, wuvt,
        ca, sa, cb, sb, cbt, sbt, seq=seq, tm=512)

    shape_a = (batch, seq, A_WIDTH)
    oa = _dilated_attention(qa.reshape(shape_a), ka.reshape(shape_a), va.reshape(shape_a))
    oa = oa.reshape(batch * seq, A_GROUP_WIDTH)

    ob = _mla_attention(qbt, kb.reshape(batch, seq, B_WIDTH), vbt)
    ob = ob.reshape(batch * seq, B_HEADS * B_V)

    out = _mix_ffn(x2, oa, ob, gates, w_o_a.astype(bf), w_o_b.astype(bf), w_out.astype(bf),
                   vec(norm_mix_post), vec(norm_mlp_pre), w_ff1.astype(bf), w_ff2.astype(bf),
                   vec(norm_mlp_post), tm=512)
    return out.reshape(batch, seq, d_model)
```

```python
import functools

import jax
import jax.numpy as jnp
import numpy as np
from jax import lax
from jax.experimental import pallas as pl
from jax.experimental.pallas import tpu as pltpu

EPS = 1e-6
LANES = 128

A_HEAD_DIM = 64
A_HEADS_PER_GROUP = 4
A_GROUPS = ((128, 1), (512, 4), (2048, 16))
A_HEADS = A_HEADS_PER_GROUP * len(A_GROUPS)
A_WIDTH = A_HEADS * A_HEAD_DIM
A_GROUP_WIDTH = A_HEADS_PER_GROUP * A_HEAD_DIM
A_ROT_DIM = A_HEAD_DIM // 4
A_ROPE_THETA = 500000.0
A_TILE = 128
A_TILES_PER_ITER = 4
A_GROUP_ORDER = (2, 0, 1)
A_DATA_INTERLEAVE = 4
B_HEADS = 8
B_NOPE = 64
B_ROPE = 32
B_QK = B_NOPE + B_ROPE
B_V = 64
B_Q_RANK = 512
B_KV_RANK = 256
B_ROPE_THETA = 10000.0
B_WIDTH = B_HEADS * LANES
MLA_KEY_BLOCK = 256
MLA_PAIRS_PER_ITER = 1
MLA_HEADS_PER_STEP = 2
MLA_Q_TILES_PER_STEP = 4
MLA_SCORE_STREAMS_RUN = 2
MLA_VT_ROWS = B_V + 16

TOKEN_TILE = 512
NEG_BIG = -1e30
LOG2_E = 1.4426950408889634
VMEM_LIMIT = 56 * 1024 * 1024


def _rms(x, g):
    return x * lax.rsqrt(jnp.mean(x * x, axis=-1, keepdims=True) + EPS) * g


def _rotate_block(xb, cos, sin_signed, first_half, half):
    up = pltpu.roll(xb, LANES - half, 1)
    down = pltpu.roll(xb, half, 1)
    return xb * cos + jnp.where(first_half, up, down) * sin_signed


_C_QA, _C_KA, _C_VA = 0, A_WIDTH, 2 * A_WIDTH
_C_QC = 3 * A_WIDTH
_C_KVC = _C_QC + B_Q_RANK
_C_KPE = _C_KVC + B_KV_RANK


_NT_DIMS = (((1,), (1,)), ((), ()))


def _in_proj_kernel(x_ref, g_ref, w_ref, wkpe_ref, wgate_ref, bg_ref, qn_ref, wuqt_ref, kvn_ref, wuk_ref, wuvt_ref,
                    ca_ref, sa_ref, cb_ref, sb_ref, cbt_ref, sbt_ref,
                    qa_o, ka_o, va_o, qbt_o, kb_o, vbt_o, gate_o, *, d_model):
    xn = _rms(x_ref[...], g_ref[...]).astype(jnp.bfloat16)

    def proj(lo, hi):
        return jnp.dot(xn, w_ref[:, lo:hi], preferred_element_type=jnp.float32)

    lane = lax.broadcasted_iota(jnp.int32, (1, LANES), 1)
    first_a = (lane % A_HEAD_DIM) < (A_ROT_DIM // 2)
    first_b = lane < (B_NOPE + B_ROPE // 2)
    ca, sa, cb, sb = ca_ref[...], sa_ref[...], cb_ref[...], sb_ref[...]

    qcn = _rms(proj(_C_QC, _C_QC + B_Q_RANK), qn_ref[...]).astype(jnp.bfloat16)
    kvn = _rms(proj(_C_KVC, _C_KVC + B_KV_RANK), kvn_ref[...]).astype(jnp.bfloat16)
    kpe = jnp.dot(xn, wkpe_ref[...], preferred_element_type=jnp.float32)
    kpe = _rotate_block(kpe, cb, sb, first_b, B_ROPE // 2)

    z = jnp.dot(xn, wgate_ref[...], preferred_element_type=jnp.float32) + bg_ref[...]
    gate_o[...] = (0.5 * jnp.tanh(0.5 * z) + 0.5).astype(gate_o.dtype)

    qa = proj(_C_QA, _C_QA + A_WIDTH)
    ka = proj(_C_KA, _C_KA + A_WIDTH)
    a_scale = A_HEAD_DIM ** -0.5 * LOG2_E
    for j in range(A_WIDTH // LANES):
        blk = slice(j * LANES, (j + 1) * LANES)
        qa_o[:, blk] = (_rotate_block(qa[:, blk], ca, sa, first_a, A_ROT_DIM // 2) * a_scale).astype(qa_o.dtype)
        ka_o[:, blk] = _rotate_block(ka[:, blk], ca, sa, first_a, A_ROT_DIM // 2).astype(ka_o.dtype)
    va_o[...] = proj(_C_VA, _C_VA + A_WIDTH).astype(va_o.dtype)

    qbt = lax.dot_general(wuqt_ref[...], qcn, _NT_DIMS, preferred_element_type=jnp.float32)
    vbt = lax.dot_general(wuvt_ref[...], kvn, _NT_DIMS, preferred_element_type=jnp.float32)
    kb = jnp.dot(kvn, wuk_ref[...], preferred_element_type=jnp.float32)
    b_scale = B_QK ** -0.5 * LOG2_E
    cbt, sbt = cbt_ref[...], sbt_ref[...]
    row = lax.broadcasted_iota(jnp.int32, (LANES, 1), 0)
    r1, r2 = B_NOPE + B_ROPE // 2, B_QK
    for h in range(B_HEADS):
        blk = slice(h * LANES, (h + 1) * LANES)
        xt = qbt[blk, :]
        partner = jnp.concatenate([xt[:B_NOPE], xt[r1:r2], xt[B_NOPE:r1], xt[r2:]], axis=0)
        qbt_o[0, 0, blk, :] = ((xt * cbt + partner * sbt) * b_scale).astype(qbt_o.dtype)
        kb_o[:, blk] = (kb[:, blk] + kpe).astype(kb_o.dtype)
        vbt_o[0, 0, blk, :] = jnp.where(row < B_V, vbt[blk, :], 1.0).astype(vbt_o.dtype)


def _in_proj(x2, g, w_main, w_kpe, w_gate, bg, qn, wuqt, kvn, wuk, wuvt, ca, sa, cb, sb, cbt, sbt, *, seq, tm):
    t, d_model = x2.shape
    steps_per_seq = seq // tm
    row = lambda i: (i, 0)
    const = lambda i: (0, 0)
    pos = lambda i: (i % steps_per_seq, 0)
    pos_t = lambda i: (0, i % steps_per_seq)
    tile_t = lambda i: (i // steps_per_seq, i % steps_per_seq, 0, 0)
    bf = jnp.bfloat16
    transposed = jax.ShapeDtypeStruct((t // seq, steps_per_seq, B_WIDTH, tm), bf)
    out_shape = (
        jax.ShapeDtypeStruct((t, A_WIDTH), bf), jax.ShapeDtypeStruct((t, A_WIDTH), bf),
        jax.ShapeDtypeStruct((t, A_WIDTH), bf),
        transposed, jax.ShapeDtypeStruct((t, B_WIDTH), bf), transposed,
        jax.ShapeDtypeStruct((t, 2 * d_model), bf),
    )
    return pl.pallas_call(
        functools.partial(_in_proj_kernel, d_model=d_model),
        grid=(t // tm,),
        in_specs=[
            pl.BlockSpec((tm, d_model), row),
            pl.BlockSpec((1, d_model), const),
            pl.BlockSpec((d_model, _C_KPE), const),
            pl.BlockSpec((d_model, LANES), const),
            pl.BlockSpec((d_model, 2 * d_model), const),
            pl.BlockSpec((1, 2 * d_model), const),
            pl.BlockSpec((1, B_Q_RANK), const),
            pl.BlockSpec((B_WIDTH, B_Q_RANK), const),
            pl.BlockSpec((1, B_KV_RANK), const),
            pl.BlockSpec((B_KV_RANK, B_WIDTH), const),
            pl.BlockSpec((B_WIDTH, B_KV_RANK), const),
            pl.BlockSpec((tm, LANES), pos), pl.BlockSpec((tm, LANES), pos),
            pl.BlockSpec((tm, LANES), pos), pl.BlockSpec((tm, LANES), pos),
            pl.BlockSpec((LANES, tm), pos_t), pl.BlockSpec((LANES, tm), pos_t),
        ],
        out_specs=(
            pl.BlockSpec((tm, A_WIDTH), row), pl.BlockSpec((tm, A_WIDTH), row),
            pl.BlockSpec((tm, A_WIDTH), row),
            pl.BlockSpec((1, 1, B_WIDTH, tm), tile_t), pl.BlockSpec((tm, B_WIDTH), row),
            pl.BlockSpec((1, 1, B_WIDTH, tm), tile_t),
            pl.BlockSpec((tm, 2 * d_model), row),
        ),
        out_shape=out_shape,
        compiler_params=pltpu.CompilerParams(dimension_semantics=("parallel",), vmem_limit_bytes=VMEM_LIMIT),
        name="in_proj",
    )(x2, g, w_main, w_kpe, w_gate, bg, qn, wuqt, kvn, wuk, wuvt, ca, sa, cb, sb, cbt, sbt)


def _dilated_group(qf, kf, vf, m_sc, l_sc, acc_sc, *, seq, window, dilation, first, interleave=1):
    length = seq // dilation
    radius = window // (2 * dilation)
    win = A_TILE + 2 * radius
    tiles = length // A_TILE
    assert length % A_TILE == 0 and tiles % A_TILES_PER_ITER == 0 and win <= length
    iters_per_residue = tiles // A_TILES_PER_ITER
    lane = lax.broadcasted_iota(jnp.int32, (1, LANES), 1)
    head0 = lane < A_HEAD_DIM
    delta = (lax.broadcasted_iota(jnp.int32, (2 * A_TILE, win), 0) % A_TILE
             - lax.broadcasted_iota(jnp.int32, (2 * A_TILE, win), 1))

    def strided(start, size):
        if dilation == 1:
            return pl.ds(pl.multiple_of(start, radius), size)
        return pl.ds(start, size, stride=dilation)

    def data_rows(r, pos, size):
        if interleave == 1:
            return strided(r + dilation * pos, size)
        step = dilation // interleave
        return pl.ds((r % interleave) * (seq // interleave) + r // interleave + step * pos, size, stride=step)

    def body(n, carry):
        r = n // iters_per_residue
        i0 = (n % iters_per_residue) * A_TILES_PER_ITER
        l0s = [(i0 + t) * A_TILE for t in range(A_TILES_PER_ITER)]
        wss = [jnp.clip(l0 - radius, 0, length - win) for l0 in l0s]
        ss = []
        for l0, ws in zip(l0s, wss):
            q = qf[data_rows(r, l0, A_TILE), :]
            k = kf[data_rows(r, ws, win), :].astype(jnp.bfloat16)
            q2 = jnp.concatenate([jnp.where(head0, q, 0), jnp.where(head0, 0, q)], axis=0)
            ss.append(lax.dot_general(q2.astype(jnp.bfloat16), k, _NT_DIMS, preferred_element_type=jnp.float32))
        stats = []
        for l0, ws, s in zip(l0s, wss, ss):
            dist = delta + (l0 - ws)
            s = jnp.where((dist <= radius) & (dist >= -radius), s, NEG_BIG)
            m = jnp.max(s, axis=-1, keepdims=True)
            p = jnp.exp2(s - m)
            stats.append((m, jnp.sum(p, axis=-1, keepdims=True), p.astype(jnp.bfloat16)))
        for l0, ws, (m, l, p) in zip(l0s, wss, stats):
            v = vf[data_rows(r, ws, win), :].astype(jnp.bfloat16)
            pv = jnp.dot(p, v, preferred_element_type=jnp.float32)
            acc_t = jnp.where(head0, pv[:A_TILE], pv[A_TILE:])
            m_t = jnp.where(head0, m[:A_TILE], m[A_TILE:])
            l_t = jnp.where(head0, l[:A_TILE], l[A_TILE:])
            rows = strided(r + dilation * l0, A_TILE)
            if first:
                m_sc[rows, :] = m_t
                l_sc[rows, :] = l_t
                acc_sc[rows, :] = acc_t
            else:
                m_old = m_sc[rows, :]
                m_new = jnp.maximum(m_old, m_t)
                a = jnp.exp2(m_old - m_new)
                b = jnp.exp2(m_t - m_new)
                m_sc[rows, :] = m_new
                l_sc[rows, :] = a * l_sc[rows, :] + b * l_t
                acc_sc[rows, :] = a * acc_sc[rows, :] + b * acc_t
        return carry

    lax.fori_loop(0, dilation * iters_per_residue, body, 0)


def _dilated_kernel(q_ref, k_ref, v_ref, o_ref, qf, kf, vf, m_sc, l_sc, acc_sc, tmp, *, seq):
    turn = pl.program_id(2)
    chunk = tmp.shape[0]

    def widen(i, carry, interleave=1):
        rows = pl.ds(pl.multiple_of(i * chunk, chunk), chunk)
        for src, dst in ((q_ref, qf), (k_ref, kf), (v_ref, vf)):
            if interleave == 1:
                dst[rows, :] = src[0, rows, :].astype(jnp.float32)
            else:
                tmp[...] = src[0, rows, :].astype(jnp.float32)
                part = chunk // interleave
                for c in range(interleave):
                    out = pl.ds(pl.multiple_of(c * (seq // interleave) + i * part, part), part)
                    dst[out, :] = tmp[pl.ds(c, part, stride=interleave), :]
        return carry

    for t, gi in enumerate(A_GROUP_ORDER):
        window, dilation = A_GROUPS[gi]

        @pl.when(turn == t)
        def _(window=window, dilation=dilation, t=t):
            interleave = A_DATA_INTERLEAVE if dilation % (4 * A_DATA_INTERLEAVE) == 0 else 1
            if dilation == 1:
                srcs = (q_ref.at[0], k_ref.at[0], v_ref.at[0])
            else:
                lax.fori_loop(0, seq // chunk, functools.partial(widen, interleave=interleave), 0)
                srcs = (qf, kf, vf)
            _dilated_group(*srcs, m_sc, l_sc, acc_sc, seq=seq, window=window, dilation=dilation, first=t == 0,
                           interleave=interleave)

    @pl.when(turn == len(A_GROUPS) - 1)
    def _():
        def finish(i, carry):
            rows = pl.ds(pl.multiple_of(i * chunk, chunk), chunk)
            o_ref[0, rows, :] = (acc_sc[rows, :] / l_sc[rows, :]).astype(o_ref.dtype)
            return carry

        lax.fori_loop(0, seq // chunk, finish, 0)


def _dilated_attention(qa, ka, va):
    batch, seq, _ = qa.shape
    pairs = A_GROUP_WIDTH // LANES
    first, n = A_GROUP_ORDER[0], len(A_GROUPS)
    assert A_GROUP_ORDER == tuple((first + t) % n for t in range(n))
    in_spec = pl.BlockSpec((1, seq, LANES), lambda b, hp, t: (b, 0, ((first + t) % n) * pairs + hp))
    state = pltpu.VMEM((seq, LANES), jnp.float32)
    return pl.pallas_call(
        functools.partial(_dilated_kernel, seq=seq),
        grid=(batch, pairs, len(A_GROUPS)),
        in_specs=[in_spec, in_spec, in_spec],
        out_specs=pl.BlockSpec((1, seq, LANES), lambda b, hp, g: (b, 0, hp)),
        out_shape=jax.ShapeDtypeStruct((batch, seq, A_GROUP_WIDTH), jnp.bfloat16),
        scratch_shapes=[state] * 6 + [pltpu.VMEM((TOKEN_TILE, LANES), jnp.float32)],
        compiler_params=pltpu.CompilerParams(dimension_semantics=("parallel", "parallel", "arbitrary"),
                                             vmem_limit_bytes=VMEM_LIMIT),
        name="dilated",
    )(qa, ka, va)


def _mla_kernel(qt_ref, k_ref, vt_ref, o_ref, *scratch, n_groups):
    tq = qt_ref.shape[3]
    tk = vt_ref.shape[3]
    n_streams = MLA_HEADS_PER_STEP * MLA_Q_TILES_PER_STEP
    sa_ref, sb_ref, pa_ref, pb_ref = (scratch[i * n_streams:(i + 1) * n_streams] for i in range(4))
    key_blocks = tk // MLA_KEY_BLOCK
    heads = range(n_streams)
    blks = [slice((st % MLA_HEADS_PER_STEP) * LANES, (st % MLA_HEADS_PER_STEP + 1) * LANES) for st in heads]
    qts = [qt_ref[0, st // MLA_HEADS_PER_STEP, blks[st], :] for st in heads]

    def rows(c):
        return slice(c * MLA_KEY_BLOCK, (c + 1) * MLA_KEY_BLOCK)

    def qk_block(h, g, c, s_ref):
        base = g * tk if isinstance(g, int) else pl.multiple_of(g * tk, tk)
        k = k_ref[0, pl.ds(base + c * MLA_KEY_BLOCK, MLA_KEY_BLOCK), blks[h]]
        s = jnp.dot(k, qts[h], preferred_element_type=jnp.float32)
        s_ref[h][rows(c), :] = s
        return jnp.max(s, axis=0, keepdims=True)

    def pv(h, g, p_ref):
        vt = vt_ref[0, g, blks[h].start:blks[h].start + MLA_VT_ROWS, :]
        return jnp.dot(vt, p_ref[h][...], preferred_element_type=jnp.float32)

    def matmuls(g_qk, s_ref, g_pv, p_ref):
        tops, pvs = [None for _ in heads], [None for _ in heads]

        def qk(h):
            for c in range(key_blocks if g_qk is not None else 0):
                top = qk_block(h, g_qk, c, s_ref)
                tops[h] = top if tops[h] is None else jnp.maximum(tops[h], top)

        for lo in range(0, n_streams, MLA_SCORE_STREAMS_RUN):
            run = heads[lo:lo + MLA_SCORE_STREAMS_RUN]
            for h in run:
                qk(h)
            if g_pv is not None:
                for h in run:
                    pvs[h] = pv(h, g_pv, p_ref)
        return tops, pvs

    def softmax(h, m, top, s_ref, p_ref):
        m_new = jnp.maximum(m, top)
        for c in range(key_blocks):
            p_ref[h][rows(c), :] = jnp.exp2((s_ref[h][rows(c), :] - m_new).astype(p_ref[h].dtype))
        return m_new, jnp.exp2(m - m_new)

    def step(g, carry, s_read, s_write, p_read, p_write, first=False, last=False):
        tops, pvs = matmuls(None if last else g + 1, s_write, None if first else g - 1, p_read)
        new = []
        for h in heads:
            m, acc, alpha, top = carry[h]
            m_new, alpha_new = softmax(h, m, top, s_read, p_write)
            new.append((m_new, acc if first else alpha * acc + pvs[h], alpha_new, tops[h]))
        return tuple(new)

    def even(g, carry, **kw):
        return step(g, carry, sa_ref, sb_ref, pb_ref, pa_ref, **kw)

    def odd(g, carry, **kw):
        return step(g, carry, sb_ref, sa_ref, pa_ref, pb_ref, **kw)

    tops, _ = matmuls(0, sa_ref, None, None)
    init = tuple((jnp.full((1, tq), NEG_BIG, jnp.float32), jnp.zeros((MLA_VT_ROWS, tq), jnp.float32),
                  jnp.ones((1, tq), jnp.float32), tops[h]) for h in heads)
    carry = even(0, init, first=True)

    def pairs(g, carry, count):
        for i in range(count):
            carry = even(g + 2 * i + 1, odd(g + 2 * i, carry))
        return carry

    n_pairs = (n_groups - 2) // 2
    trips = n_pairs // MLA_PAIRS_PER_ITER
    carry = lax.fori_loop(
        0, trips, lambda t, c: pairs(2 * MLA_PAIRS_PER_ITER * t + 1, c, MLA_PAIRS_PER_ITER), carry)
    carry = pairs(2 * MLA_PAIRS_PER_ITER * trips + 1, carry, n_pairs - trips * MLA_PAIRS_PER_ITER)
    carry = odd(n_groups - 1, carry, last=True)
    _, pvs = matmuls(None, None, n_groups - 1, pb_ref)
    accs = [alpha * acc + pvs[h] for h, (_, acc, alpha, _) in enumerate(carry)]
    outs = [a[:B_V] / a[B_V:B_V + 1] for a in accs]
    for t in range(MLA_Q_TILES_PER_STEP):
        tile_outs = outs[t * MLA_HEADS_PER_STEP:(t + 1) * MLA_HEADS_PER_STEP]
        o_ref[0, t * tq:(t + 1) * tq, :] = jnp.concatenate(tile_outs, axis=0).T.astype(o_ref.dtype)


def _mla_attention(qbt, kb, vbt):
    batch, tiles, _, tile = qbt.shape
    seq = kb.shape[1]
    n, qt = MLA_HEADS_PER_STEP, MLA_Q_TILES_PER_STEP
    return pl.pallas_call(
        functools.partial(_mla_kernel, n_groups=tiles),
        grid=(batch, B_HEADS // n, tiles // qt),
        in_specs=[
            pl.BlockSpec((1, qt, n * LANES, tile), lambda b, hp, i: (b, i, hp, 0)),
            pl.BlockSpec((1, seq, n * LANES), lambda b, hp, i: (b, 0, hp)),
            pl.BlockSpec((1, tiles, n * LANES, tile), lambda b, hp, i: (b, 0, hp, 0)),
        ],
        out_specs=pl.BlockSpec((1, qt * tile, n * B_V), lambda b, hp, i: (b, i, hp)),
        out_shape=jax.ShapeDtypeStruct((batch, seq, B_HEADS * B_V), jnp.bfloat16),
        scratch_shapes=([pltpu.VMEM((tile, tile), jnp.float32)] * (2 * n * qt)
                        + [pltpu.VMEM((tile, tile), jnp.bfloat16)] * (2 * n * qt)),
        compiler_params=pltpu.CompilerParams(dimension_semantics=("parallel", "parallel", "parallel"),
                                             vmem_limit_bytes=VMEM_LIMIT),
        name="mla_attn",
    )(qbt, kb, vbt)


def _mix_ffn_kernel(x_ref, oa_ref, ob_ref, gate_ref, woa_ref, wob_ref, wout_ref, gpost_ref, gpre_ref,
                    w1_ref, w2_ref, gmlp_ref, out_o, *, d_model):
    o_a = jnp.dot(oa_ref[...], woa_ref[...], preferred_element_type=jnp.float32)
    o_b = jnp.dot(ob_ref[...], wob_ref[...], preferred_element_type=jnp.float32)
    gates = gate_ref[...].astype(jnp.float32)
    mixin = (gates[:, :d_model] * o_a + gates[:, d_model:] * o_b).astype(jnp.bfloat16)
    mix = jnp.dot(mixin, wout_ref[...], preferred_element_type=jnp.float32)
    h = x_ref[...] + _rms(mix, gpost_ref[...])
    hn = _rms(h, gpre_ref[...]).astype(jnp.bfloat16)
    u = jnp.dot(hn, w1_ref[...], preferred_element_type=jnp.float32)
    u = jnp.square(jnp.maximum(u, 0.0)).astype(jnp.bfloat16)
    ff = jnp.dot(u, w2_ref[...], preferred_element_type=jnp.float32)
    out_o[...] = h + _rms(ff, gmlp_ref[...])


def _mix_ffn(x2, oa, ob, gates, woa, wob, wout, gpost, gpre, w1, w2, gmlp, *, tm):
    t, d_model = x2.shape
    d_ff = w1.shape[1]
    row = lambda i: (i, 0)
    whole = lambda shape: pl.BlockSpec(shape, lambda i: (0, 0), pipeline_mode=pl.Buffered(1))
    return pl.pallas_call(
        functools.partial(_mix_ffn_kernel, d_model=d_model),
        grid=(t // tm,),
        in_specs=[
            pl.BlockSpec((tm, d_model), row),
            pl.BlockSpec((tm, A_GROUP_WIDTH), row),
            pl.BlockSpec((tm, B_HEADS * B_V), row),
            pl.BlockSpec((tm, 2 * d_model), row),
            whole((A_GROUP_WIDTH, d_model)),
            whole((B_HEADS * B_V, d_model)),
            whole((d_model, d_model)),
            whole((1, d_model)),
            whole((1, d_model)),
            whole((d_model, d_ff)),
            whole((d_ff, d_model)),
            whole((1, d_model)),
        ],
        out_specs=pl.BlockSpec((tm, d_model), row),
        out_shape=jax.ShapeDtypeStruct((t, d_model), jnp.float32),
        compiler_params=pltpu.CompilerParams(dimension_semantics=("parallel",), vmem_limit_bytes=VMEM_LIMIT),
        name="mix_ffn",
    )(x2, oa, ob, gates, woa, wob, wout, gpost, gpre, w1, w2, gmlp)


def _rotary_tables(seq, theta, rot_dim, period, offset):
    half = rot_dim // 2
    inv = np.float64(np.float32(theta)) ** (-(np.arange(half, dtype=np.float64) * 2.0 / rot_dim))
    ang = np.arange(seq, dtype=np.float64)[:, None] * inv[None, :]
    cos, sin = np.cos(ang), np.sin(ang)
    cos_p = np.ones((seq, period))
    sin_p = np.zeros((seq, period))
    cos_p[:, offset:offset + rot_dim] = np.concatenate([cos, cos], 1)
    sin_p[:, offset:offset + rot_dim] = np.concatenate([-sin, sin], 1)
    reps = LANES // period
    return np.tile(cos_p, (1, reps)).astype(np.float32), np.tile(sin_p, (1, reps)).astype(np.float32)


def _pad_heads(w, heads, width):
    rows = w.shape[0]
    w = w.reshape(rows, heads, width)
    return jnp.pad(w, ((0, 0), (0, 0), (0, LANES - width))).reshape(rows, heads * LANES)


def kernel(x, norm_mix_pre, w_in, b_gate, mla_q_norm, mla_w_uq, mla_kv_norm, mla_w_ukv, w_o_a, w_o_b, w_out,
           norm_mix_post, norm_mlp_pre, w_ff1, w_ff2, norm_mlp_post):
    batch, seq, d_model = x.shape
    bf = jnp.bfloat16
    x2 = x.reshape(batch * seq, d_model)

    w_main = w_in[:, :_C_KPE].astype(bf)
    w_kpe = jnp.pad(w_in[:, _C_KPE:_C_KPE + B_ROPE].astype(bf), ((0, 0), (B_NOPE, LANES - B_QK)))
    w_gate = w_in[:, _C_KPE + B_ROPE:].astype(bf)
    wuqt = _pad_heads(mla_w_uq, B_HEADS, B_QK).T.astype(bf)
    wukv = mla_w_ukv.reshape(B_KV_RANK, B_HEADS, B_NOPE + B_V)
    wuk = _pad_heads(wukv[:, :, :B_NOPE].reshape(B_KV_RANK, B_HEADS * B_NOPE), B_HEADS, B_NOPE).astype(bf)
    wuvt = _pad_heads(wukv[:, :, B_NOPE:].reshape(B_KV_RANK, B_HEADS * B_V), B_HEADS, B_V).T.astype(bf)

    ca, sa = _rotary_tables(seq, A_ROPE_THETA, A_ROT_DIM, A_HEAD_DIM, 0)
    cb, sb = _rotary_tables(seq, B_ROPE_THETA, B_ROPE, LANES, B_NOPE)
    cbt, sbt = np.ascontiguousarray(cb.T), np.ascontiguousarray(sb.T)

    vec = lambda v: v.reshape(1, -1)
    qa, ka, va, qbt, kb, vbt, gates = _in_proj(
        x2, vec(norm_mix_pre), w_main, w_kpe, w_gate, vec(b_gate), vec(mla_q_norm), wuqt, vec(mla_kv_norm), wuk, wuvt,
        ca, sa, cb, sb, cbt, sbt, seq=seq, tm=TOKEN_TILE)

    shape_a = (batch, seq, A_WIDTH)
    oa = _dilated_attention(qa.reshape(shape_a), ka.reshape(shape_a), va.reshape(shape_a))
    oa = oa.reshape(batch * seq, A_GROUP_WIDTH)

    ob = _mla_attention(qbt, kb.reshape(batch, seq, B_WIDTH), vbt)
    ob = ob.reshape(batch * seq, B_HEADS * B_V)

    out = _mix_ffn(x2, oa, ob, gates, w_o_a.astype(bf), w_o_b.astype(bf), w_out.astype(bf),
                   vec(norm_mix_post), vec(norm_mlp_pre), w_ff1.astype(bf), w_ff2.astype(bf),
                   vec(norm_mlp_post), tm=TOKEN_TILE)
    return out.reshape(batch, seq, d_model)
```

```python
import functools

import jax
import jax.numpy as jnp
import numpy as np
from jax import lax
from jax.experimental import pallas as pl
from jax.experimental.pallas import tpu as pltpu

EPS = 1e-6
LANES = 128

A_HEAD_DIM = 64
A_HEADS_PER_GROUP = 4
A_GROUPS = ((128, 1), (512, 4), (2048, 16))
A_HEADS = A_HEADS_PER_GROUP * len(A_GROUPS)
A_WIDTH = A_HEADS * A_HEAD_DIM
A_GROUP_WIDTH = A_HEADS_PER_GROUP * A_HEAD_DIM
A_ROT_DIM = A_HEAD_DIM // 4
A_ROPE_THETA = 500000.0
A_TILE = 128
A_TILES_PER_ITER = 4
A_GROUP_ORDER = (2, 0, 1)
A_DATA_INTERLEAVE = 4
B_HEADS = 8
B_NOPE = 64
B_ROPE = 32
B_QK = B_NOPE + B_ROPE
B_V = 64
B_Q_RANK = 512
B_KV_RANK = 256
B_ROPE_THETA = 10000.0
B_WIDTH = B_HEADS * LANES
MLA_KEY_BLOCK = 256
MLA_PAIRS_PER_ITER = 1
MLA_HEADS_PER_STEP = 2
MLA_Q_TILES_PER_STEP = 4
MLA_SCORE_STREAMS_RUN = 2
MLA_VT_ROWS = B_V + 16

TOKEN_TILE = 512
NEG_BIG = -1e30
LOG2_E = 1.4426950408889634
VMEM_LIMIT = 56 * 1024 * 1024


def _rms(x, g):
    return x * lax.rsqrt(jnp.mean(x * x, axis=-1, keepdims=True) + EPS) * g


def _rotate_block(xb, cos, sin_signed, first_half, half):
    up = pltpu.roll(xb, LANES - half, 1)
    down = pltpu.roll(xb, half, 1)
    return xb * cos + jnp.where(first_half, up, down) * sin_signed


_C_QA, _C_KA, _C_VA = 0, A_WIDTH, 2 * A_WIDTH
_C_QC = 3 * A_WIDTH
_C_KVC = _C_QC + B_Q_RANK
_C_KPE = _C_KVC + B_KV_RANK


_NT_DIMS = (((1,), (1,)), ((), ()))


def _in_proj_kernel(x_ref, g_ref, w_ref, wkpe_ref, wgate_ref, bg_ref, qn_ref, wuqt_ref, kvn_ref, wuk_ref, wuvt_ref,
                    ca_ref, sa_ref, cb_ref, sb_ref, cbt_ref, sbt_ref,
                    qa_o, ka_o, va_o, qbt_o, kb_o, vbt_o, gate_o, *, d_model):
    xn = _rms(x_ref[...], g_ref[...]).astype(jnp.bfloat16)

    def proj(lo, hi):
        return jnp.dot(xn, w_ref[:, lo:hi], preferred_element_type=jnp.float32)

    lane = lax.broadcasted_iota(jnp.int32, (1, LANES), 1)
    first_a = (lane % A_HEAD_DIM) < (A_ROT_DIM // 2)
    first_b = lane < (B_NOPE + B_ROPE // 2)
    ca, sa, cb, sb = ca_ref[...], sa_ref[...], cb_ref[...], sb_ref[...]

    qcn = _rms(proj(_C_QC, _C_QC + B_Q_RANK), qn_ref[...]).astype(jnp.bfloat16)
    kvn = _rms(proj(_C_KVC, _C_KVC + B_KV_RANK), kvn_ref[...]).astype(jnp.bfloat16)
    kpe = jnp.dot(xn, wkpe_ref[...], preferred_element_type=jnp.float32)
    kpe = _rotate_block(kpe, cb, sb, first_b, B_ROPE // 2)

    z = jnp.dot(xn, wgate_ref[...], preferred_element_type=jnp.float32) + bg_ref[...]
    gate_o[...] = (0.5 * jnp.tanh(0.5 * z) + 0.5).astype(gate_o.dtype)

    qa = proj(_C_QA, _C_QA + A_WIDTH)
    ka = proj(_C_KA, _C_KA + A_WIDTH)
    a_scale = A_HEAD_DIM ** -0.5 * LOG2_E
    for j in range(A_WIDTH // LANES):
        blk = slice(j * LANES, (j + 1) * LANES)
        qa_o[:, blk] = (_rotate_block(qa[:, blk], ca, sa, first_a, A_ROT_DIM // 2) * a_scale).astype(qa_o.dtype)
        ka_o[:, blk] = _rotate_block(ka[:, blk], ca, sa, first_a, A_ROT_DIM // 2).astype(ka_o.dtype)
    va_o[...] = proj(_C_VA, _C_VA + A_WIDTH).astype(va_o.dtype)

    qbt = lax.dot_general(wuqt_ref[...], qcn, _NT_DIMS, preferred_element_type=jnp.float32)
    vbt = lax.dot_general(wuvt_ref[...], kvn, _NT_DIMS, preferred_element_type=jnp.float32)
    kb = jnp.dot(kvn, wuk_ref[...], preferred_element_type=jnp.float32)
    b_scale = B_QK ** -0.5 * LOG2_E
    cbt, sbt = cbt_ref[...], sbt_ref[...]
    row = lax.broadcasted_iota(jnp.int32, (LANES, 1), 0)
    r1, r2 = B_NOPE + B_ROPE // 2, B_QK
    for h in range(B_HEADS):
        blk = slice(h * LANES, (h + 1) * LANES)
        xt = qbt[blk, :]
        partner = jnp.concatenate([xt[:B_NOPE], xt[r1:r2], xt[B_NOPE:r1], xt[r2:]], axis=0)
        qbt_o[0, 0, blk, :] = ((xt * cbt + partner * sbt) * b_scale).astype(qbt_o.dtype)
        kb_o[:, blk] = (kb[:, blk] + kpe).astype(kb_o.dtype)
        vbt_o[0, 0, blk, :] = jnp.where(row < B_V, vbt[blk, :], 1.0).astype(vbt_o.dtype)


def _in_proj(x2, g, w_main, w_kpe, w_gate, bg, qn, wuqt, kvn, wuk, wuvt, ca, sa, cb, sb, cbt, sbt, *, seq, tm):
    t, d_model = x2.shape
    steps_per_seq = seq // tm
    row = lambda i: (i, 0)
    const = lambda i: (0, 0)
    pos = lambda i: (i % steps_per_seq, 0)
    pos_t = lambda i: (0, i % steps_per_seq)
    tile_t = lambda i: (i // steps_per_seq, i % steps_per_seq, 0, 0)
    bf = jnp.bfloat16
    transposed = jax.ShapeDtypeStruct((t // seq, steps_per_seq, B_WIDTH, tm), bf)
    out_shape = (
        jax.ShapeDtypeStruct((t, A_WIDTH), bf), jax.ShapeDtypeStruct((t, A_WIDTH), bf),
        jax.ShapeDtypeStruct((t, A_WIDTH), bf),
        transposed, jax.ShapeDtypeStruct((t, B_WIDTH), bf), transposed,
        jax.ShapeDtypeStruct((t, 2 * d_model), bf),
    )
    return pl.pallas_call(
        functools.partial(_in_proj_kernel, d_model=d_model),
        grid=(t // tm,),
        in_specs=[
            pl.BlockSpec((tm, d_model), row),
            pl.BlockSpec((1, d_model), const),
            pl.BlockSpec((d_model, _C_KPE), const),
            pl.BlockSpec((d_model, LANES), const),
            pl.BlockSpec((d_model, 2 * d_model), const),
            pl.BlockSpec((1, 2 * d_model), const),
            pl.BlockSpec((1, B_Q_RANK), const),
            pl.BlockSpec((B_WIDTH, B_Q_RANK), const),
            pl.BlockSpec((1, B_KV_RANK), const),
            pl.BlockSpec((B_KV_RANK, B_WIDTH), const),
            pl.BlockSpec((B_WIDTH, B_KV_RANK), const),
            pl.BlockSpec((tm, LANES), pos), pl.BlockSpec((tm, LANES), pos),
            pl.BlockSpec((tm, LANES), pos), pl.BlockSpec((tm, LANES), pos),
            pl.BlockSpec((LANES, tm), pos_t), pl.BlockSpec((LANES, tm), pos_t),
        ],
        out_specs=(
            pl.BlockSpec((tm, A_WIDTH), row), pl.BlockSpec((tm, A_WIDTH), row),
            pl.BlockSpec((tm, A_WIDTH), row),
            pl.BlockSpec((1, 1, B_WIDTH, tm), tile_t), pl.BlockSpec((tm, B_WIDTH), row),
            pl.BlockSpec((1, 1, B_WIDTH, tm), tile_t),
            pl.BlockSpec((tm, 2 * d_model), row),
        ),
        out_shape=out_shape,
        compiler_params=pltpu.CompilerParams(dimension_semantics=("parallel",), vmem_limit_bytes=VMEM_LIMIT),
        name="in_proj",
    )(x2, g, w_main, w_kpe, w_gate, bg, qn, wuqt, kvn, wuk, wuvt, ca, sa, cb, sb, cbt, sbt)


def _dilated_group(qf, kf, vf, m_sc, l_sc, acc_sc, *, seq, window, dilation, first, interleave=1):
    length = seq // dilation
    radius = window // (2 * dilation)
    win = A_TILE + 2 * radius
    tiles = length // A_TILE
    assert length % A_TILE == 0 and tiles % A_TILES_PER_ITER == 0 and win <= length
    iters_per_residue = tiles // A_TILES_PER_ITER
    lane = lax.broadcasted_iota(jnp.int32, (1, LANES), 1)
    head0 = lane < A_HEAD_DIM
    delta = (lax.broadcasted_iota(jnp.int32, (2 * A_TILE, win), 0) % A_TILE
             - lax.broadcasted_iota(jnp.int32, (2 * A_TILE, win), 1))

    def strided(start, size):
        if dilation == 1:
            return pl.ds(pl.multiple_of(start, radius), size)
        return pl.ds(start, size, stride=dilation)

    def data_rows(r, pos, size):
        if interleave == 1:
            return strided(r + dilation * pos, size)
        step = dilation // interleave
        return pl.ds((r % interleave) * (seq // interleave) + r // interleave + step * pos, size, stride=step)

    def body(n, carry):
        r = n // iters_per_residue
        i0 = (n % iters_per_residue) * A_TILES_PER_ITER
        l0s = [(i0 + t) * A_TILE for t in range(A_TILES_PER_ITER)]
        wss = [jnp.clip(l0 - radius, 0, length - win) for l0 in l0s]
        ss = []
        for l0, ws in zip(l0s, wss):
            q = qf[data_rows(r, l0, A_TILE), :]
            k = kf[data_rows(r, ws, win), :].astype(jnp.bfloat16)
            q2 = jnp.concatenate([jnp.where(head0, q, 0), jnp.where(head0, 0, q)], axis=0)
            ss.append(lax.dot_general(q2.astype(jnp.bfloat16), k, _NT_DIMS, preferred_element_type=jnp.float32))
        stats = []
        for l0, ws, s in zip(l0s, wss, ss):
            dist = delta + (l0 - ws)
            s = jnp.where((dist <= radius) & (dist >= -radius), s, NEG_BIG)
            m = jnp.max(s, axis=-1, keepdims=True)
            p = jnp.exp2(s - m)
            stats.append((m, jnp.sum(p, axis=-1, keepdims=True), p.astype(jnp.bfloat16)))
        for l0, ws, (m, l, p) in zip(l0s, wss, stats):
            v = vf[data_rows(r, ws, win), :].astype(jnp.bfloat16)
            pv = jnp.dot(p, v, preferred_element_type=jnp.float32)
            acc_t = jnp.where(head0, pv[:A_TILE], pv[A_TILE:])
            m_t = jnp.where(head0, m[:A_TILE], m[A_TILE:])
            l_t = jnp.where(head0, l[:A_TILE], l[A_TILE:])
            rows = strided(r + dilation * l0, A_TILE)
            if first:
                m_sc[rows, :] = m_t
                l_sc[rows, :] = l_t
                acc_sc[rows, :] = acc_t
            else:
                m_old = m_sc[rows, :]
                m_new = jnp.maximum(m_old, m_t)
                a = jnp.exp2(m_old - m_new)
                b = jnp.exp2(m_t - m_new)
                m_sc[rows, :] = m_new
                l_sc[rows, :] = a * l_sc[rows, :] + b * l_t
                acc_sc[rows, :] = a * acc_sc[rows, :] + b * acc_t
        return carry

    lax.fori_loop(0, dilation * iters_per_residue, body, 0)


def _dilated_kernel(q_ref, k_ref, v_ref, o_ref, qf, kf, vf, m_sc, l_sc, acc_sc, tmp, *, seq):
    turn = pl.program_id(2)
    chunk = tmp.shape[0]

    def widen(i, carry, interleave=1):
        rows = pl.ds(pl.multiple_of(i * chunk, chunk), chunk)
        for src, dst in ((q_ref, qf), (k_ref, kf), (v_ref, vf)):
            if interleave == 1:
                dst[rows, :] = src[0, rows, :].astype(jnp.float32)
            else:
                tmp[...] = src[0, rows, :].astype(jnp.float32)
                part = chunk // interleave
                for c in range(interleave):
                    out = pl.ds(pl.multiple_of(c * (seq // interleave) + i * part, part), part)
                    dst[out, :] = tmp[pl.ds(c, part, stride=interleave), :]
        return carry

    for t, gi in enumerate(A_GROUP_ORDER):
        window, dilation = A_GROUPS[gi]

        @pl.when(turn == t)
        def _(window=window, dilation=dilation, t=t):
            interleave = A_DATA_INTERLEAVE if dilation % (4 * A_DATA_INTERLEAVE) == 0 else 1
            if dilation == 1:
                srcs = (q_ref.at[0], k_ref.at[0], v_ref.at[0])
            else:
                lax.fori_loop(0, seq // chunk, functools.partial(widen, interleave=interleave), 0)
                srcs = (qf, kf, vf)
            _dilated_group(*srcs, m_sc, l_sc, acc_sc, seq=seq, window=window, dilation=dilation, first=t == 0,
                           interleave=interleave)

    @pl.when(turn == len(A_GROUPS) - 1)
    def _():
        def finish(i, carry):
            rows = pl.ds(pl.multiple_of(i * chunk, chunk), chunk)
            o_ref[0, rows, :] = (acc_sc[rows, :] / l_sc[rows, :]).astype(o_ref.dtype)
            return carry

        lax.fori_loop(0, seq // chunk, finish, 0)


def _dilated_attention(qa, ka, va):
    batch, seq, _ = qa.shape
    pairs = A_GROUP_WIDTH // LANES
    first, n = A_GROUP_ORDER[0], len(A_GROUPS)
    assert A_GROUP_ORDER == tuple((first + t) % n for t in range(n))
    in_spec = pl.BlockSpec((1, seq, LANES), lambda b, hp, t: (b, 0, ((first + t) % n) * pairs + hp))
    state = pltpu.VMEM((seq, LANES), jnp.float32)
    return pl.pallas_call(
        functools.partial(_dilated_kernel, seq=seq),
        grid=(batch, pairs, len(A_GROUPS)),
        in_specs=[in_spec, in_spec, in_spec],
        out_specs=pl.BlockSpec((1, seq, LANES), lambda b, hp, g: (b, 0, hp)),
        out_shape=jax.ShapeDtypeStruct((batch, seq, A_GROUP_WIDTH), jnp.bfloat16),
        scratch_shapes=[state] * 6 + [pltpu.VMEM((TOKEN_TILE, LANES), jnp.float32)],
        compiler_params=pltpu.CompilerParams(dimension_semantics=("parallel", "parallel", "arbitrary"),
                                             vmem_limit_bytes=VMEM_LIMIT),
        name="dilated",
    )(qa, ka, va)


def _mla_kernel(qt_ref, k_ref, vt_ref, o_ref, *scratch, n_groups):
    tq = qt_ref.shape[3]
    tk = vt_ref.shape[3]
    n_streams = MLA_HEADS_PER_STEP * MLA_Q_TILES_PER_STEP
    sa_ref, sb_ref, pa_ref, pb_ref = (scratch[i * n_streams:(i + 1) * n_streams] for i in range(4))
    key_blocks = tk // MLA_KEY_BLOCK
    heads = range(n_streams)
    blks = [slice((st % MLA_HEADS_PER_STEP) * LANES, (st % MLA_HEADS_PER_STEP + 1) * LANES) for st in heads]
    qts = [qt_ref[0, st // MLA_HEADS_PER_STEP, blks[st], :] for st in heads]

    def rows(c):
        return slice(c * MLA_KEY_BLOCK, (c + 1) * MLA_KEY_BLOCK)

    def qk_block(h, g, c, s_ref):
        base = g * tk if isinstance(g, int) else pl.multiple_of(g * tk, tk)
        k = k_ref[0, pl.ds(base + c * MLA_KEY_BLOCK, MLA_KEY_BLOCK), blks[h]]
        s = jnp.dot(k, qts[h], preferred_element_type=jnp.float32)
        s_ref[h][rows(c), :] = s
        return jnp.max(s, axis=0, keepdims=True)

    def pv(h, g, p_ref):
        vt = vt_ref[0, g, blks[h].start:blks[h].start + MLA_VT_ROWS, :]
        return jnp.dot(vt, p_ref[h][...], preferred_element_type=jnp.float32)

    def matmuls(g_qk, s_ref, g_pv, p_ref):
        tops, pvs = [None for _ in heads], [None for _ in heads]

        def qk(h):
            for c in range(key_blocks if g_qk is not None else 0):
                top = qk_block(h, g_qk, c, s_ref)
                tops[h] = top if tops[h] is None else jnp.maximum(tops[h], top)

        for lo in range(0, n_streams, MLA_SCORE_STREAMS_RUN):
            run = heads[lo:lo + MLA_SCORE_STREAMS_RUN]
            for h in run:
                qk(h)
            if g_pv is not None:
                for h in run:
                    pvs[h] = pv(h, g_pv, p_ref)
        return tops, pvs

    def softmax(h, m, top, s_ref, p_ref):
        m_new = jnp.maximum(m, top)
        for c in range(key_blocks):
            p_ref[h][rows(c), :] = jnp.exp2((s_ref[h][rows(c), :] - m_new).astype(p_ref[h].dtype))
        return m_new, jnp.exp2(m - m_new)

    def step(g, carry, s_read, s_write, p_read, p_write, first=False, last=False):
        tops, pvs = matmuls(None if last else g + 1, s_write, None if first else g - 1, p_read)
        new = []
        for h in heads:
            m, acc, alpha, top = carry[h]
            m_new, alpha_new = softmax(h, m, top, s_read, p_write)
            new.append((m_new, acc if first else alpha * acc + pvs[h], alpha_new, tops[h]))
        return tuple(new)

    def even(g, carry, **kw):
        return step(g, carry, sa_ref, sb_ref, pb_ref, pa_ref, **kw)

    def odd(g, carry, **kw):
        return step(g, carry, sb_ref, sa_ref, pa_ref, pb_ref, **kw)

    tops, _ = matmuls(0, sa_ref, None, None)
    init = tuple((jnp.full((1, tq), NEG_BIG, jnp.float32), jnp.zeros((MLA_VT_ROWS, tq), jnp.float32),
                  jnp.ones((1, tq), jnp.float32), tops[h]) for h in heads)
    carry = even(0, init, first=True)

    def pairs(g, carry, count):
        for i in range(count):
            carry = even(g + 2 * i + 1, odd(g + 2 * i, carry))
        return carry

    n_pairs = (n_groups - 2) // 2
    trips = n_pairs // MLA_PAIRS_PER_ITER
    carry = lax.fori_loop(
        0, trips, lambda t, c: pairs(2 * MLA_PAIRS_PER_ITER * t + 1, c, MLA_PAIRS_PER_ITER), carry)
    carry = pairs(2 * MLA_PAIRS_PER_ITER * trips + 1, carry, n_pairs - trips * MLA_PAIRS_PER_ITER)
    carry = odd(n_groups - 1, carry, last=True)
    _, pvs = matmuls(None, None, n_groups - 1, pb_ref)
    accs = [alpha * acc + pvs[h] for h, (_, acc, alpha, _) in enumerate(carry)]
    outs = [a[:B_V] / a[B_V:B_V + 1] for a in accs]
    for t in range(MLA_Q_TILES_PER_STEP):
        tile_outs = outs[t * MLA_HEADS_PER_STEP:(t + 1) * MLA_HEADS_PER_STEP]
        o_ref[0, t * tq:(t + 1) * tq, :] = jnp.concatenate(tile_outs, axis=0).T.astype(o_ref.dtype)


def _mla_attention(qbt, kb, vbt):
    batch, tiles, _, tile = qbt.shape
    seq = kb.shape[1]
    n, qt = MLA_HEADS_PER_STEP, MLA_Q_TILES_PER_STEP
    return pl.pallas_call(
        functools.partial(_mla_kernel, n_groups=tiles),
        grid=(batch, B_HEADS // n, tiles // qt),
        in_specs=[
            pl.BlockSpec((1, qt, n * LANES, tile), lambda b, hp, i: (b, i, hp, 0)),
            pl.BlockSpec((1, seq, n * LANES), lambda b, hp, i: (b, 0, hp)),
            pl.BlockSpec((1, tiles, n * LANES, tile), lambda b, hp, i: (b, 0, hp, 0)),
        ],
        out_specs=pl.BlockSpec((1, qt * tile, n * B_V), lambda b, hp, i: (b, i, hp)),
        out_shape=jax.ShapeDtypeStruct((batch, seq, B_HEADS * B_V), jnp.bfloat16),
        scratch_shapes=([pltpu.VMEM((tile, tile), jnp.float32)] * (2 * n * qt)
                        + [pltpu.VMEM((tile, tile), jnp.bfloat16)] * (2 * n * qt)),
        compiler_params=pltpu.CompilerParams(dimension_semantics=("parallel", "parallel", "parallel"),
                                             vmem_limit_bytes=VMEM_LIMIT),
        name="mla_attn",
    )(qbt, kb, vbt)


def _mix_ffn_kernel(x_ref, oa_ref, ob_ref, gate_ref, woa_ref, wob_ref, wout_ref, gpost_ref, gpre_ref,
                    w1_ref, w2_ref, gmlp_ref, out_o, *, d_model):
    tm = x_ref.shape[0]
    halves = [slice(i * tm // 2, (i + 1) * tm // 2) for i in range(2)]
    mixes = []
    for r in halves:
        o_a = jnp.dot(oa_ref[r, :], woa_ref[...], preferred_element_type=jnp.float32)
        o_b = jnp.dot(ob_ref[r, :], wob_ref[...], preferred_element_type=jnp.float32)
        gates = gate_ref[r, :].astype(jnp.float32)
        mixin = (gates[:, :d_model] * o_a + gates[:, d_model:] * o_b).astype(jnp.bfloat16)
        mixes.append(jnp.dot(mixin, wout_ref[...], preferred_element_type=jnp.float32))
    hs = [x_ref[r, :] + _rms(mix, gpost_ref[...]) for r, mix in zip(halves, mixes)]
    us = []
    for h in hs:
        hn = _rms(h, gpre_ref[...]).astype(jnp.bfloat16)
        u = jnp.dot(hn, w1_ref[...], preferred_element_type=jnp.float32)
        us.append(jnp.square(jnp.maximum(u, 0.0)).astype(jnp.bfloat16))
    ffs = [jnp.dot(u, w2_ref[...], preferred_element_type=jnp.float32) for u in us]
    for r, h, ff in zip(halves, hs, ffs):
        out_o[r, :] = h + _rms(ff, gmlp_ref[...])


def _mix_ffn(x2, oa, ob, gates, woa, wob, wout, gpost, gpre, w1, w2, gmlp, *, tm):
    t, d_model = x2.shape
    d_ff = w1.shape[1]
    row = lambda i: (i, 0)
    whole = lambda shape: pl.BlockSpec(shape, lambda i: (0, 0), pipeline_mode=pl.Buffered(1))
    return pl.pallas_call(
        functools.partial(_mix_ffn_kernel, d_model=d_model),
        grid=(t // tm,),
        in_specs=[
            pl.BlockSpec((tm, d_model), row),
            pl.BlockSpec((tm, A_GROUP_WIDTH), row),
            pl.BlockSpec((tm, B_HEADS * B_V), row),
            pl.BlockSpec((tm, 2 * d_model), row),
            whole((A_GROUP_WIDTH, d_model)),
            whole((B_HEADS * B_V, d_model)),
            whole((d_model, d_model)),
            whole((1, d_model)),
            whole((1, d_model)),
            whole((d_model, d_ff)),
            whole((d_ff, d_model)),
            whole((1, d_model)),
        ],
        out_specs=pl.BlockSpec((tm, d_model), row),
        out_shape=jax.ShapeDtypeStruct((t, d_model), jnp.float32),
        compiler_params=pltpu.CompilerParams(dimension_semantics=("parallel",), vmem_limit_bytes=VMEM_LIMIT),
        name="mix_ffn",
    )(x2, oa, ob, gates, woa, wob, wout, gpost, gpre, w1, w2, gmlp)


def _rotary_tables(seq, theta, rot_dim, period, offset):
    half = rot_dim // 2
    inv = np.float64(np.float32(theta)) ** (-(np.arange(half, dtype=np.float64) * 2.0 / rot_dim))
    ang = np.arange(seq, dtype=np.float64)[:, None] * inv[None, :]
    cos, sin = np.cos(ang), np.sin(ang)
    cos_p = np.ones((seq, period))
    sin_p = np.zeros((seq, period))
    cos_p[:, offset:offset + rot_dim] = np.concatenate([cos, cos], 1)
    sin_p[:, offset:offset + rot_dim] = np.concatenate([-sin, sin], 1)
    reps = LANES // period
    return np.tile(cos_p, (1, reps)).astype(np.float32), np.tile(sin_p, (1, reps)).astype(np.float32)


def _pad_heads(w, heads, width):
    rows = w.shape[0]
    w = w.reshape(rows, heads, width)
    return jnp.pad(w, ((0, 0), (0, 0), (0, LANES - width))).reshape(rows, heads * LANES)


def kernel(x, norm_mix_pre, w_in, b_gate, mla_q_norm, mla_w_uq, mla_kv_norm, mla_w_ukv, w_o_a, w_o_b, w_out,
           norm_mix_post, norm_mlp_pre, w_ff1, w_ff2, norm_mlp_post):
    batch, seq, d_model = x.shape
    bf = jnp.bfloat16
    x2 = x.reshape(batch * seq, d_model)

    w_main = w_in[:, :_C_KPE].astype(bf)
    w_kpe = jnp.pad(w_in[:, _C_KPE:_C_KPE + B_ROPE].astype(bf), ((0, 0), (B_NOPE, LANES - B_QK)))
    w_gate = w_in[:, _C_KPE + B_ROPE:].astype(bf)
    wuqt = _pad_heads(mla_w_uq, B_HEADS, B_QK).T.astype(bf)
    wukv = mla_w_ukv.reshape(B_KV_RANK, B_HEADS, B_NOPE + B_V)
    wuk = _pad_heads(wukv[:, :, :B_NOPE].reshape(B_KV_RANK, B_HEADS * B_NOPE), B_HEADS, B_NOPE).astype(bf)
    wuvt = _pad_heads(wukv[:, :, B_NOPE:].reshape(B_KV_RANK, B_HEADS * B_V), B_HEADS, B_V).T.astype(bf)

    ca, sa = _rotary_tables(seq, A_ROPE_THETA, A_ROT_DIM, A_HEAD_DIM, 0)
    cb, sb = _rotary_tables(seq, B_ROPE_THETA, B_ROPE, LANES, B_NOPE)
    cbt, sbt = np.ascontiguousarray(cb.T), np.ascontiguousarray(sb.T)

    vec = lambda v: v.reshape(1, -1)
    qa, ka, va, qbt, kb, vbt, gates = _in_proj(
        x2, vec(norm_mix_pre), w_main, w_kpe, w_gate, vec(b_gate), vec(mla_q_norm), wuqt, vec(mla_kv_norm), wuk, wuvt,
        ca, sa, cb, sb, cbt, sbt, seq=seq, tm=TOKEN_TILE)

    shape_a = (batch, seq, A_WIDTH)
    oa = _dilated_attention(qa.reshape(shape_a), ka.reshape(shape_a), va.reshape(shape_a))
    oa = oa.reshape(batch * seq, A_GROUP_WIDTH)

    ob = _mla_attention(qbt, kb.reshape(batch, seq, B_WIDTH), vbt)
    ob = ob.reshape(batch * seq, B_HEADS * B_V)

    out = _mix_ffn(x2, oa, ob, gates, w_o_a.astype(bf), w_o_b.astype(bf), w_out.astype(bf),
                   vec(norm_mix_post), vec(norm_mlp_pre), w_ff1.astype(bf), w_ff2.astype(bf),
                   vec(norm_mlp_post), tm=TOKEN_TILE)
    return out.reshape(batch, seq, d_model)
```

```python
import functools

import jax
import jax.numpy as jnp
import numpy as np
from jax import lax
from jax.experimental import pallas as pl
from jax.experimental.pallas import tpu as pltpu

EPS = 1e-6
LANES = 128

A_HEAD_DIM = 64
A_HEADS_PER_GROUP = 4
A_GROUPS = ((128, 1), (512, 4), (2048, 16))
A_HEADS = A_HEADS_PER_GROUP * len(A_GROUPS)
A_WIDTH = A_HEADS * A_HEAD_DIM
A_GROUP_WIDTH = A_HEADS_PER_GROUP * A_HEAD_DIM
A_ROT_DIM = A_HEAD_DIM // 4
A_ROPE_THETA = 500000.0
A_TILE = 128
A_TILES_PER_ITER = 4
A_GROUP_ORDER = (2, 0, 1)
A_DATA_INTERLEAVE = 4
B_HEADS = 8
B_NOPE = 64
B_ROPE = 32
B_QK = B_NOPE + B_ROPE
B_V = 64
B_Q_RANK = 512
B_KV_RANK = 256
B_ROPE_THETA = 10000.0
B_WIDTH = B_HEADS * LANES
MLA_KEY_BLOCK = 256
MLA_PAIRS_PER_ITER = 1
MLA_HEADS_PER_STEP = 2
MLA_Q_TILES_PER_STEP = 4
MLA_SCORE_STREAMS_RUN = 2
MLA_VT_ROWS = B_V + 16

TOKEN_TILE = 512
NEG_BIG = -1e30
LOG2_E = 1.4426950408889634
VMEM_LIMIT = 56 * 1024 * 1024


def _rms(x, g):
    return x * lax.rsqrt(jnp.mean(x * x, axis=-1, keepdims=True) + EPS) * g


def _rotate_block(xb, cos, sin_signed, first_half, half):
    up = pltpu.roll(xb, LANES - half, 1)
    down = pltpu.roll(xb, half, 1)
    return xb * cos + jnp.where(first_half, up, down) * sin_signed


_C_QA, _C_KA, _C_VA = 0, A_WIDTH, 2 * A_WIDTH
_C_QC = 3 * A_WIDTH
_C_KVC = _C_QC + B_Q_RANK
_C_KPE = _C_KVC + B_KV_RANK


_NT_DIMS = (((1,), (1,)), ((), ()))


def _in_proj_kernel(x_ref, g_ref, w_ref, wkpe_ref, wgate_ref, bg_ref, qn_ref, wuqt_ref, kvn_ref, wuk_ref, wuvt_ref,
                    ca_ref, sa_ref, cb_ref, sb_ref, cbt_ref, sbt_ref,
                    qa_o, ka_o, va_o, qbt_o, kb_o, vbt_o, gate_o, *, d_model):
    xn = _rms(x_ref[...], g_ref[...]).astype(jnp.bfloat16)

    def proj(lo, hi):
        return jnp.dot(xn, w_ref[:, lo:hi], preferred_element_type=jnp.float32)

    lane = lax.broadcasted_iota(jnp.int32, (1, LANES), 1)
    first_a = (lane % A_HEAD_DIM) < (A_ROT_DIM // 2)
    first_b = lane < (B_NOPE + B_ROPE // 2)
    ca, sa, cb, sb = ca_ref[...], sa_ref[...], cb_ref[...], sb_ref[...]

    qcn = _rms(proj(_C_QC, _C_QC + B_Q_RANK), qn_ref[...]).astype(jnp.bfloat16)
    kvn = _rms(proj(_C_KVC, _C_KVC + B_KV_RANK), kvn_ref[...]).astype(jnp.bfloat16)
    kpe = jnp.dot(xn, wkpe_ref[...], preferred_element_type=jnp.float32)
    kpe = _rotate_block(kpe, cb, sb, first_b, B_ROPE // 2)

    z = jnp.dot(xn, wgate_ref[...], preferred_element_type=jnp.float32) + bg_ref[...]
    gate_o[...] = (0.5 * jnp.tanh(0.5 * z) + 0.5).astype(gate_o.dtype)

    qa = proj(_C_QA, _C_QA + A_WIDTH)
    ka = proj(_C_KA, _C_KA + A_WIDTH)
    a_scale = A_HEAD_DIM ** -0.5 * LOG2_E
    for j in range(A_WIDTH // LANES):
        blk = slice(j * LANES, (j + 1) * LANES)
        qa_o[:, blk] = (_rotate_block(qa[:, blk], ca, sa, first_a, A_ROT_DIM // 2) * a_scale).astype(qa_o.dtype)
        ka_o[:, blk] = _rotate_block(ka[:, blk], ca, sa, first_a, A_ROT_DIM // 2).astype(ka_o.dtype)
    va_o[...] = proj(_C_VA, _C_VA + A_WIDTH).astype(va_o.dtype)

    qbt = lax.dot_general(wuqt_ref[...], qcn, _NT_DIMS, preferred_element_type=jnp.float32)
    vbt = lax.dot_general(wuvt_ref[...], kvn, _NT_DIMS, preferred_element_type=jnp.float32)
    kb = jnp.dot(kvn, wuk_ref[...], preferred_element_type=jnp.float32)
    b_scale = B_QK ** -0.5 * LOG2_E
    cbt, sbt = cbt_ref[...], sbt_ref[...]
    row = lax.broadcasted_iota(jnp.int32, (LANES, 1), 0)
    r1, r2 = B_NOPE + B_ROPE // 2, B_QK
    for h in range(B_HEADS):
        blk = slice(h * LANES, (h + 1) * LANES)
        xt = qbt[blk, :]
        partner = jnp.concatenate([xt[:B_NOPE], xt[r1:r2], xt[B_NOPE:r1], xt[r2:]], axis=0)
        qbt_o[0, 0, blk, :] = ((xt * cbt + partner * sbt) * b_scale).astype(qbt_o.dtype)
        kb_o[:, blk] = (kb[:, blk] + kpe).astype(kb_o.dtype)
        vbt_o[0, 0, blk, :] = jnp.where(row < B_V, vbt[blk, :], 1.0).astype(vbt_o.dtype)


def _in_proj(x2, g, w_main, w_kpe, w_gate, bg, qn, wuqt, kvn, wuk, wuvt, ca, sa, cb, sb, cbt, sbt, *, seq, tm):
    t, d_model = x2.shape
    steps_per_seq = seq // tm
    row = lambda i: (i, 0)
    const = lambda i: (0, 0)
    pos = lambda i: (i % steps_per_seq, 0)
    pos_t = lambda i: (0, i % steps_per_seq)
    tile_t = lambda i: (i // steps_per_seq, i % steps_per_seq, 0, 0)
    bf = jnp.bfloat16
    transposed = jax.ShapeDtypeStruct((t // seq, steps_per_seq, B_WIDTH, tm), bf)
    out_shape = (
        jax.ShapeDtypeStruct((t, A_WIDTH), bf), jax.ShapeDtypeStruct((t, A_WIDTH), bf),
        jax.ShapeDtypeStruct((t, A_WIDTH), bf),
        transposed, jax.ShapeDtypeStruct((t, B_WIDTH), bf), transposed,
        jax.ShapeDtypeStruct((t, 2 * d_model), bf),
    )
    return pl.pallas_call(
        functools.partial(_in_proj_kernel, d_model=d_model),
        grid=(t // tm,),
        in_specs=[
            pl.BlockSpec((tm, d_model), row),
            pl.BlockSpec((1, d_model), const),
            pl.BlockSpec((d_model, _C_KPE), const),
            pl.BlockSpec((d_model, LANES), const),
            pl.BlockSpec((d_model, 2 * d_model), const),
            pl.BlockSpec((1, 2 * d_model), const),
            pl.BlockSpec((1, B_Q_RANK), const),
            pl.BlockSpec((B_WIDTH, B_Q_RANK), const),
            pl.BlockSpec((1, B_KV_RANK), const),
            pl.BlockSpec((B_KV_RANK, B_WIDTH), const),
            pl.BlockSpec((B_WIDTH, B_KV_RANK), const),
            pl.BlockSpec((tm, LANES), pos), pl.BlockSpec((tm, LANES), pos),
            pl.BlockSpec((tm, LANES), pos), pl.BlockSpec((tm, LANES), pos),
            pl.BlockSpec((LANES, tm), pos_t), pl.BlockSpec((LANES, tm), pos_t),
        ],
        out_specs=(
            pl.BlockSpec((tm, A_WIDTH), row), pl.BlockSpec((tm, A_WIDTH), row),
            pl.BlockSpec((tm, A_WIDTH), row),
            pl.BlockSpec((1, 1, B_WIDTH, tm), tile_t), pl.BlockSpec((tm, B_WIDTH), row),
            pl.BlockSpec((1, 1, B_WIDTH, tm), tile_t),
            pl.BlockSpec((tm, 2 * d_model), row),
        ),
        out_shape=out_shape,
        compiler_params=pltpu.CompilerParams(dimension_semantics=("parallel",), vmem_limit_bytes=VMEM_LIMIT),
        name="in_proj",
    )(x2, g, w_main, w_kpe, w_gate, bg, qn, wuqt, kvn, wuk, wuvt, ca, sa, cb, sb, cbt, sbt)


def _dilated_group(qf, kf, vf, m_sc, l_sc, acc_sc, *, seq, window, dilation, first, interleave=1):
    length = seq // dilation
    radius = window // (2 * dilation)
    win = A_TILE + 2 * radius
    tiles = length // A_TILE
    assert length % A_TILE == 0 and tiles % A_TILES_PER_ITER == 0 and win <= length
    iters_per_residue = tiles // A_TILES_PER_ITER
    lane = lax.broadcasted_iota(jnp.int32, (1, LANES), 1)
    head0 = lane < A_HEAD_DIM
    delta = (lax.broadcasted_iota(jnp.int32, (A_TILE, win), 0)
             - lax.broadcasted_iota(jnp.int32, (A_TILE, win), 1))

    def strided(start, size):
        if dilation == 1:
            return pl.ds(pl.multiple_of(start, radius), size)
        return pl.ds(start, size, stride=dilation)

    def data_rows(r, pos, size):
        if interleave == 1:
            return strided(r + dilation * pos, size)
        step = dilation // interleave
        return pl.ds((r % interleave) * (seq // interleave) + r // interleave + step * pos, size, stride=step)

    def body(n, carry):
        r = n // iters_per_residue
        i0 = (n % iters_per_residue) * A_TILES_PER_ITER
        l0s = [(i0 + t) * A_TILE for t in range(A_TILES_PER_ITER)]
        wss = [jnp.clip(l0 - radius, 0, length - win) for l0 in l0s]
        ss = []
        for l0, ws in zip(l0s, wss):
            q = qf[data_rows(r, l0, A_TILE), :]
            k = kf[data_rows(r, ws, win), :].astype(jnp.bfloat16)
            qs = [jnp.where(head0, q, 0).astype(jnp.bfloat16), jnp.where(head0, 0, q).astype(jnp.bfloat16)]
            ss.append([lax.dot_general(qh, k, _NT_DIMS, preferred_element_type=jnp.float32) for qh in qs])
        stats = []
        for l0, ws, pair in zip(l0s, wss, ss):
            dist = delta + (l0 - ws)
            valid = (dist <= radius) & (dist >= -radius)
            out = []
            for s in pair:
                s = jnp.where(valid, s, NEG_BIG)
                m = jnp.max(s, axis=-1, keepdims=True)
                p = jnp.exp2(s - m)
                out.append((m, jnp.sum(p, axis=-1, keepdims=True), p.astype(jnp.bfloat16)))
            stats.append(out)
        for l0, ws, ((m0, l0_, p0), (m1, l1_, p1)) in zip(l0s, wss, stats):
            v = vf[data_rows(r, ws, win), :].astype(jnp.bfloat16)
            acc_t = jnp.where(head0, jnp.dot(p0, v, preferred_element_type=jnp.float32),
                              jnp.dot(p1, v, preferred_element_type=jnp.float32))
            m_t = jnp.where(head0, m0, m1)
            l_t = jnp.where(head0, l0_, l1_)
            rows = strided(r + dilation * l0, A_TILE)
            if first:
                m_sc[rows, :] = m_t
                l_sc[rows, :] = l_t
                acc_sc[rows, :] = acc_t
            else:
                m_old = m_sc[rows, :]
                m_new = jnp.maximum(m_old, m_t)
                a = jnp.exp2(m_old - m_new)
                b = jnp.exp2(m_t - m_new)
                m_sc[rows, :] = m_new
                l_sc[rows, :] = a * l_sc[rows, :] + b * l_t
                acc_sc[rows, :] = a * acc_sc[rows, :] + b * acc_t
        return carry

    lax.fori_loop(0, dilation * iters_per_residue, body, 0)


def _dilated_kernel(q_ref, k_ref, v_ref, o_ref, qf, kf, vf, m_sc, l_sc, acc_sc, tmp, *, seq):
    turn = pl.program_id(2)
    chunk = tmp.shape[0]

    def widen(i, carry, interleave=1):
        rows = pl.ds(pl.multiple_of(i * chunk, chunk), chunk)
        for src, dst in ((q_ref, qf), (k_ref, kf), (v_ref, vf)):
            if interleave == 1:
                dst[rows, :] = src[0, rows, :].astype(jnp.float32)
            else:
                tmp[...] = src[0, rows, :].astype(jnp.float32)
                part = chunk // interleave
                for c in range(interleave):
                    out = pl.ds(pl.multiple_of(c * (seq // interleave) + i * part, part), part)
                    dst[out, :] = tmp[pl.ds(c, part, stride=interleave), :]
        return carry

    for t, gi in enumerate(A_GROUP_ORDER):
        window, dilation = A_GROUPS[gi]

        @pl.when(turn == t)
        def _(window=window, dilation=dilation, t=t):
            interleave = A_DATA_INTERLEAVE if dilation % (4 * A_DATA_INTERLEAVE) == 0 else 1
            if dilation == 1:
                srcs = (q_ref.at[0], k_ref.at[0], v_ref.at[0])
            else:
                lax.fori_loop(0, seq // chunk, functools.partial(widen, interleave=interleave), 0)
                srcs = (qf, kf, vf)
            _dilated_group(*srcs, m_sc, l_sc, acc_sc, seq=seq, window=window, dilation=dilation, first=t == 0,
                           interleave=interleave)

    @pl.when(turn == len(A_GROUPS) - 1)
    def _():
        def finish(i, carry):
            rows = pl.ds(pl.multiple_of(i * chunk, chunk), chunk)
            o_ref[0, rows, :] = (acc_sc[rows, :] / l_sc[rows, :]).astype(o_ref.dtype)
            return carry

        lax.fori_loop(0, seq // chunk, finish, 0)


def _dilated_attention(qa, ka, va):
    batch, seq, _ = qa.shape
    pairs = A_GROUP_WIDTH // LANES
    first, n = A_GROUP_ORDER[0], len(A_GROUPS)
    assert A_GROUP_ORDER == tuple((first + t) % n for t in range(n))
    in_spec = pl.BlockSpec((1, seq, LANES), lambda b, hp, t: (b, 0, ((first + t) % n) * pairs + hp))
    state = pltpu.VMEM((seq, LANES), jnp.float32)
    return pl.pallas_call(
        functools.partial(_dilated_kernel, seq=seq),
        grid=(batch, pairs, len(A_GROUPS)),
        in_specs=[in_spec, in_spec, in_spec],
        out_specs=pl.BlockSpec((1, seq, LANES), lambda b, hp, g: (b, 0, hp)),
        out_shape=jax.ShapeDtypeStruct((batch, seq, A_GROUP_WIDTH), jnp.bfloat16),
        scratch_shapes=[state] * 6 + [pltpu.VMEM((TOKEN_TILE, LANES), jnp.float32)],
        compiler_params=pltpu.CompilerParams(dimension_semantics=("parallel", "parallel", "arbitrary"),
                                             vmem_limit_bytes=VMEM_LIMIT),
        name="dilated",
    )(qa, ka, va)


def _mla_kernel(qt_ref, k_ref, vt_ref, o_ref, *scratch, n_groups):
    tq = qt_ref.shape[3]
    tk = vt_ref.shape[3]
    n_streams = MLA_HEADS_PER_STEP * MLA_Q_TILES_PER_STEP
    sa_ref, sb_ref, pa_ref, pb_ref = (scratch[i * n_streams:(i + 1) * n_streams] for i in range(4))
    key_blocks = tk // MLA_KEY_BLOCK
    heads = range(n_streams)
    blks = [slice((st % MLA_HEADS_PER_STEP) * LANES, (st % MLA_HEADS_PER_STEP + 1) * LANES) for st in heads]
    qts = [qt_ref[0, st // MLA_HEADS_PER_STEP, blks[st], :] for st in heads]

    def rows(c):
        return slice(c * MLA_KEY_BLOCK, (c + 1) * MLA_KEY_BLOCK)

    def qk_block(h, g, c, s_ref):
        base = g * tk if isinstance(g, int) else pl.multiple_of(g * tk, tk)
        k = k_ref[0, pl.ds(base + c * MLA_KEY_BLOCK, MLA_KEY_BLOCK), blks[h]]
        s = jnp.dot(k, qts[h], preferred_element_type=jnp.float32)
        s_ref[h][rows(c), :] = s
        return jnp.max(s, axis=0, keepdims=True)

    def pv(h, g, p_ref):
        vt = vt_ref[0, g, blks[h].start:blks[h].start + MLA_VT_ROWS, :]
        return jnp.dot(vt, p_ref[h][...], preferred_element_type=jnp.float32)

    def matmuls(g_qk, s_ref, g_pv, p_ref):
        tops, pvs = [None for _ in heads], [None for _ in heads]

        def qk(h):
            for c in range(key_blocks if g_qk is not None else 0):
                top = qk_block(h, g_qk, c, s_ref)
                tops[h] = top if tops[h] is None else jnp.maximum(tops[h], top)

        for lo in range(0, n_streams, MLA_SCORE_STREAMS_RUN):
            run = heads[lo:lo + MLA_SCORE_STREAMS_RUN]
            for h in run:
                qk(h)
            if g_pv is not None:
                for h in run:
                    pvs[h] = pv(h, g_pv, p_ref)
        return tops, pvs

    def softmax(h, m, top, s_ref, p_ref):
        m_new = jnp.maximum(m, top)
        for c in range(key_blocks):
            p_ref[h][rows(c), :] = jnp.exp2((s_ref[h][rows(c), :] - m_new).astype(p_ref[h].dtype))
        return m_new, jnp.exp2(m - m_new)

    def step(g, carry, s_read, s_write, p_read, p_write, first=False, last=False):
        tops, pvs = matmuls(None if last else g + 1, s_write, None if first else g - 1, p_read)
        new = []
        for h in heads:
            m, acc, alpha, top = carry[h]
            m_new, alpha_new = softmax(h, m, top, s_read, p_write)
            new.append((m_new, acc if first else alpha * acc + pvs[h], alpha_new, tops[h]))
        return tuple(new)

    def even(g, carry, **kw):
        return step(g, carry, sa_ref, sb_ref, pb_ref, pa_ref, **kw)

    def odd(g, carry, **kw):
        return step(g, carry, sb_ref, sa_ref, pa_ref, pb_ref, **kw)

    tops, _ = matmuls(0, sa_ref, None, None)
    init = tuple((jnp.full((1, tq), NEG_BIG, jnp.float32), jnp.zeros((MLA_VT_ROWS, tq), jnp.float32),
                  jnp.ones((1, tq), jnp.float32), tops[h]) for h in heads)
    carry = even(0, init, first=True)

    def pairs(g, carry, count):
        for i in range(count):
            carry = even(g + 2 * i + 1, odd(g + 2 * i, carry))
        return carry

    n_pairs = (n_groups - 2) // 2
    trips = n_pairs // MLA_PAIRS_PER_ITER
    carry = lax.fori_loop(
        0, trips, lambda t, c: pairs(2 * MLA_PAIRS_PER_ITER * t + 1, c, MLA_PAIRS_PER_ITER), carry)
    carry = pairs(2 * MLA_PAIRS_PER_ITER * trips + 1, carry, n_pairs - trips * MLA_PAIRS_PER_ITER)
    carry = odd(n_groups - 1, carry, last=True)
    _, pvs = matmuls(None, None, n_groups - 1, pb_ref)
    accs = [alpha * acc + pvs[h] for h, (_, acc, alpha, _) in enumerate(carry)]
    outs = [a[:B_V] / a[B_V:B_V + 1] for a in accs]
    for t in range(MLA_Q_TILES_PER_STEP):
        tile_outs = outs[t * MLA_HEADS_PER_STEP:(t + 1) * MLA_HEADS_PER_STEP]
        o_ref[0, t * tq:(t + 1) * tq, :] = jnp.concatenate(tile_outs, axis=0).T.astype(o_ref.dtype)


def _mla_attention(qbt, kb, vbt):
    batch, tiles, _, tile = qbt.shape
    seq = kb.shape[1]
    n, qt = MLA_HEADS_PER_STEP, MLA_Q_TILES_PER_STEP
    return pl.pallas_call(
        functools.partial(_mla_kernel, n_groups=tiles),
        grid=(batch, B_HEADS // n, tiles // qt),
        in_specs=[
            pl.BlockSpec((1, qt, n * LANES, tile), lambda b, hp, i: (b, i, hp, 0)),
            pl.BlockSpec((1, seq, n * LANES), lambda b, hp, i: (b, 0, hp)),
            pl.BlockSpec((1, tiles, n * LANES, tile), lambda b, hp, i: (b, 0, hp, 0)),
        ],
        out_specs=pl.BlockSpec((1, qt * tile, n * B_V), lambda b, hp, i: (b, i, hp)),
        out_shape=jax.ShapeDtypeStruct((batch, seq, B_HEADS * B_V), jnp.bfloat16),
        scratch_shapes=([pltpu.VMEM((tile, tile), jnp.float32)] * (2 * n * qt)
                        + [pltpu.VMEM((tile, tile), jnp.bfloat16)] * (2 * n * qt)),
        compiler_params=pltpu.CompilerParams(dimension_semantics=("parallel", "parallel", "parallel"),
                                             vmem_limit_bytes=VMEM_LIMIT),
        name="mla_attn",
    )(qbt, kb, vbt)


def _mix_ffn_kernel(x_ref, oa_ref, ob_ref, gate_ref, woa_ref, wob_ref, wout_ref, gpost_ref, gpre_ref,
                    w1_ref, w2_ref, gmlp_ref, out_o, *, d_model):
    tm = x_ref.shape[0]
    halves = [slice(i * tm // 2, (i + 1) * tm // 2) for i in range(2)]
    mixes = []
    for r in halves:
        o_a = jnp.dot(oa_ref[r, :], woa_ref[...], preferred_element_type=jnp.float32)
        o_b = jnp.dot(ob_ref[r, :], wob_ref[...], preferred_element_type=jnp.float32)
        gates = gate_ref[r, :].astype(jnp.float32)
        mixin = (gates[:, :d_model] * o_a + gates[:, d_model:] * o_b).astype(jnp.bfloat16)
        mixes.append(jnp.dot(mixin, wout_ref[...], preferred_element_type=jnp.float32))
    hs = [x_ref[r, :] + _rms(mix, gpost_ref[...]) for r, mix in zip(halves, mixes)]
    us = []
    for h in hs:
        hn = _rms(h, gpre_ref[...]).astype(jnp.bfloat16)
        u = jnp.dot(hn, w1_ref[...], preferred_element_type=jnp.float32)
        us.append(jnp.square(jnp.maximum(u, 0.0)).astype(jnp.bfloat16))
    ffs = [jnp.dot(u, w2_ref[...], preferred_element_type=jnp.float32) for u in us]
    for r, h, ff in zip(halves, hs, ffs):
        out_o[r, :] = h + _rms(ff, gmlp_ref[...])


def _mix_ffn(x2, oa, ob, gates, woa, wob, wout, gpost, gpre, w1, w2, gmlp, *, tm):
    t, d_model = x2.shape
    d_ff = w1.shape[1]
    row = lambda i: (i, 0)
    whole = lambda shape: pl.BlockSpec(shape, lambda i: (0, 0), pipeline_mode=pl.Buffered(1))
    return pl.pallas_call(
        functools.partial(_mix_ffn_kernel, d_model=d_model),
        grid=(t // tm,),
        in_specs=[
            pl.BlockSpec((tm, d_model), row),
            pl.BlockSpec((tm, A_GROUP_WIDTH), row),
            pl.BlockSpec((tm, B_HEADS * B_V), row),
            pl.BlockSpec((tm, 2 * d_model), row),
            whole((A_GROUP_WIDTH, d_model)),
            whole((B_HEADS * B_V, d_model)),
            whole((d_model, d_model)),
            whole((1, d_model)),
            whole((1, d_model)),
            whole((d_model, d_ff)),
            whole((d_ff, d_model)),
            whole((1, d_model)),
        ],
        out_specs=pl.BlockSpec((tm, d_model), row),
        out_shape=jax.ShapeDtypeStruct((t, d_model), jnp.float32),
        compiler_params=pltpu.CompilerParams(dimension_semantics=("parallel",), vmem_limit_bytes=VMEM_LIMIT),
        name="mix_ffn",
    )(x2, oa, ob, gates, woa, wob, wout, gpost, gpre, w1, w2, gmlp)


def _rotary_tables(seq, theta, rot_dim, period, offset):
    half = rot_dim // 2
    inv = np.float64(np.float32(theta)) ** (-(np.arange(half, dtype=np.float64) * 2.0 / rot_dim))
    ang = np.arange(seq, dtype=np.float64)[:, None] * inv[None, :]
    cos, sin = np.cos(ang), np.sin(ang)
    cos_p = np.ones((seq, period))
    sin_p = np.zeros((seq, period))
    cos_p[:, offset:offset + rot_dim] = np.concatenate([cos, cos], 1)
    sin_p[:, offset:offset + rot_dim] = np.concatenate([-sin, sin], 1)
    reps = LANES // period
    return np.tile(cos_p, (1, reps)).astype(np.float32), np.tile(sin_p, (1, reps)).astype(np.float32)


def _pad_heads(w, heads, width):
    rows = w.shape[0]
    w = w.reshape(rows, heads, width)
    return jnp.pad(w, ((0, 0), (0, 0), (0, LANES - width))).reshape(rows, heads * LANES)


def kernel(x, norm_mix_pre, w_in, b_gate, mla_q_norm, mla_w_uq, mla_kv_norm, mla_w_ukv, w_o_a, w_o_b, w_out,
           norm_mix_post, norm_mlp_pre, w_ff1, w_ff2, norm_mlp_post):
    batch, seq, d_model = x.shape
    bf = jnp.bfloat16
    x2 = x.reshape(batch * seq, d_model)

    w_main = w_in[:, :_C_KPE].astype(bf)
    w_kpe = jnp.pad(w_in[:, _C_KPE:_C_KPE + B_ROPE].astype(bf), ((0, 0), (B_NOPE, LANES - B_QK)))
    w_gate = w_in[:, _C_KPE + B_ROPE:].astype(bf)
    wuqt = _pad_heads(mla_w_uq, B_HEADS, B_QK).T.astype(bf)
    wukv = mla_w_ukv.reshape(B_KV_RANK, B_HEADS, B_NOPE + B_V)
    wuk = _pad_heads(wukv[:, :, :B_NOPE].reshape(B_KV_RANK, B_HEADS * B_NOPE), B_HEADS, B_NOPE).astype(bf)
    wuvt = _pad_heads(wukv[:, :, B_NOPE:].reshape(B_KV_RANK, B_HEADS * B_V), B_HEADS, B_V).T.astype(bf)

    ca, sa = _rotary_tables(seq, A_ROPE_THETA, A_ROT_DIM, A_HEAD_DIM, 0)
    cb, sb = _rotary_tables(seq, B_ROPE_THETA, B_ROPE, LANES, B_NOPE)
    cbt, sbt = np.ascontiguousarray(cb.T), np.ascontiguousarray(sb.T)

    vec = lambda v: v.reshape(1, -1)
    qa, ka, va, qbt, kb, vbt, gates = _in_proj(
        x2, vec(norm_mix_pre), w_main, w_kpe, w_gate, vec(b_gate), vec(mla_q_norm), wuqt, vec(mla_kv_norm), wuk, wuvt,
        ca, sa, cb, sb, cbt, sbt, seq=seq, tm=TOKEN_TILE)

    shape_a = (batch, seq, A_WIDTH)
    oa = _dilated_attention(qa.reshape(shape_a), ka.reshape(shape_a), va.reshape(shape_a))
    oa = oa.reshape(batch * seq, A_GROUP_WIDTH)

    ob = _mla_attention(qbt, kb.reshape(batch, seq, B_WIDTH), vbt)
    ob = ob.reshape(batch * seq, B_HEADS * B_V)

    out = _mix_ffn(x2, oa, ob, gates, w_o_a.astype(bf), w_o_b.astype(bf), w_out.astype(bf),
                   vec(norm_mix_post), vec(norm_mlp_pre), w_ff1.astype(bf), w_ff2.astype(bf),
                   vec(norm_mlp_post), tm=TOKEN_TILE)
    return out.reshape(batch, seq, d_model)
```

```python
import functools

import jax
import jax.numpy as jnp
import numpy as np
from jax import lax
from jax.experimental import pallas as pl
from jax.experimental.pallas import tpu as pltpu

EPS = 1e-6
LANES = 128

A_HEAD_DIM = 64
A_HEADS_PER_GROUP = 4
A_GROUPS = ((128, 1), (512, 4), (2048, 16))
A_HEADS = A_HEADS_PER_GROUP * len(A_GROUPS)
A_WIDTH = A_HEADS * A_HEAD_DIM
A_GROUP_WIDTH = A_HEADS_PER_GROUP * A_HEAD_DIM
A_ROT_DIM = A_HEAD_DIM // 4
A_ROPE_THETA = 500000.0
A_TILE = 128
A_TILES_PER_ITER = 8
A_GROUP_ORDER = (2, 0, 1)
A_DATA_INTERLEAVE = 4
B_HEADS = 8
B_NOPE = 64
B_ROPE = 32
B_QK = B_NOPE + B_ROPE
B_V = 64
B_Q_RANK = 512
B_KV_RANK = 256
B_ROPE_THETA = 10000.0
B_WIDTH = B_HEADS * LANES
MLA_KEY_BLOCK = 256
MLA_PAIRS_PER_ITER = 1
MLA_HEADS_PER_STEP = 2
MLA_Q_TILES_PER_STEP = 4
MLA_SCORE_STREAMS_RUN = 2
MLA_VT_ROWS = B_V + 16

TOKEN_TILE = 512
NEG_BIG = -1e30
LOG2_E = 1.4426950408889634
VMEM_LIMIT = 56 * 1024 * 1024


def _rms(x, g):
    return x * lax.rsqrt(jnp.mean(x * x, axis=-1, keepdims=True) + EPS) * g


def _rotate_block(xb, cos, sin_signed, first_half, half):
    up = pltpu.roll(xb, LANES - half, 1)
    down = pltpu.roll(xb, half, 1)
    return xb * cos + jnp.where(first_half, up, down) * sin_signed


_C_QA, _C_KA, _C_VA = 0, A_WIDTH, 2 * A_WIDTH
_C_QC = 3 * A_WIDTH
_C_KVC = _C_QC + B_Q_RANK
_C_KPE = _C_KVC + B_KV_RANK


_NT_DIMS = (((1,), (1,)), ((), ()))


def _in_proj_kernel(x_ref, g_ref, w_ref, wkpe_ref, wgate_ref, bg_ref, qn_ref, wuqt_ref, kvn_ref, wuk_ref, wuvt_ref,
                    ca_ref, sa_ref, cb_ref, sb_ref, cbt_ref, sbt_ref,
                    qa_o, ka_o, va_o, qbt_o, kb_o, vbt_o, gate_o, *, d_model):
    xn = _rms(x_ref[...], g_ref[...]).astype(jnp.bfloat16)

    def proj(lo, hi):
        return jnp.dot(xn, w_ref[:, lo:hi], preferred_element_type=jnp.float32)

    lane = lax.broadcasted_iota(jnp.int32, (1, LANES), 1)
    first_a = (lane % A_HEAD_DIM) < (A_ROT_DIM // 2)
    first_b = lane < (B_NOPE + B_ROPE // 2)
    ca, sa, cb, sb = ca_ref[...], sa_ref[...], cb_ref[...], sb_ref[...]

    qcn = _rms(proj(_C_QC, _C_QC + B_Q_RANK), qn_ref[...]).astype(jnp.bfloat16)
    kvn = _rms(proj(_C_KVC, _C_KVC + B_KV_RANK), kvn_ref[...]).astype(jnp.bfloat16)
    kpe = jnp.dot(xn, wkpe_ref[...], preferred_element_type=jnp.float32)
    kpe = _rotate_block(kpe, cb, sb, first_b, B_ROPE // 2)

    z = jnp.dot(xn, wgate_ref[...], preferred_element_type=jnp.float32) + bg_ref[...]
    gate_o[...] = (0.5 * jnp.tanh(0.5 * z) + 0.5).astype(gate_o.dtype)

    qa = proj(_C_QA, _C_QA + A_WIDTH)
    ka = proj(_C_KA, _C_KA + A_WIDTH)
    a_scale = A_HEAD_DIM ** -0.5 * LOG2_E
    for j in range(A_WIDTH // LANES):
        blk = slice(j * LANES, (j + 1) * LANES)
        qa_o[:, blk] = (_rotate_block(qa[:, blk], ca, sa, first_a, A_ROT_DIM // 2) * a_scale).astype(qa_o.dtype)
        ka_o[:, blk] = _rotate_block(ka[:, blk], ca, sa, first_a, A_ROT_DIM // 2).astype(ka_o.dtype)
    va_o[...] = proj(_C_VA, _C_VA + A_WIDTH).astype(va_o.dtype)

    qbt = lax.dot_general(wuqt_ref[...], qcn, _NT_DIMS, preferred_element_type=jnp.float32)
    vbt = lax.dot_general(wuvt_ref[...], kvn, _NT_DIMS, preferred_element_type=jnp.float32)
    kb = jnp.dot(kvn, wuk_ref[...], preferred_element_type=jnp.float32)
    b_scale = B_QK ** -0.5 * LOG2_E
    cbt, sbt = cbt_ref[...], sbt_ref[...]
    row = lax.broadcasted_iota(jnp.int32, (LANES, 1), 0)
    r1, r2 = B_NOPE + B_ROPE // 2, B_QK
    for h in range(B_HEADS):
        blk = slice(h * LANES, (h + 1) * LANES)
        xt = qbt[blk, :]
        partner = jnp.concatenate([xt[:B_NOPE], xt[r1:r2], xt[B_NOPE:r1], xt[r2:]], axis=0)
        qbt_o[0, 0, blk, :] = ((xt * cbt + partner * sbt) * b_scale).astype(qbt_o.dtype)
        kb_o[:, blk] = (kb[:, blk] + kpe).astype(kb_o.dtype)
        vbt_o[0, 0, blk, :] = jnp.where(row < B_V, vbt[blk, :], 1.0).astype(vbt_o.dtype)


def _in_proj(x2, g, w_main, w_kpe, w_gate, bg, qn, wuqt, kvn, wuk, wuvt, ca, sa, cb, sb, cbt, sbt, *, seq, tm):
    t, d_model = x2.shape
    steps_per_seq = seq // tm
    row = lambda i: (i, 0)
    const = lambda i: (0, 0)
    pos = lambda i: (i % steps_per_seq, 0)
    pos_t = lambda i: (0, i % steps_per_seq)
    tile_t = lambda i: (i // steps_per_seq, i % steps_per_seq, 0, 0)
    bf = jnp.bfloat16
    transposed = jax.ShapeDtypeStruct((t // seq, steps_per_seq, B_WIDTH, tm), bf)
    out_shape = (
        jax.ShapeDtypeStruct((t, A_WIDTH), bf), jax.ShapeDtypeStruct((t, A_WIDTH), bf),
        jax.ShapeDtypeStruct((t, A_WIDTH), bf),
        transposed, jax.ShapeDtypeStruct((t, B_WIDTH), bf), transposed,
        jax.ShapeDtypeStruct((t, 2 * d_model), bf),
    )
    return pl.pallas_call(
        functools.partial(_in_proj_kernel, d_model=d_model),
        grid=(t // tm,),
        in_specs=[
            pl.BlockSpec((tm, d_model), row),
            pl.BlockSpec((1, d_model), const),
            pl.BlockSpec((d_model, _C_KPE), const),
            pl.BlockSpec((d_model, LANES), const),
            pl.BlockSpec((d_model, 2 * d_model), const),
            pl.BlockSpec((1, 2 * d_model), const),
            pl.BlockSpec((1, B_Q_RANK), const),
            pl.BlockSpec((B_WIDTH, B_Q_RANK), const),
            pl.BlockSpec((1, B_KV_RANK), const),
            pl.BlockSpec((B_KV_RANK, B_WIDTH), const),
            pl.BlockSpec((B_WIDTH, B_KV_RANK), const),
            pl.BlockSpec((tm, LANES), pos), pl.BlockSpec((tm, LANES), pos),
            pl.BlockSpec((tm, LANES), pos), pl.BlockSpec((tm, LANES), pos),
            pl.BlockSpec((LANES, tm), pos_t), pl.BlockSpec((LANES, tm), pos_t),
        ],
        out_specs=(
            pl.BlockSpec((tm, A_WIDTH), row), pl.BlockSpec((tm, A_WIDTH), row),
            pl.BlockSpec((tm, A_WIDTH), row),
            pl.BlockSpec((1, 1, B_WIDTH, tm), tile_t), pl.BlockSpec((tm, B_WIDTH), row),
            pl.BlockSpec((1, 1, B_WIDTH, tm), tile_t),
            pl.BlockSpec((tm, 2 * d_model), row),
        ),
        out_shape=out_shape,
        compiler_params=pltpu.CompilerParams(dimension_semantics=("parallel",), vmem_limit_bytes=VMEM_LIMIT),
        name="in_proj",
    )(x2, g, w_main, w_kpe, w_gate, bg, qn, wuqt, kvn, wuk, wuvt, ca, sa, cb, sb, cbt, sbt)


def _dilated_group(qf, kf, vf, m_sc, l_sc, acc_sc, *, seq, window, dilation, first, interleave=1):
    length = seq // dilation
    radius = window // (2 * dilation)
    win = A_TILE + 2 * radius
    tiles = length // A_TILE
    per_iter = min(A_TILES_PER_ITER, tiles)
    assert length % A_TILE == 0 and tiles % per_iter == 0 and win <= length
    iters_per_residue = tiles // per_iter
    lane = lax.broadcasted_iota(jnp.int32, (1, LANES), 1)
    head0 = lane < A_HEAD_DIM
    delta = (lax.broadcasted_iota(jnp.int32, (A_TILE, win), 0)
             - lax.broadcasted_iota(jnp.int32, (A_TILE, win), 1))

    def strided(start, size):
        if dilation == 1:
            return pl.ds(pl.multiple_of(start, radius), size)
        return pl.ds(start, size, stride=dilation)

    def data_rows(r, pos, size):
        if interleave == 1:
            return strided(r + dilation * pos, size)
        step = dilation // interleave
        return pl.ds((r % interleave) * (seq // interleave) + r // interleave + step * pos, size, stride=step)

    def body(n, carry):
        r = n // iters_per_residue
        i0 = (n % iters_per_residue) * per_iter
        l0s = [(i0 + t) * A_TILE for t in range(per_iter)]
        wss = [jnp.clip(l0 - radius, 0, length - win) for l0 in l0s]
        ss = []
        for l0, ws in zip(l0s, wss):
            q = qf[data_rows(r, l0, A_TILE), :]
            k = kf[data_rows(r, ws, win), :].astype(jnp.bfloat16)
            qs = [jnp.where(head0, q, 0).astype(jnp.bfloat16), jnp.where(head0, 0, q).astype(jnp.bfloat16)]
            ss.append([lax.dot_general(qh, k, _NT_DIMS, preferred_element_type=jnp.float32) for qh in qs])
        stats = []
        for l0, ws, pair in zip(l0s, wss, ss):
            dist = delta + (l0 - ws)
            valid = (dist <= radius) & (dist >= -radius)
            out = []
            for s in pair:
                s = jnp.where(valid, s, NEG_BIG)
                m = jnp.max(s, axis=-1, keepdims=True)
                p = jnp.exp2(s - m)
                out.append((m, jnp.sum(p, axis=-1, keepdims=True), p.astype(jnp.bfloat16)))
            stats.append(out)
        for l0, ws, ((m0, l0_, p0), (m1, l1_, p1)) in zip(l0s, wss, stats):
            v = vf[data_rows(r, ws, win), :].astype(jnp.bfloat16)
            acc_t = jnp.where(head0, jnp.dot(p0, v, preferred_element_type=jnp.float32),
                              jnp.dot(p1, v, preferred_element_type=jnp.float32))
            m_t = jnp.where(head0, m0, m1)
            l_t = jnp.where(head0, l0_, l1_)
            rows = strided(r + dilation * l0, A_TILE)
            if first:
                m_sc[rows, :] = m_t
                l_sc[rows, :] = l_t
                acc_sc[rows, :] = acc_t
            else:
                m_old = m_sc[rows, :]
                m_new = jnp.maximum(m_old, m_t)
                a = jnp.exp2(m_old - m_new)
                b = jnp.exp2(m_t - m_new)
                m_sc[rows, :] = m_new
                l_sc[rows, :] = a * l_sc[rows, :] + b * l_t
                acc_sc[rows, :] = a * acc_sc[rows, :] + b * acc_t
        return carry

    lax.fori_loop(0, dilation * iters_per_residue, body, 0)


def _dilated_kernel(q_ref, k_ref, v_ref, o_ref, qf, kf, vf, m_sc, l_sc, acc_sc, tmp, *, seq):
    turn = pl.program_id(2)
    chunk = tmp.shape[0]

    def widen(i, carry, interleave=1):
        rows = pl.ds(pl.multiple_of(i * chunk, chunk), chunk)
        for src, dst in ((q_ref, qf), (k_ref, kf), (v_ref, vf)):
            if interleave == 1:
                dst[rows, :] = src[0, rows, :].astype(jnp.float32)
            else:
                tmp[...] = src[0, rows, :].astype(jnp.float32)
                part = chunk // interleave
                for c in range(interleave):
                    out = pl.ds(pl.multiple_of(c * (seq // interleave) + i * part, part), part)
                    dst[out, :] = tmp[pl.ds(c, part, stride=interleave), :]
        return carry

    for t, gi in enumerate(A_GROUP_ORDER):
        window, dilation = A_GROUPS[gi]

        @pl.when(turn == t)
        def _(window=window, dilation=dilation, t=t):
            interleave = A_DATA_INTERLEAVE if dilation % (4 * A_DATA_INTERLEAVE) == 0 else 1
            if dilation == 1:
                srcs = (q_ref.at[0], k_ref.at[0], v_ref.at[0])
            else:
                lax.fori_loop(0, seq // chunk, functools.partial(widen, interleave=interleave), 0)
                srcs = (qf, kf, vf)
            _dilated_group(*srcs, m_sc, l_sc, acc_sc, seq=seq, window=window, dilation=dilation, first=t == 0,
                           interleave=interleave)

    @pl.when(turn == len(A_GROUPS) - 1)
    def _():
        def finish(i, carry):
            rows = pl.ds(pl.multiple_of(i * chunk, chunk), chunk)
            o_ref[0, rows, :] = (acc_sc[rows, :] / l_sc[rows, :]).astype(o_ref.dtype)
            return carry

        lax.fori_loop(0, seq // chunk, finish, 0)


def _dilated_attention(qa, ka, va):
    batch, seq, _ = qa.shape
    pairs = A_GROUP_WIDTH // LANES
    first, n = A_GROUP_ORDER[0], len(A_GROUPS)
    assert A_GROUP_ORDER == tuple((first + t) % n for t in range(n))
    in_spec = pl.BlockSpec((1, seq, LANES), lambda b, hp, t: (b, 0, ((first + t) % n) * pairs + hp))
    state = pltpu.VMEM((seq, LANES), jnp.float32)
    return pl.pallas_call(
        functools.partial(_dilated_kernel, seq=seq),
        grid=(batch, pairs, len(A_GROUPS)),
        in_specs=[in_spec, in_spec, in_spec],
        out_specs=pl.BlockSpec((1, seq, LANES), lambda b, hp, g: (b, 0, hp)),
        out_shape=jax.ShapeDtypeStruct((batch, seq, A_GROUP_WIDTH), jnp.bfloat16),
        scratch_shapes=[state] * 6 + [pltpu.VMEM((TOKEN_TILE, LANES), jnp.float32)],
        compiler_params=pltpu.CompilerParams(dimension_semantics=("parallel", "parallel", "arbitrary"),
                                             vmem_limit_bytes=VMEM_LIMIT),
        name="dilated",
    )(qa, ka, va)


def _mla_kernel(qt_ref, k_ref, vt_ref, o_ref, *scratch, n_groups):
    tq = qt_ref.shape[3]
    tk = vt_ref.shape[3]
    n_streams = MLA_HEADS_PER_STEP * MLA_Q_TILES_PER_STEP
    sa_ref, sb_ref, pa_ref, pb_ref = (scratch[i * n_streams:(i + 1) * n_streams] for i in range(4))
    key_blocks = tk // MLA_KEY_BLOCK
    heads = range(n_streams)
    blks = [slice((st % MLA_HEADS_PER_STEP) * LANES, (st % MLA_HEADS_PER_STEP + 1) * LANES) for st in heads]
    qts = [qt_ref[0, st // MLA_HEADS_PER_STEP, blks[st], :] for st in heads]

    def rows(c):
        return slice(c * MLA_KEY_BLOCK, (c + 1) * MLA_KEY_BLOCK)

    def qk_block(h, g, c, s_ref):
        base = g * tk if isinstance(g, int) else pl.multiple_of(g * tk, tk)
        k = k_ref[0, pl.ds(base + c * MLA_KEY_BLOCK, MLA_KEY_BLOCK), blks[h]]
        s = jnp.dot(k, qts[h], preferred_element_type=jnp.float32)
        s_ref[h][rows(c), :] = s
        return jnp.max(s, axis=0, keepdims=True)

    def pv(h, g, p_ref):
        vt = vt_ref[0, g, blks[h].start:blks[h].start + MLA_VT_ROWS, :]
        return jnp.dot(vt, p_ref[h][...], preferred_element_type=jnp.float32)

    def matmuls(g_qk, s_ref, g_pv, p_ref):
        tops, pvs = [None for _ in heads], [None for _ in heads]

        def qk(h):
            for c in range(key_blocks if g_qk is not None else 0):
                top = qk_block(h, g_qk, c, s_ref)
                tops[h] = top if tops[h] is None else jnp.maximum(tops[h], top)

        for lo in range(0, n_streams, MLA_SCORE_STREAMS_RUN):
            run = heads[lo:lo + MLA_SCORE_STREAMS_RUN]
            for h in run:
                qk(h)
            if g_pv is not None:
                for h in run:
                    pvs[h] = pv(h, g_pv, p_ref)
        return tops, pvs

    def softmax(h, m, top, s_ref, p_ref):
        m_new = jnp.maximum(m, top)
        for c in range(key_blocks):
            p_ref[h][rows(c), :] = jnp.exp2((s_ref[h][rows(c), :] - m_new).astype(p_ref[h].dtype))
        return m_new, jnp.exp2(m - m_new)

    def step(g, carry, s_read, s_write, p_read, p_write, first=False, last=False):
        tops, pvs = matmuls(None if last else g + 1, s_write, None if first else g - 1, p_read)
        new = []
        for h in heads:
            m, acc, alpha, top = carry[h]
            m_new, alpha_new = softmax(h, m, top, s_read, p_write)
            new.append((m_new, acc if first else alpha * acc + pvs[h], alpha_new, tops[h]))
        return tuple(new)

    def even(g, carry, **kw):
        return step(g, carry, sa_ref, sb_ref, pb_ref, pa_ref, **kw)

    def odd(g, carry, **kw):
        return step(g, carry, sb_ref, sa_ref, pa_ref, pb_ref, **kw)

    tops, _ = matmuls(0, sa_ref, None, None)
    init = tuple((jnp.full((1, tq), NEG_BIG, jnp.float32), jnp.zeros((MLA_VT_ROWS, tq), jnp.float32),
                  jnp.ones((1, tq), jnp.float32), tops[h]) for h in heads)
    carry = even(0, init, first=True)

    def pairs(g, carry, count):
        for i in range(count):
            carry = even(g + 2 * i + 1, odd(g + 2 * i, carry))
        return carry

    n_pairs = (n_groups - 2) // 2
    trips = n_pairs // MLA_PAIRS_PER_ITER
    carry = lax.fori_loop(
        0, trips, lambda t, c: pairs(2 * MLA_PAIRS_PER_ITER * t + 1, c, MLA_PAIRS_PER_ITER), carry)
    carry = pairs(2 * MLA_PAIRS_PER_ITER * trips + 1, carry, n_pairs - trips * MLA_PAIRS_PER_ITER)
    carry = odd(n_groups - 1, carry, last=True)
    _, pvs = matmuls(None, None, n_groups - 1, pb_ref)
    accs = [alpha * acc + pvs[h] for h, (_, acc, alpha, _) in enumerate(carry)]
    outs = [a[:B_V] / a[B_V:B_V + 1] for a in accs]
    for t in range(MLA_Q_TILES_PER_STEP):
        tile_outs = outs[t * MLA_HEADS_PER_STEP:(t + 1) * MLA_HEADS_PER_STEP]
        o_ref[0, t * tq:(t + 1) * tq, :] = jnp.concatenate(tile_outs, axis=0).T.astype(o_ref.dtype)


def _mla_attention(qbt, kb, vbt):
    batch, tiles, _, tile = qbt.shape
    seq = kb.shape[1]
    n, qt = MLA_HEADS_PER_STEP, MLA_Q_TILES_PER_STEP
    return pl.pallas_call(
        functools.partial(_mla_kernel, n_groups=tiles),
        grid=(batch, B_HEADS // n, tiles // qt),
        in_specs=[
            pl.BlockSpec((1, qt, n * LANES, tile), lambda b, hp, i: (b, i, hp, 0)),
            pl.BlockSpec((1, seq, n * LANES), lambda b, hp, i: (b, 0, hp)),
            pl.BlockSpec((1, tiles, n * LANES, tile), lambda b, hp, i: (b, 0, hp, 0)),
        ],
        out_specs=pl.BlockSpec((1, qt * tile, n * B_V), lambda b, hp, i: (b, i, hp)),
        out_shape=jax.ShapeDtypeStruct((batch, seq, B_HEADS * B_V), jnp.bfloat16),
        scratch_shapes=([pltpu.VMEM((tile, tile), jnp.float32)] * (2 * n * qt)
                        + [pltpu.VMEM((tile, tile), jnp.bfloat16)] * (2 * n * qt)),
        compiler_params=pltpu.CompilerParams(dimension_semantics=("parallel", "parallel", "parallel"),
                                             vmem_limit_bytes=VMEM_LIMIT),
        name="mla_attn",
    )(qbt, kb, vbt)


def _mix_ffn_kernel(x_ref, oa_ref, ob_ref, gate_ref, woa_ref, wob_ref, wout_ref, gpost_ref, gpre_ref,
                    w1_ref, w2_ref, gmlp_ref, out_o, *, d_model):
    tm = x_ref.shape[0]
    halves = [slice(i * tm // 2, (i + 1) * tm // 2) for i in range(2)]
    mixes = []
    for r in halves:
        o_a = jnp.dot(oa_ref[r, :], woa_ref[...], preferred_element_type=jnp.float32)
        o_b = jnp.dot(ob_ref[r, :], wob_ref[...], preferred_element_type=jnp.float32)
        gates = gate_ref[r, :].astype(jnp.float32)
        mixin = (gates[:, :d_model] * o_a + gates[:, d_model:] * o_b).astype(jnp.bfloat16)
        mixes.append(jnp.dot(mixin, wout_ref[...], preferred_element_type=jnp.float32))
    hs = [x_ref[r, :] + _rms(mix, gpost_ref[...]) for r, mix in zip(halves, mixes)]
    us = []
    for h in hs:
        hn = _rms(h, gpre_ref[...]).astype(jnp.bfloat16)
        u = jnp.dot(hn, w1_ref[...], preferred_element_type=jnp.float32)
        us.append(jnp.square(jnp.maximum(u, 0.0)).astype(jnp.bfloat16))
    ffs = [jnp.dot(u, w2_ref[...], preferred_element_type=jnp.float32) for u in us]
    for r, h, ff in zip(halves, hs, ffs):
        out_o[r, :] = h + _rms(ff, gmlp_ref[...])


def _mix_ffn(x2, oa, ob, gates, woa, wob, wout, gpost, gpre, w1, w2, gmlp, *, tm):
    t, d_model = x2.shape
    d_ff = w1.shape[1]
    row = lambda i: (i, 0)
    whole = lambda shape: pl.BlockSpec(shape, lambda i: (0, 0), pipeline_mode=pl.Buffered(1))
    return pl.pallas_call(
        functools.partial(_mix_ffn_kernel, d_model=d_model),
        grid=(t // tm,),
        in_specs=[
            pl.BlockSpec((tm, d_model), row),
            pl.BlockSpec((tm, A_GROUP_WIDTH), row),
            pl.BlockSpec((tm, B_HEADS * B_V), row),
            pl.BlockSpec((tm, 2 * d_model), row),
            whole((A_GROUP_WIDTH, d_model)),
            whole((B_HEADS * B_V, d_model)),
            whole((d_model, d_model)),
            whole((1, d_model)),
            whole((1, d_model)),
            whole((d_model, d_ff)),
            whole((d_ff, d_model)),
            whole((1, d_model)),
        ],
        out_specs=pl.BlockSpec((tm, d_model), row),
        out_shape=jax.ShapeDtypeStruct((t, d_model), jnp.float32),
        compiler_params=pltpu.CompilerParams(dimension_semantics=("parallel",), vmem_limit_bytes=VMEM_LIMIT),
        name="mix_ffn",
    )(x2, oa, ob, gates, woa, wob, wout, gpost, gpre, w1, w2, gmlp)


def _rotary_tables(seq, theta, rot_dim, period, offset):
    half = rot_dim // 2
    inv = np.float64(np.float32(theta)) ** (-(np.arange(half, dtype=np.float64) * 2.0 / rot_dim))
    ang = np.arange(seq, dtype=np.float64)[:, None] * inv[None, :]
    cos, sin = np.cos(ang), np.sin(ang)
    cos_p = np.ones((seq, period))
    sin_p = np.zeros((seq, period))
    cos_p[:, offset:offset + rot_dim] = np.concatenate([cos, cos], 1)
    sin_p[:, offset:offset + rot_dim] = np.concatenate([-sin, sin], 1)
    reps = LANES // period
    return np.tile(cos_p, (1, reps)).astype(np.float32), np.tile(sin_p, (1, reps)).astype(np.float32)


def _pad_heads(w, heads, width):
    rows = w.shape[0]
    w = w.reshape(rows, heads, width)
    return jnp.pad(w, ((0, 0), (0, 0), (0, LANES - width))).reshape(rows, heads * LANES)


def kernel(x, norm_mix_pre, w_in, b_gate, mla_q_norm, mla_w_uq, mla_kv_norm, mla_w_ukv, w_o_a, w_o_b, w_out,
           norm_mix_post, norm_mlp_pre, w_ff1, w_ff2, norm_mlp_post):
    batch, seq, d_model = x.shape
    bf = jnp.bfloat16
    x2 = x.reshape(batch * seq, d_model)

    w_main = w_in[:, :_C_KPE].astype(bf)
    w_kpe = jnp.pad(w_in[:, _C_KPE:_C_KPE + B_ROPE].astype(bf), ((0, 0), (B_NOPE, LANES - B_QK)))
    w_gate = w_in[:, _C_KPE + B_ROPE:].astype(bf)
    wuqt = _pad_heads(mla_w_uq, B_HEADS, B_QK).T.astype(bf)
    wukv = mla_w_ukv.reshape(B_KV_RANK, B_HEADS, B_NOPE + B_V)
    wuk = _pad_heads(wukv[:, :, :B_NOPE].reshape(B_KV_RANK, B_HEADS * B_NOPE), B_HEADS, B_NOPE).astype(bf)
    wuvt = _pad_heads(wukv[:, :, B_NOPE:].reshape(B_KV_RANK, B_HEADS * B_V), B_HEADS, B_V).T.astype(bf)

    ca, sa = _rotary_tables(seq, A_ROPE_THETA, A_ROT_DIM, A_HEAD_DIM, 0)
    cb, sb = _rotary_tables(seq, B_ROPE_THETA, B_ROPE, LANES, B_NOPE)
    cbt, sbt = np.ascontiguousarray(cb.T), np.ascontiguousarray(sb.T)

    vec = lambda v: v.reshape(1, -1)
    qa, ka, va, qbt, kb, vbt, gates = _in_proj(
        x2, vec(norm_mix_pre), w_main, w_kpe, w_gate, vec(b_gate), vec(mla_q_norm), wuqt, vec(mla_kv_norm), wuk, wuvt,
        ca, sa, cb, sb, cbt, sbt, seq=seq, tm=TOKEN_TILE)

    shape_a = (batch, seq, A_WIDTH)
    oa = _dilated_attention(qa.reshape(shape_a), ka.reshape(shape_a), va.reshape(shape_a))
    oa = oa.reshape(batch * seq, A_GROUP_WIDTH)

    ob = _mla_attention(qbt, kb.reshape(batch, seq, B_WIDTH), vbt)
    ob = ob.reshape(batch * seq, B_HEADS * B_V)

    out = _mix_ffn(x2, oa, ob, gates, w_o_a.astype(bf), w_o_b.astype(bf), w_out.astype(bf),
                   vec(norm_mix_post), vec(norm_mlp_pre), w_ff1.astype(bf), w_ff2.astype(bf),
                   vec(norm_mlp_post), tm=TOKEN_TILE)
    return out.reshape(batch, seq, d_model)
```
